```python
import math
import jax
import jax.numpy as jnp
from jax import lax
import numpy as np

D_MODEL = 4096
BATCH = 2
SEQ = 8192
DEPTH = 2

MEM_LEN = 256
N_MIXERS = 2
N_POOL_LAYERS = (DEPTH + N_MIXERS - 1) // N_MIXERS
N_NSA_LAYERS = DEPTH // N_MIXERS

POOL_WINDOWS = (2, 4, 8, 16)
POOL_GROUPS = len(POOL_WINDOWS)
POOL_GC = D_MODEL // POOL_GROUPS

HEAD_DIM = 128
NSA_HEADS = D_MODEL // HEAD_DIM
NSA_KV_GROUPS = 4
NSA_HPG = NSA_HEADS // NSA_KV_GROUPS
NSA_BRANCHES = 3
CMP_LEN = 32
CMP_STRIDE = 16
CMP_HIDDEN = 512
SEL_BLOCK = 64
N_SELECT = 16
WINDOW = 512
NSA_Q_BLOCK = 64
Q_WIDTH = NSA_HEADS * HEAD_DIM
KV_WIDTH = NSA_BRANCHES * 2 * NSA_KV_GROUPS * HEAD_DIM
GATE_WIDTH = NSA_BRANCHES * NSA_HEADS
NSA_IN_WIDTH = Q_WIDTH + KV_WIDTH + GATE_WIDTH
FORCED_SCORE = 1e6

XA_HEADS = 4
XA_WIDTH = XA_HEADS * HEAD_DIM

D_FF = 11008
CONV_WIDTH = 3

RMS_EPS = 1e-6
NEG_BIG = -1e30

kernel_name = "hybrid_pool_nsa_memxattn_convffn"


def rmsnorm(x, g):
    xf = x.astype(jnp.float32)
    y = xf * lax.rsqrt(jnp.mean(xf * xf, axis=-1, keepdims=True) + RMS_EPS)
    return (y * g.astype(jnp.float32)).astype(x.dtype)


def alibi_slopes(n):
    return 2.0 ** (-8.0 * jnp.arange(1, n + 1, dtype=jnp.float32) / n)


def masked_softmax(s, mask):
    s = jnp.where(mask, s, NEG_BIG)
    m = jnp.max(s, axis=-1, keepdims=True)
    e = jnp.exp(s - m) * mask
    return e / jnp.maximum(jnp.sum(e, axis=-1, keepdims=True), 1e-30)


def pool_mixer(h, w, scale):
    B, S, D = h.shape
    hf = h.astype(jnp.float32)
    csum = jnp.cumsum(hf, axis=1)
    t1 = jnp.arange(1, S + 1, dtype=jnp.float32)
    parts = []
    for g, win in enumerate(POOL_WINDOWS):
        c = csum[..., g * POOL_GC:(g + 1) * POOL_GC]
        c_prev = jnp.pad(c, ((0, 0), (win, 0), (0, 0)))[:, :S]
        cnt = jnp.minimum(t1, float(win))
        parts.append((c - c_prev) / cnt[None, :, None])
    pooled = jnp.concatenate(parts, axis=-1)
    d = (pooled - hf).astype(h.dtype).reshape(B, S, POOL_GROUPS, POOL_GC)
    y = jnp.einsum('bsgc,gcd->bsgd', d, w).reshape(B, S, D)
    return y * scale


def compress_blocks(raw, pos, w1, b1, w2):
    B, S, G, dh = raw.shape
    ratio = CMP_LEN // CMP_STRIDE
    nch = S // CMP_STRIDE
    nc = nch - ratio + 1
    chunks = raw.reshape(B, nch, CMP_STRIDE, G, dh)
    blocks = jnp.concatenate([chunks[:, r:r + nc] for r in range(ratio)], axis=2)
    blocks = blocks + pos[None, None, :, None, :]
    flat = blocks.transpose(0, 1, 3, 2, 4).reshape(B, nc, G, CMP_LEN * dh)
    hid = jax.nn.gelu(flat @ w1 + b1)
    return hid @ w2


def nsa_mixer(h, w_in, w_out, cmp_pos, cmp_w1, cmp_b1, cmp_w2):
    B, S, _ = h.shape
    G, J, dh = NSA_KV_GROUPS, NSA_HPG, HEAD_DIM
    proj = h @ w_in
    q = proj[..., :Q_WIDTH].reshape(B, S, G, J, dh)
    kv = proj[..., Q_WIDTH:Q_WIDTH + KV_WIDTH].reshape(B, S, NSA_BRANCHES, 2, G, dh)
    gates = jax.nn.sigmoid(proj[..., Q_WIDTH + KV_WIDTH:].astype(jnp.float32)).reshape(B, S, NSA_BRANCHES, G, J)

    k_cmp = compress_blocks(kv[:, :, 0, 0], cmp_pos[0], cmp_w1[0], cmp_b1[0], cmp_w2[0])
    v_cmp = compress_blocks(kv[:, :, 0, 1], cmp_pos[1], cmp_w1[1], cmp_b1[1], cmp_w2[1])
    nc = k_cmp.shape[1]
    ns = S // SEL_BLOCK
    n_top = min(N_SELECT, ns)
    k_sel = kv[:, :, 1, 0].reshape(B, ns, SEL_BLOCK, G, dh).transpose(0, 3, 1, 2, 4)
    v_sel = kv[:, :, 1, 1].reshape(B, ns, SEL_BLOCK, G, dh).transpose(0, 3, 1, 2, 4)
    k_win = jnp.pad(kv[:, :, 2, 0], ((0, 0), (WINDOW, 0), (0, 0), (0, 0)))
    v_win = jnp.pad(kv[:, :, 2, 1], ((0, 0), (WINDOW, 0), (0, 0), (0, 0)))

    slope_b = alibi_slopes(NSA_HEADS).reshape(G, J)[None, :, :, None, None]
    scale = HEAD_DIM ** -0.5
    cmp_start = jnp.arange(nc) * CMP_STRIDE
    cmp_end = (cmp_start + CMP_LEN - 1).astype(jnp.float32)
    sel_start = jnp.arange(ns) * SEL_BLOCK
    overlap = ((cmp_start[:, None] < sel_start[None, :] + SEL_BLOCK)
               & (cmp_start[:, None] + CMP_LEN > sel_start[None, :])).astype(jnp.float32)
    blk_ids = jnp.arange(ns)
    bi = jnp.arange(B)[:, None, None, None]
    gi = jnp.arange(G)[None, :, None, None]
    sel_off = jnp.arange(SEL_BLOCK)
    win_off = jnp.arange(WINDOW + NSA_Q_BLOCK)
    QB = NSA_Q_BLOCK

    def block(qb):
        start = qb * QB
        t = start + jnp.arange(QB)
        tf = t.astype(jnp.float32)
        q_blk = lax.dynamic_slice_in_dim(q, start, QB, axis=1)
        g_blk = lax.dynamic_slice_in_dim(gates, start, QB, axis=1)
        d_c = tf[:, None] - cmp_end[None, :]
        s_c = jnp.einsum('bqgjd,bcgd->bgjqc', q_blk, k_cmp).astype(jnp.float32) * scale - slope_b * d_c
        p_c = masked_softmax(s_c, d_c >= 0)
        o_c = jnp.einsum('bgjqc,bcgd->bqgjd', p_c.astype(v_cmp.dtype), v_cmp)
        score = jnp.einsum('bgqc,cn->bgqn', jnp.sum(p_c, axis=2), overlap)
        cur = t // SEL_BLOCK
        forced = (blk_ids[None, :] == 0) | (blk_ids[None, :] == cur[:, None]) | (blk_ids[None, :] == cur[:, None] - 1)
        future = blk_ids[None, :] > cur[:, None]
        score = jnp.where(future, -1.0, jnp.where(forced, FORCED_SCORE, score))
        _, idx = lax.top_k(score, n_top)
        kg = k_sel[bi, gi, idx].reshape(B, G, QB, n_top * SEL_BLOCK, dh)
        vg = v_sel[bi, gi, idx].reshape(B, G, QB, n_top * SEL_BLOCK, dh)
        pos = (idx[..., None] * SEL_BLOCK + sel_off).reshape(B, G, QB, n_top * SEL_BLOCK)
        d_s = (t[None, None, :, None] - pos).astype(jnp.float32)[:, :, None]
        s_s = jnp.einsum('bqgjd,bgqkd->bgjqk', q_blk, kg).astype(jnp.float32) * scale - slope_b * d_s
        p_s = masked_softmax(s_s, d_s >= 0)
        o_s = jnp.einsum('bgjqk,bgqkd->bqgjd', p_s.astype(vg.dtype), vg)
        kw = lax.dynamic_slice_in_dim(k_win, start, WINDOW + QB, axis=1)
        vw = lax.dynamic_slice_in_dim(v_win, start, WINDOW + QB, axis=1)
        pos_w = start - WINDOW + win_off
        d_w = t[:, None] - pos_w[None, :]
        mask_w = (d_w >= 0) & (d_w < WINDOW) & (pos_w[None, :] >= 0)
        s_w = jnp.einsum('bqgjd,bkgd->bgjqk', q_blk, kw).astype(jnp.float32) * scale - slope_b * d_w.astype(jnp.float32)
        p_w = masked_softmax(s_w, mask_w)
        o_w = jnp.einsum('bgjqk,bkgd->bqgjd', p_w.astype(vw.dtype), vw)
        g = g_blk[..., None]
        o = g[:, :, 0] * o_c + g[:, :, 1] * o_s + g[:, :, 2] * o_w
        return o.astype(h.dtype)

    o = lax.map(block, jnp.arange(S // QB))
    o = o.transpose(1, 0, 2, 3, 4, 5).reshape(B, S, Q_WIDTH)
    return o @ w_out


def mem_cross_attn(h, memn, wq, wkv, wo):
    B, S, _ = h.shape
    M = memn.shape[1]
    q = (h @ wq).reshape(B, S, XA_HEADS, HEAD_DIM)
    kv = (memn @ wkv).reshape(B, M, 2, XA_HEADS, HEAD_DIM)
    k, v = kv[:, :, 0], kv[:, :, 1]
    s = jnp.einsum('bshd,bmhd->bhsm', q, k).astype(jnp.float32) * (HEAD_DIM ** -0.5)
    p = jax.nn.softmax(s, axis=-1)
    o = jnp.einsum('bhsm,bmhd->bshd', p.astype(v.dtype), v).reshape(B, S, XA_WIDTH)
    return o @ wo


def conv_ffn(h, w_gu, conv_w, conv_b, w_down):
    gu = h @ w_gu
    gate, up = gu[..., :D_FF], gu[..., D_FF:]
    gp = jnp.pad(gate, ((0, 0), (CONV_WIDTH - 1, 0), (0, 0)))
    gate = conv_w[0] * gp[:, :-2] + conv_w[1] * gp[:, 1:-1] + conv_w[2] * gp[:, 2:] + conv_b
    return (jax.nn.silu(gate) * up) @ w_down


def setup_inputs(seed: int = 0) -> dict:
    key = jax.random.key(seed)
    ks = jax.random.split(key, 24)
    f32 = jnp.float32

    def nrm(k, shape, s):
        return jax.random.normal(k, shape, f32) * s

    return {
        "x": nrm(ks[0], (BATCH, SEQ, D_MODEL), 1.0),
        "mem": nrm(ks[1], (BATCH, MEM_LEN, D_MODEL), 1.0),
        "ln_mix": 1.0 + nrm(ks[2], (DEPTH, 2, D_MODEL), 0.05),
        "ln_xa": 1.0 + nrm(ks[3], (DEPTH, 2, D_MODEL), 0.05),
        "ln_ffn": 1.0 + nrm(ks[4], (DEPTH, 2, D_MODEL), 0.05),
        "mem_norm": 1.0 + nrm(ks[5], (D_MODEL,), 0.05),
        "pool_w": nrm(ks[6], (N_POOL_LAYERS, POOL_GROUPS, POOL_GC, POOL_GC), POOL_GC ** -0.5),
        "pool_scale": 1.0 + nrm(ks[7], (N_POOL_LAYERS, D_MODEL), 0.1),
        "nsa_w_in": nrm(ks[8], (N_NSA_LAYERS, D_MODEL, NSA_IN_WIDTH), D_MODEL ** -0.5),
        "nsa_w_out": nrm(ks[9], (N_NSA_LAYERS, Q_WIDTH, D_MODEL), Q_WIDTH ** -0.5),
        "nsa_cmp_pos": nrm(ks[10], (N_NSA_LAYERS, 2, CMP_LEN, HEAD_DIM), 0.5),
        "nsa_cmp_w1": nrm(ks[11], (N_NSA_LAYERS, 2, CMP_LEN * HEAD_DIM, CMP_HIDDEN), (CMP_LEN * HEAD_DIM) ** -0.5),
        "nsa_cmp_b1": nrm(ks[12], (N_NSA_LAYERS, 2, CMP_HIDDEN), 0.01),
        "nsa_cmp_w2": nrm(ks[13], (N_NSA_LAYERS, 2, CMP_HIDDEN, HEAD_DIM), CMP_HIDDEN ** -0.5),
        "xa_wq": nrm(ks[14], (DEPTH, D_MODEL, XA_WIDTH), D_MODEL ** -0.5),
        "xa_wkv": nrm(ks[15], (DEPTH, D_MODEL, 2 * XA_WIDTH), D_MODEL ** -0.5),
        "xa_wo": nrm(ks[16], (DEPTH, XA_WIDTH, D_MODEL), XA_WIDTH ** -0.5),
        "ffn_w_gu": nrm(ks[17], (DEPTH, D_MODEL, 2 * D_FF), D_MODEL ** -0.5),
        "ffn_conv_w": nrm(ks[18], (DEPTH, CONV_WIDTH, D_FF), CONV_WIDTH ** -0.5),
        "ffn_conv_b": nrm(ks[19], (DEPTH, D_FF), 0.01),
        "ffn_w_down": nrm(ks[20], (DEPTH, D_FF, D_MODEL), D_FF ** -0.5),
    }


def reference(x, mem, ln_mix, ln_xa, ln_ffn, mem_norm, pool_w, pool_scale, nsa_w_in, nsa_w_out,
              nsa_cmp_pos, nsa_cmp_w1, nsa_cmp_b1, nsa_cmp_w2, xa_wq, xa_wkv, xa_wo,
              ffn_w_gu, ffn_conv_w, ffn_conv_b, ffn_w_down):
    memn = rmsnorm(mem, mem_norm)
    h = x
    for i in range(DEPTH):
        j = i // N_MIXERS
        a = rmsnorm(h, ln_mix[i, 0])
        if i % N_MIXERS == 0:
            a = pool_mixer(a, pool_w[j], pool_scale[j])
        else:
            a = nsa_mixer(a, nsa_w_in[j], nsa_w_out[j], nsa_cmp_pos[j], nsa_cmp_w1[j],
                          nsa_cmp_b1[j], nsa_cmp_w2[j])
        h = h + rmsnorm(a, ln_mix[i, 1])
        c = mem_cross_attn(rmsnorm(h, ln_xa[i, 0]), memn, xa_wq[i], xa_wkv[i], xa_wo[i])
        h = h + rmsnorm(c, ln_xa[i, 1])
        f = conv_ffn(rmsnorm(h, ln_ffn[i, 0]), ffn_w_gu[i], ffn_conv_w[i], ffn_conv_b[i], ffn_w_down[i])
        h = h + rmsnorm(f, ln_ffn[i, 1])
    return h
```

```python
import functools

import jax
import jax.numpy as jnp
from jax import lax
from jax.experimental import pallas as pl
from jax.experimental.pallas import tpu as pltpu

F32 = jnp.float32
BF16 = jnp.bfloat16

RMS_EPS = 1e-6
NEG_BIG = -1e30
HEAD_DIM = 128
POOL_WINDOWS = (2, 4, 8, 16)
POOL_HALO = 16
NSA_KV_GROUPS = 4
NSA_HPG = 8
NSA_BRANCHES = 3
CMP_LEN = 32
CMP_STRIDE = 16
SEL_BLOCK = 64
N_SELECT = 16
WINDOW = 512
FORCED_SCORE = 1e6
XA_HEADS = 4
CONV_HALO = 8
VMEM_LIMIT = 56 * 1024 * 1024
NORM_CHUNK = 128

_NT = (((1,), (1,)), ((), ()))


def _params(*sem):
    return pltpu.CompilerParams(dimension_semantics=sem, vmem_limit_bytes=VMEM_LIMIT)


def _rms(x, g):
    ms = jnp.mean(x * x, axis=-1, keepdims=True)
    return x * lax.rsqrt(ms + RMS_EPS) * g


def _norm_rows(h_ref, g_ref, a_scr):
    tm = h_ref.shape[0]
    chunk = min(NORM_CHUNK, tm)

    def body(c, carry):
        r = pl.multiple_of(c * chunk, chunk)
        a_scr[pl.ds(r, chunk), :] = _rms(h_ref[pl.ds(r, chunk), :], g_ref[...]).astype(BF16)
        return carry

    lax.fori_loop(0, tm // chunk, body, 0)


def _once(shape, index_map):
    return pl.BlockSpec(shape, index_map, pipeline_mode=pl.Buffered(1))


def _norm_mm_kernel(h_ref, g_ref, w_ref, o_ref, a_scr, *, epilogue):
    @pl.when(pl.program_id(1) == 0)
    def _():
        _norm_rows(h_ref, g_ref, a_scr)

    acc = jnp.dot(a_scr[...], w_ref[...], preferred_element_type=F32)
    if epilogue == "slabs":
        for s in range(o_ref.shape[0]):
            o_ref[s] = acc[:, s * HEAD_DIM:(s + 1) * HEAD_DIM].astype(o_ref.dtype)
    elif epilogue == "sigmoid":
        o_ref[...] = jax.nn.sigmoid(acc).astype(o_ref.dtype)
    else:
        o_ref[...] = acc.astype(o_ref.dtype)


def norm_matmul(h, g, w, *, tm, tn, out_dtype, epilogue="plain"):
    M, D = h.shape
    N = w.shape[1]
    assert M % tm == 0 and N % tn == 0
    if epilogue == "slabs":
        ns = tn // HEAD_DIM
        out_shape = jax.ShapeDtypeStruct((N // HEAD_DIM, M, HEAD_DIM), out_dtype)
        out_spec = pl.BlockSpec((ns, tm, HEAD_DIM), lambda i, j: (j, i, 0))
    else:
        out_shape = jax.ShapeDtypeStruct((M, N), out_dtype)
        out_spec = pl.BlockSpec((tm, tn), lambda i, j: (i, j))
    return pl.pallas_call(
        functools.partial(_norm_mm_kernel, epilogue=epilogue),
        grid=(M // tm, N // tn),
        in_specs=[
            _once((tm, D), lambda i, j: (i, 0)),
            pl.BlockSpec((1, D), lambda i, j: (0, 0)),
            pl.BlockSpec((D, tn), lambda i, j: (0, j)),
        ],
        out_specs=out_spec,
        out_shape=out_shape,
        scratch_shapes=[pltpu.VMEM((tm, D), BF16)],
        compiler_params=_params("parallel", "arbitrary"),
        name="norm_matmul_" + epilogue,
    )(h, g.reshape(1, D), w)


def _mm_kernel(a_ref, w_ref, o_ref):
    o_ref[...] = jnp.dot(a_ref[...], w_ref[...], preferred_element_type=F32).astype(o_ref.dtype)


def matmul(a, w, *, tm, tn, out_dtype=F32):
    M, K = a.shape
    N = w.shape[1]
    assert M % tm == 0 and N % tn == 0
    return pl.pallas_call(
        _mm_kernel,
        grid=(M // tm, N // tn),
        in_specs=[
            _once((tm, K), lambda i, j: (i, 0)),
            pl.BlockSpec((K, tn), lambda i, j: (0, j)),
        ],
        out_specs=pl.BlockSpec((tm, tn), lambda i, j: (i, j)),
        out_shape=jax.ShapeDtypeStruct((M, N), out_dtype),
        compiler_params=_params("parallel", "arbitrary"),
        name="matmul",
    )(a, w)


def _resid_kernel(h_ref, f_ref, g_ref, o_ref):
    o_ref[...] = h_ref[...] + _rms(f_ref[...], g_ref[...])


def residual_norm(h, f, g, *, tm=256):
    M, D = h.shape
    row = pl.BlockSpec((tm, D), lambda i: (i, 0))
    return pl.pallas_call(
        _resid_kernel,
        grid=(M // tm,),
        in_specs=[row, row, pl.BlockSpec((1, D), lambda i: (0, 0))],
        out_specs=row,
        out_shape=jax.ShapeDtypeStruct((M, D), F32),
        compiler_params=_params("parallel"),
        name="residual_norm",
    )(h, f, g.reshape(1, D))


def _pool_kernel(x_ref, halo_ref, g0_ref, w_ref, sc_ref, g1_ref, o_ref, *, seq_len):
    tm, D = x_ref.shape
    gc = D // len(POOL_WINDOWS)
    t0 = (pl.program_id(0) * tm) % seq_len
    x = x_ref[...]
    a = _rms(x, g0_ref[...])
    ha = _rms(halo_ref[...], g0_ref[...])
    ha = jnp.where(t0 == 0, 0.0, ha)
    full = jnp.concatenate([ha, a], axis=0)
    t1 = (t0 + 1 + lax.broadcasted_iota(jnp.int32, (tm, 1), 0)).astype(F32)
    ys = []
    for gi, win in enumerate(POOL_WINDOWS):
        s = full[:, gi * gc:(gi + 1) * gc]
        k = 1
        while k < win:
            s = s + pltpu.roll(s, k, 0)
            k *= 2
        inv_cnt = 1.0 / jnp.minimum(t1, float(win))
        pooled = s[POOL_HALO:] * inv_cnt
        d = (pooled - a[:, gi * gc:(gi + 1) * gc]).astype(BF16)
        ys.append(jnp.dot(d, w_ref[gi], preferred_element_type=F32))
    y = jnp.concatenate(ys, axis=-1) * sc_ref[...]
    o_ref[...] = x + _rms(y, g1_ref[...])


def pool_layer(x, g0, w, scale, g1, *, seq_len, tm=256):
    M, D = x.shape
    G, gc, _ = w.shape
    vec = pl.BlockSpec((1, D), lambda i: (0, 0))
    hb = tm // POOL_HALO
    return pl.pallas_call(
        functools.partial(_pool_kernel, seq_len=seq_len),
        grid=(M // tm,),
        in_specs=[
            pl.BlockSpec((tm, D), lambda i: (i, 0)),
            pl.BlockSpec((POOL_HALO, D), lambda i: (jnp.maximum(i * hb - 1, 0), 0)),
            vec,
            _once((G, gc, gc), lambda i: (0, 0, 0)),
            vec,
            vec,
        ],
        out_specs=pl.BlockSpec((tm, D), lambda i: (i, 0)),
        out_shape=jax.ShapeDtypeStruct((M, D), F32),
        compiler_params=_params("parallel"),
        name="pool_layer",
    )(x, x, g0.reshape(1, D), w, scale.reshape(1, D), g1.reshape(1, D))


def _xattn_kernel(h_ref, g0_ref, wq_ref, k_ref, v_ref, wo_ref, g1_ref, o_ref):
    h = h_ref[...]
    n = _rms(h, g0_ref[...]).astype(BF16)
    q = jnp.dot(n, wq_ref[...], preferred_element_type=F32).astype(BF16)
    k = k_ref[0]
    v = v_ref[0]
    scale = HEAD_DIM ** -0.5
    outs = []
    for hh in range(XA_HEADS):
        sl = slice(hh * HEAD_DIM, (hh + 1) * HEAD_DIM)
        s = lax.dot_general(q[:, sl], k[:, sl], _NT, preferred_element_type=F32) * scale
        m = jnp.max(s, axis=-1, keepdims=True)
        e = jnp.exp(s - m)
        p = e * (1.0 / jnp.sum(e, axis=-1, keepdims=True))
        outs.append(jnp.dot(p.astype(BF16), v[:, sl], preferred_element_type=F32))
    o = jnp.concatenate(outs, axis=-1).astype(BF16)
    c = jnp.dot(o, wo_ref[...], preferred_element_type=F32)
    o_ref[...] = h + _rms(c, g1_ref[...])


def xattn_layer(h, g0, wq, kv, wo, g1, *, seq_len, tm=256):
    M, D = h.shape
    xw = wq.shape[1]
    mem_len = kv.shape[1]
    per_seq = seq_len // tm
    vec = pl.BlockSpec((1, D), lambda i: (0, 0))
    return pl.pallas_call(
        _xattn_kernel,
        grid=(M // tm,),
        in_specs=[
            pl.BlockSpec((tm, D), lambda i: (i, 0)),
            vec,
            _once((D, xw), lambda i: (0, 0)),
            pl.BlockSpec((1, mem_len, xw), lambda i: (i // per_seq, 0, 0)),
            pl.BlockSpec((1, mem_len, xw), lambda i: (i // per_seq, 0, 1)),
            _once((xw, D), lambda i: (0, 0)),
            vec,
        ],
        out_specs=pl.BlockSpec((tm, D), lambda i: (i, 0)),
        out_shape=jax.ShapeDtypeStruct((M, D), F32),
        compiler_params=_params("parallel"),
        name="xattn_layer",
    )(h, g0.reshape(1, D), wq, kv, kv, wo, g1.reshape(1, D))


def _ffn_up_kernel(h_ref, halo_ref, g_ref, wg_ref, wu_ref, cw_ref, cb_ref, o_ref, a_scr, halo_scr, *, seq_len):
    tm = h_ref.shape[0]

    @pl.when(pl.program_id(1) == 0)
    def _():
        _norm_rows(h_ref, g_ref, a_scr)
        t0 = (pl.program_id(0) * tm) % seq_len
        hn = _rms(halo_ref[...], g_ref[...])
        halo_scr[...] = jnp.where(t0 == 0, 0.0, hn).astype(BF16)

    a = a_scr[...]
    gate = jnp.dot(a, wg_ref[...], preferred_element_type=F32)
    up = jnp.dot(a, wu_ref[...], preferred_element_type=F32)
    hg = jnp.dot(halo_scr[...], wg_ref[...], preferred_element_type=F32)
    row = lax.broadcasted_iota(jnp.int32, (tm, 1), 0)
    prev1 = hg[CONV_HALO - 1:CONV_HALO]
    prev2 = hg[CONV_HALO - 2:CONV_HALO - 1]
    g1 = jnp.where(row == 0, prev1, pltpu.roll(gate, 1, 0))
    g2 = jnp.where(row == 0, prev2, jnp.where(row == 1, prev1, pltpu.roll(gate, 2, 0)))
    cw = cw_ref[...]
    gc = cw[0:1] * g2 + cw[1:2] * g1 + cw[2:3] * gate + cb_ref[...]
    o_ref[...] = (gc * jax.nn.sigmoid(gc) * up).astype(o_ref.dtype)


def ffn_up(h, g, w_gu, conv_w, conv_b, *, seq_len, tm=1024, tn=256):
    M, D = h.shape
    F = w_gu.shape[1] // 2
    assert F % tn == 0 and M % tm == 0 and seq_len % tm == 0
    nj = F // tn
    hb = tm // CONV_HALO
    return pl.pallas_call(
        functools.partial(_ffn_up_kernel, seq_len=seq_len),
        grid=(M // tm, nj),
        in_specs=[
            _once((tm, D), lambda i, j: (i, 0)),
            _once((CONV_HALO, D), lambda i, j: (jnp.maximum(i * hb - 1, 0), 0)),
            pl.BlockSpec((1, D), lambda i, j: (0, 0)),
            pl.BlockSpec((D, tn), lambda i, j: (0, j)),
            pl.BlockSpec((D, tn), lambda i, j: (0, j + nj)),
            pl.BlockSpec((conv_w.shape[0], tn), lambda i, j: (0, j)),
            pl.BlockSpec((1, tn), lambda i, j: (0, j)),
        ],
        out_specs=pl.BlockSpec((tm, tn), lambda i, j: (i, j)),
        out_shape=jax.ShapeDtypeStruct((M, F), BF16),
        scratch_shapes=[pltpu.VMEM((tm, D), BF16), pltpu.VMEM((CONV_HALO, D), BF16)],
        compiler_params=_params("parallel", "arbitrary"),
        name="ffn_up",
    )(h, h, g.reshape(1, D), w_gu, w_gu, conv_w, conv_b.reshape(1, F))


def _gelu_tanh(x):
    return 0.5 * x * (1.0 + jnp.tanh(0.7978845608028654 * (x + 0.044715 * x * x * x)))


def _compress_kernel(r_ref, pos_ref, w1_ref, b1_ref, w2_ref, o_ref):
    r = r_ref[0, 0].astype(F32)
    nch, half = r.shape
    pos = pos_ref[0]
    xa = (r + pos[0:1]).astype(BF16)
    xb = (r + pos[1:2]).astype(BF16)
    h1 = jnp.dot(xa, w1_ref[0, :half, :], preferred_element_type=F32)
    h2 = jnp.dot(xb, w1_ref[0, half:, :], preferred_element_type=F32)
    hid = h1 + pltpu.roll(h2, nch - 1, 0) + b1_ref[0]
    out = jnp.dot(_gelu_tanh(hid).astype(BF16), w2_ref[0], preferred_element_type=F32)
    row = lax.broadcasted_iota(jnp.int32, (nch, 1), 0)
    o_ref[0, 0, 0] = jnp.where(row < nch - 1, out, 0.0).astype(o_ref.dtype)


def compress(kv_slabs, pos, w1, b1, w2, *, batch, seq_len):
    G = NSA_KV_GROUPS
    nch = seq_len // CMP_STRIDE
    half = CMP_STRIDE * HEAD_DIM
    r = kv_slabs.reshape(kv_slabs.shape[0], batch, nch, half)
    hidden = w1.shape[-1]
    return pl.pallas_call(
        _compress_kernel,
        grid=(2, batch, G),
        in_specs=[
            pl.BlockSpec((1, 1, nch, half), lambda kv, b, g: (kv * G + g, b, 0, 0)),
            pl.BlockSpec((1, 2, half), lambda kv, b, g: (kv, 0, 0)),
            pl.BlockSpec((1, 2 * half, hidden), lambda kv, b, g: (kv, 0, 0)),
            pl.BlockSpec((1, 1, hidden), lambda kv, b, g: (kv, 0, 0)),
            pl.BlockSpec((1, hidden, HEAD_DIM), lambda kv, b, g: (kv, 0, 0)),
        ],
        out_specs=pl.BlockSpec((1, 1, 1, nch, HEAD_DIM), lambda kv, b, g: (kv, b, g, 0, 0)),
        out_shape=jax.ShapeDtypeStruct((2, batch, G, nch, HEAD_DIM), BF16),
        compiler_params=_params("parallel", "parallel", "parallel"),
        name="nsa_compress",
    )(r, pos.reshape(2, 2, half), w1, b1.reshape(2, 1, hidden), w2)


def _split3(x):
    hi = x.astype(BF16)
    r1 = x - hi.astype(F32)
    mid = r1.astype(BF16)
    lo = (r1 - mid.astype(F32)).astype(BF16)
    return hi, mid, lo


def _masked_softmax_rows(s, mask):
    s = jnp.where(mask, s, NEG_BIG)
    m = jnp.max(s, axis=-1, keepdims=True)
    e = jnp.where(mask, jnp.exp(s - m), 0.0)
    return e * (1.0 / jnp.maximum(jnp.sum(e, axis=-1, keepdims=True), 1e-30))


def _nsa_kernel(slope_ref, q_ref, gate_ref, kc_ref, vc_ref, ks_ref, vs_ref, kw_ref, vw_ref, o_ref,
                p_scr, m_scr, l_scr, acc_scr, *, kt):
    J, dh = NSA_HPG, HEAD_DIM
    qb = q_ref.shape[0]
    seq_len = ks_ref.shape[2]
    ncp = kc_ref.shape[3]
    ns = seq_len // SEL_BLOCK
    g = pl.program_id(1)
    start = pl.program_id(2) * qb
    scale = dh ** -0.5

    q = q_ref[...]
    qr = jnp.concatenate([q[:, j * dh:(j + 1) * dh] for j in range(J)], axis=0)
    slopes = [slope_ref[g * J + j] for j in range(J)]
    t_col = start + lax.broadcasted_iota(jnp.int32, (qb, 1), 0)

    kc = kc_ref[0, 0, 0]
    vc = vc_ref[0, 0, 0]
    sc = lax.dot_general(qr, kc, _NT, preferred_element_type=F32)
    c_end = lax.broadcasted_iota(jnp.int32, (1, ncp), 1) * CMP_STRIDE + (CMP_LEN - 1)
    d_c = (t_col - c_end).astype(F32)
    mask_c = d_c >= 0.0
    psum = jnp.zeros((qb, ncp), F32)
    for j in range(J):
        p = _masked_softmax_rows(sc[j * qb:(j + 1) * qb] * scale - slopes[j] * d_c, mask_c)
        psum = psum + p
        p_scr[j * qb:(j + 1) * qb, :ncp] = p.astype(BF16)
    o_cmp = jnp.dot(p_scr[:, :ncp], vc, preferred_element_type=F32)

    n_row = lax.broadcasted_iota(jnp.int32, (ns, ncp), 0) * SEL_BLOCK
    c_col = lax.broadcasted_iota(jnp.int32, (ns, ncp), 1) * CMP_STRIDE
    overlap_t = ((c_col < n_row + SEL_BLOCK) & (c_col + CMP_LEN > n_row)
                 & (c_col < (ncp - 1) * CMP_STRIDE)).astype(BF16)
    score_t = jnp.zeros((ns, qb), F32)
    for part in _split3(psum):
        score_t = score_t + lax.dot_general(overlap_t, part, _NT, preferred_element_type=F32)
    n_idx = lax.broadcasted_iota(jnp.int32, (ns, qb), 0)
    cur = (start + lax.broadcasted_iota(jnp.int32, (1, qb), 1)) // SEL_BLOCK
    forced = (n_idx == 0) | (n_idx == cur) | (n_idx == cur - 1)
    score_t = jnp.where(n_idx > cur, -1.0, jnp.where(forced, FORCED_SCORE, score_t))
    sel_t = jnp.zeros((ns, qb), F32)
    for _ in range(min(N_SELECT, ns)):
        best = jnp.max(score_t, axis=0, keepdims=True)
        first = jnp.min(jnp.where(score_t == best, n_idx, ns), axis=0, keepdims=True)
        hit = n_idx == first
        sel_t = jnp.where(hit, 1.0, sel_t)
        score_t = jnp.where(hit, -2.0, score_t)
    sel = sel_t.T.astype(BF16)

    m_scr[...] = jnp.full(m_scr.shape, NEG_BIG, F32)
    l_scr[...] = jnp.zeros(l_scr.shape, F32)
    acc_scr[...] = jnp.zeros(acc_scr.shape, F32)

    def sel_tile(i, carry):
        k0 = pl.multiple_of(i * kt, kt)
        ks = ks_ref[0, 0, pl.ds(k0, kt), :]
        vs = vs_ref[0, 0, pl.ds(k0, kt), :]
        s = lax.dot_general(qr, ks, _NT, preferred_element_type=F32)
        pos = k0 + lax.broadcasted_iota(jnp.int32, (1, kt), 1)
        expand = (lax.broadcasted_iota(jnp.int32, (ns, kt), 0)
                  == (k0 + lax.broadcasted_iota(jnp.int32, (ns, kt), 1)) // SEL_BLOCK).astype(BF16)
        chosen = jnp.dot(sel, expand, preferred_element_type=F32)
        d_s = (t_col - pos).astype(F32)
        mask = (chosen > 0.5) & (d_s >= 0.0)
        for j in range(J):
            rows = slice(j * qb, (j + 1) * qb)
            sj = jnp.where(mask, s[rows] * scale - slopes[j] * d_s, NEG_BIG)
            m_old = m_scr[rows]
            m_new = jnp.maximum(m_old, jnp.max(sj, axis=-1, keepdims=True))
            e = jnp.where(mask, jnp.exp(sj - m_new), 0.0)
            alpha = jnp.exp(m_old - m_new)
            l_scr[rows] = alpha * l_scr[rows] + jnp.sum(e, axis=-1, keepdims=True)
            acc_scr[rows] = alpha * acc_scr[rows] + jnp.dot(e.astype(BF16), vs, preferred_element_type=F32)
            m_scr[rows] = m_new
        return carry

    lax.fori_loop(0, (start + qb + kt - 1) // kt, sel_tile, 0)

    wl = WINDOW + qb
    w0 = pl.multiple_of(jnp.maximum(start - WINDOW, 0), qb)
    kw = kw_ref[0, 0, pl.ds(w0, wl), :]
    vw = vw_ref[0, 0, pl.ds(w0, wl), :]
    sw = lax.dot_general(qr, kw, _NT, preferred_element_type=F32)
    d_w = (t_col - (w0 + lax.broadcasted_iota(jnp.int32, (1, wl), 1))).astype(F32)
    mask_w = (d_w >= 0.0) & (d_w < float(WINDOW))
    for j in range(J):
        p = _masked_softmax_rows(sw[j * qb:(j + 1) * qb] * scale - slopes[j] * d_w, mask_w)
        p_scr[j * qb:(j + 1) * qb, :wl] = p.astype(BF16)
    o_win = jnp.dot(p_scr[:, :wl], vw, preferred_element_type=F32)

    gate = gate_ref[0]
    for j in range(J):
        rows = slice(j * qb, (j + 1) * qb)
        o_sel = acc_scr[rows] * (1.0 / jnp.maximum(l_scr[rows], 1e-30))
        o = (gate[:, j:j + 1] * o_cmp[rows] + gate[:, J + j:J + j + 1] * o_sel
             + gate[:, 2 * J + j:2 * J + j + 1] * o_win[rows])
        o_ref[:, j * dh:(j + 1) * dh] = o.astype(o_ref.dtype)


def nsa_attention(q, gates, kv_cmp, kv_slabs, slopes, *, batch, seq_len, qb=128, kt=512):
    G, J, dh = NSA_KV_GROUPS, NSA_HPG, HEAD_DIM
    M = q.shape[0]
    nq = seq_len // qb
    ncp = kv_cmp.shape[3]
    assert seq_len % kt == 0 and kt % qb == 0 and seq_len >= WINDOW + qb
    slabs = kv_slabs.reshape(kv_slabs.shape[0], batch, seq_len, dh)

    def slab(branch, is_v):
        base = (branch * 2 + is_v) * G
        return pl.BlockSpec((1, 1, seq_len, dh), lambda b, g, i, sl: (base + g, b, 0, 0))

    def cmp_spec(is_v):
        return pl.BlockSpec((1, 1, 1, ncp, dh), lambda b, g, i, sl: (is_v, b, g, 0, 0))

    grid_spec = pltpu.PrefetchScalarGridSpec(
        num_scalar_prefetch=1,
        grid=(batch, G, nq),
        in_specs=[
            pl.BlockSpec((qb, J * dh), lambda b, g, i, sl: (b * nq + i, g)),
            pl.BlockSpec((1, qb, NSA_BRANCHES * J), lambda b, g, i, sl: (g, b * nq + i, 0)),
            cmp_spec(0), cmp_spec(1),
            slab(1, 0), slab(1, 1), slab(2, 0), slab(2, 1),
        ],
        out_specs=pl.BlockSpec((qb, J * dh), lambda b, g, i, sl: (b * nq + i, g)),
        scratch_shapes=[
            pltpu.VMEM((J * qb, max(ncp, WINDOW + qb)), BF16),
            pltpu.VMEM((J * qb, 1), F32),
            pltpu.VMEM((J * qb, 1), F32),
            pltpu.VMEM((J * qb, dh), F32),
        ],
    )
    return pl.pallas_call(
        functools.partial(_nsa_kernel, kt=kt),
        grid_spec=grid_spec,
        out_shape=jax.ShapeDtypeStruct((M, G * J * dh), BF16),
        compiler_params=_params("parallel", "parallel", "arbitrary"),
        name="nsa_attention",
    )(slopes, q, gates, kv_cmp, kv_cmp, slabs, slabs, slabs, slabs)


def _ffn_layer(h, g0, g1, w_gu, conv_w, conv_b, w_down, *, seq_len):
    act = ffn_up(h, g0, w_gu.astype(BF16), conv_w, conv_b, seq_len=seq_len)
    f = matmul(act, w_down.astype(BF16), tm=512, tn=512)
    return residual_norm(h, f, g1)


def _xattn(h, memn_src, mem_norm, g0, g1, wq, wkv, wo, *, batch, seq_len):
    mem_len = memn_src.shape[0] // batch
    kv = norm_matmul(memn_src, mem_norm, wkv.astype(BF16), tm=memn_src.shape[0], tn=wkv.shape[1] // 2,
                     out_dtype=BF16)
    kv = kv.reshape(batch, mem_len, wkv.shape[1])
    return xattn_layer(h, g0, wq.astype(BF16), kv, wo.astype(BF16), g1, seq_len=seq_len)


def _nsa_layer(h, g0, g1, w_in, w_out, cmp_pos, cmp_w1, cmp_b1, cmp_w2, *, batch, seq_len):
    G, J, dh = NSA_KV_GROUPS, NSA_HPG, HEAD_DIM
    qw = G * J * dh
    kvw = NSA_BRANCHES * 2 * G * dh
    q = norm_matmul(h, g0, w_in[:, :qw].astype(BF16), tm=1024, tn=512, out_dtype=BF16)
    kv_slabs = norm_matmul(h, g0, w_in[:, qw:qw + kvw].astype(BF16), tm=1024, tn=512, out_dtype=BF16,
                           epilogue="slabs")
    gates = norm_matmul(h, g0, w_in[:, qw + kvw:].astype(BF16), tm=1024, tn=w_in.shape[1] - qw - kvw,
                        out_dtype=F32, epilogue="sigmoid")
    gates = gates.reshape(-1, NSA_BRANCHES, G, J).transpose(2, 0, 1, 3).reshape(G, -1, NSA_BRANCHES * J)
    kv_cmp = compress(kv_slabs, cmp_pos, cmp_w1.astype(BF16), cmp_b1, cmp_w2.astype(BF16),
                      batch=batch, seq_len=seq_len)
    n_heads = G * J
    slopes = 2.0 ** (-8.0 * jnp.arange(1, n_heads + 1, dtype=F32) / n_heads)
    o = nsa_attention(q, gates, kv_cmp, kv_slabs, slopes, batch=batch, seq_len=seq_len)
    a = matmul(o, w_out.astype(BF16), tm=1024, tn=512)
    return residual_norm(h, a, g1)


def kernel(x, mem, ln_mix, ln_xa, ln_ffn, mem_norm, pool_w, pool_scale, nsa_w_in, nsa_w_out, nsa_cmp_pos, nsa_cmp_w1, nsa_cmp_b1, nsa_cmp_w2, xa_wq, xa_wkv, xa_wo, ffn_w_gu, ffn_conv_w, ffn_conv_b, ffn_w_down):
    B, S, D = x.shape
    depth = ln_mix.shape[0]
    h = x.reshape(B * S, D)
    mem2 = mem.reshape(B * mem.shape[1], D)
    for i in range(depth):
        j = i // 2
        if i % 2 == 0:
            h = pool_layer(h, ln_mix[i, 0], pool_w[j].astype(BF16), pool_scale[j], ln_mix[i, 1], seq_len=S)
        else:
            h = _nsa_layer(h, ln_mix[i, 0], ln_mix[i, 1], nsa_w_in[j], nsa_w_out[j], nsa_cmp_pos[j],
                           nsa_cmp_w1[j], nsa_cmp_b1[j], nsa_cmp_w2[j], batch=B, seq_len=S)
        h = _xattn(h, mem2, mem_norm, ln_xa[i, 0], ln_xa[i, 1], xa_wq[i], xa_wkv[i], xa_wo[i],
                   batch=B, seq_len=S)
        h = _ffn_layer(h, ln_ffn[i, 0], ln_ffn[i, 1], ffn_w_gu[i], ffn_conv_w[i], ffn_conv_b[i],
                       ffn_w_down[i], seq_len=S)
    return h.reshape(B, S, D)
```

```python
import functools

import jax
import jax.numpy as jnp
from jax import lax
from jax.experimental import pallas as pl
from jax.experimental.pallas import tpu as pltpu

F32 = jnp.float32
BF16 = jnp.bfloat16

RMS_EPS = 1e-6
NEG_BIG = -1e30
HEAD_DIM = 128
POOL_WINDOWS = (2, 4, 8, 16)
POOL_HALO = 16
NSA_KV_GROUPS = 4
NSA_HPG = 8
NSA_BRANCHES = 3
CMP_LEN = 32
CMP_STRIDE = 16
SEL_BLOCK = 64
N_SELECT = 16
WINDOW = 512
FORCED_SCORE = 1e6
XA_HEADS = 4
CONV_HALO = 8
VMEM_LIMIT = 56 * 1024 * 1024
NORM_CHUNK = 128
LANES = 128

_NT = (((1,), (1,)), ((), ()))


def _params(*sem):
    return pltpu.CompilerParams(dimension_semantics=sem, vmem_limit_bytes=VMEM_LIMIT)


def _rms(x, g):
    ms = jnp.mean(x * x, axis=-1, keepdims=True)
    return x * lax.rsqrt(ms + RMS_EPS) * g


def _norm_rows(h_ref, g_ref, a_scr):
    tm = h_ref.shape[0]
    chunk = min(NORM_CHUNK, tm)

    def body(c, carry):
        r = pl.multiple_of(c * chunk, chunk)
        a_scr[pl.ds(r, chunk), :] = _rms(h_ref[pl.ds(r, chunk), :], g_ref[...]).astype(BF16)
        return carry

    lax.fori_loop(0, tm // chunk, body, 0)


def _once(shape, index_map):
    return pl.BlockSpec(shape, index_map, pipeline_mode=pl.Buffered(1))


def _norm_mm_kernel(h_ref, g_ref, w_ref, o_ref, a_scr, *, epilogue, out_scale):
    @pl.when(pl.program_id(1) == 0)
    def _():
        _norm_rows(h_ref, g_ref, a_scr)

    acc = jnp.dot(a_scr[...], w_ref[...], preferred_element_type=F32)
    if epilogue == "slabs":
        for s in range(o_ref.shape[0]):
            o_ref[s] = acc[:, s * HEAD_DIM:(s + 1) * HEAD_DIM].astype(o_ref.dtype)
    elif epilogue == "sigmoid":
        o_ref[...] = jax.nn.sigmoid(acc).astype(o_ref.dtype)
    else:
        o_ref[...] = (acc * out_scale).astype(o_ref.dtype)


def norm_matmul(h, g, w, *, tm, tn, out_dtype, epilogue="plain", out_scale=1.0):
    M, D = h.shape
    N = w.shape[1]
    assert M % tm == 0 and N % tn == 0
    if epilogue == "slabs":
        ns = tn // HEAD_DIM
        out_shape = jax.ShapeDtypeStruct((N // HEAD_DIM, M, HEAD_DIM), out_dtype)
        out_spec = pl.BlockSpec((ns, tm, HEAD_DIM), lambda i, j: (j, i, 0))
    else:
        out_shape = jax.ShapeDtypeStruct((M, N), out_dtype)
        out_spec = pl.BlockSpec((tm, tn), lambda i, j: (i, j))
    return pl.pallas_call(
        functools.partial(_norm_mm_kernel, epilogue=epilogue, out_scale=out_scale),
        grid=(M // tm, N // tn),
        in_specs=[
            _once((tm, D), lambda i, j: (i, 0)),
            pl.BlockSpec((1, D), lambda i, j: (0, 0)),
            pl.BlockSpec((D, tn), lambda i, j: (0, j)),
        ],
        out_specs=out_spec,
        out_shape=out_shape,
        scratch_shapes=[pltpu.VMEM((tm, D), BF16)],
        compiler_params=_params("parallel", "arbitrary"),
        name="norm_matmul_" + epilogue,
    )(h, g.reshape(1, D), w)


def _mm_kernel(a_ref, w_ref, o_ref):
    o_ref[...] = jnp.dot(a_ref[...], w_ref[...], preferred_element_type=F32).astype(o_ref.dtype)


def matmul(a, w, *, tm, tn, out_dtype=F32):
    M, K = a.shape
    N = w.shape[1]
    assert M % tm == 0 and N % tn == 0
    return pl.pallas_call(
        _mm_kernel,
        grid=(M // tm, N // tn),
        in_specs=[
            _once((tm, K), lambda i, j: (i, 0)),
            pl.BlockSpec((K, tn), lambda i, j: (0, j)),
        ],
        out_specs=pl.BlockSpec((tm, tn), lambda i, j: (i, j)),
        out_shape=jax.ShapeDtypeStruct((M, N), out_dtype),
        compiler_params=_params("parallel", "arbitrary"),
        name="matmul",
    )(a, w)


def _resid_kernel(h_ref, f_ref, g_ref, o_ref):
    o_ref[...] = h_ref[...] + _rms(f_ref[...], g_ref[...])


def residual_norm(h, f, g, *, tm=256):
    M, D = h.shape
    row = pl.BlockSpec((tm, D), lambda i: (i, 0))
    return pl.pallas_call(
        _resid_kernel,
        grid=(M // tm,),
        in_specs=[row, row, pl.BlockSpec((1, D), lambda i: (0, 0))],
        out_specs=row,
        out_shape=jax.ShapeDtypeStruct((M, D), F32),
        compiler_params=_params("parallel"),
        name="residual_norm",
    )(h, f, g.reshape(1, D))


def _pool_kernel(x_ref, halo_ref, g0_ref, w_ref, sc_ref, g1_ref, o_ref, *, seq_len):
    tm, D = x_ref.shape
    gc = D // len(POOL_WINDOWS)
    t0 = (pl.program_id(0) * tm) % seq_len
    x = x_ref[...]
    a = _rms(x, g0_ref[...])
    ha = _rms(halo_ref[...], g0_ref[...])
    ha = jnp.where(t0 == 0, 0.0, ha)
    full = jnp.concatenate([ha, a], axis=0)
    t1 = (t0 + 1 + lax.broadcasted_iota(jnp.int32, (tm, 1), 0)).astype(F32)
    ys = []
    for gi, win in enumerate(POOL_WINDOWS):
        s = full[:, gi * gc:(gi + 1) * gc]
        k = 1
        while k < win:
            s = s + pltpu.roll(s, k, 0)
            k *= 2
        inv_cnt = 1.0 / jnp.minimum(t1, float(win))
        pooled = s[POOL_HALO:] * inv_cnt
        d = (pooled - a[:, gi * gc:(gi + 1) * gc]).astype(BF16)
        ys.append(jnp.dot(d, w_ref[gi], preferred_element_type=F32))
    y = jnp.concatenate(ys, axis=-1) * sc_ref[...]
    o_ref[...] = x + _rms(y, g1_ref[...])


def pool_layer(x, g0, w, scale, g1, *, seq_len, tm=256):
    M, D = x.shape
    G, gc, _ = w.shape
    vec = pl.BlockSpec((1, D), lambda i: (0, 0))
    hb = tm // POOL_HALO
    return pl.pallas_call(
        functools.partial(_pool_kernel, seq_len=seq_len),
        grid=(M // tm,),
        in_specs=[
            pl.BlockSpec((tm, D), lambda i: (i, 0)),
            pl.BlockSpec((POOL_HALO, D), lambda i: (jnp.maximum(i * hb - 1, 0), 0)),
            vec,
            _once((G, gc, gc), lambda i: (0, 0, 0)),
            vec,
            vec,
        ],
        out_specs=pl.BlockSpec((tm, D), lambda i: (i, 0)),
        out_shape=jax.ShapeDtypeStruct((M, D), F32),
        compiler_params=_params("parallel"),
        name="pool_layer",
    )(x, x, g0.reshape(1, D), w, scale.reshape(1, D), g1.reshape(1, D))


def _xattn_kernel(h_ref, g0_ref, wq_ref, k_ref, v_ref, wo_ref, g1_ref, o_ref):
    h = h_ref[...]
    n = _rms(h, g0_ref[...]).astype(BF16)
    q = jnp.dot(n, wq_ref[...], preferred_element_type=F32).astype(BF16)
    k = k_ref[0]
    v = v_ref[0]
    scale = HEAD_DIM ** -0.5
    outs = []
    for hh in range(XA_HEADS):
        sl = slice(hh * HEAD_DIM, (hh + 1) * HEAD_DIM)
        s = lax.dot_general(q[:, sl], k[:, sl], _NT, preferred_element_type=F32) * scale
        m = jnp.max(s, axis=-1, keepdims=True)
        e = jnp.exp(s - m)
        p = e * (1.0 / jnp.sum(e, axis=-1, keepdims=True))
        outs.append(jnp.dot(p.astype(BF16), v[:, sl], preferred_element_type=F32))
    o = jnp.concatenate(outs, axis=-1).astype(BF16)
    c = jnp.dot(o, wo_ref[...], preferred_element_type=F32)
    o_ref[...] = h + _rms(c, g1_ref[...])


def xattn_layer(h, g0, wq, kv, wo, g1, *, seq_len, tm=256):
    M, D = h.shape
    xw = wq.shape[1]
    mem_len = kv.shape[1]
    per_seq = seq_len // tm
    vec = pl.BlockSpec((1, D), lambda i: (0, 0))
    return pl.pallas_call(
        _xattn_kernel,
        grid=(M // tm,),
        in_specs=[
            pl.BlockSpec((tm, D), lambda i: (i, 0)),
            vec,
            _once((D, xw), lambda i: (0, 0)),
            pl.BlockSpec((1, mem_len, xw), lambda i: (i // per_seq, 0, 0)),
            pl.BlockSpec((1, mem_len, xw), lambda i: (i // per_seq, 0, 1)),
            _once((xw, D), lambda i: (0, 0)),
            vec,
        ],
        out_specs=pl.BlockSpec((tm, D), lambda i: (i, 0)),
        out_shape=jax.ShapeDtypeStruct((M, D), F32),
        compiler_params=_params("parallel"),
        name="xattn_layer",
    )(h, g0.reshape(1, D), wq, kv, kv, wo, g1.reshape(1, D))


def _ffn_up_kernel(h_ref, halo_ref, g_ref, wg_ref, wu_ref, cw_ref, cb_ref, o_ref, a_scr, halo_scr, *, seq_len):
    tm = h_ref.shape[0]

    @pl.when(pl.program_id(1) == 0)
    def _():
        _norm_rows(h_ref, g_ref, a_scr)
        t0 = (pl.program_id(0) * tm) % seq_len
        hn = _rms(halo_ref[...], g_ref[...])
        halo_scr[...] = jnp.where(t0 == 0, 0.0, hn).astype(BF16)

    a = a_scr[...]
    gate = jnp.dot(a, wg_ref[...], preferred_element_type=F32)
    up = jnp.dot(a, wu_ref[...], preferred_element_type=F32)
    hg = jnp.dot(halo_scr[...], wg_ref[...], preferred_element_type=F32)
    row = lax.broadcasted_iota(jnp.int32, (tm, 1), 0)
    prev1 = hg[CONV_HALO - 1:CONV_HALO]
    prev2 = hg[CONV_HALO - 2:CONV_HALO - 1]
    g1 = jnp.where(row == 0, prev1, pltpu.roll(gate, 1, 0))
    g2 = jnp.where(row == 0, prev2, jnp.where(row == 1, prev1, pltpu.roll(gate, 2, 0)))
    cw = cw_ref[...]
    gc = cw[0:1] * g2 + cw[1:2] * g1 + cw[2:3] * gate + cb_ref[...]
    o_ref[...] = (gc * jax.nn.sigmoid(gc) * up).astype(o_ref.dtype)


def ffn_up(h, g, w_gu, conv_w, conv_b, *, seq_len, tm=1024, tn=256):
    M, D = h.shape
    F = w_gu.shape[1] // 2
    assert F % tn == 0 and M % tm == 0 and seq_len % tm == 0
    nj = F // tn
    hb = tm // CONV_HALO
    return pl.pallas_call(
        functools.partial(_ffn_up_kernel, seq_len=seq_len),
        grid=(M // tm, nj),
        in_specs=[
            _once((tm, D), lambda i, j: (i, 0)),
            _once((CONV_HALO, D), lambda i, j: (jnp.maximum(i * hb - 1, 0), 0)),
            pl.BlockSpec((1, D), lambda i, j: (0, 0)),
            pl.BlockSpec((D, tn), lambda i, j: (0, j)),
            pl.BlockSpec((D, tn), lambda i, j: (0, j + nj)),
            pl.BlockSpec((conv_w.shape[0], tn), lambda i, j: (0, j)),
            pl.BlockSpec((1, tn), lambda i, j: (0, j)),
        ],
        out_specs=pl.BlockSpec((tm, tn), lambda i, j: (i, j)),
        out_shape=jax.ShapeDtypeStruct((M, F), BF16),
        scratch_shapes=[pltpu.VMEM((tm, D), BF16), pltpu.VMEM((CONV_HALO, D), BF16)],
        compiler_params=_params("parallel", "arbitrary"),
        name="ffn_up",
    )(h, h, g.reshape(1, D), w_gu, w_gu, conv_w, conv_b.reshape(1, F))


def _gelu_tanh(x):
    return 0.5 * x * (1.0 + jnp.tanh(0.7978845608028654 * (x + 0.044715 * x * x * x)))


def _compress_kernel(r_ref, pos_ref, w1_ref, b1_ref, w2_ref, o_ref):
    r = r_ref[0, 0].astype(F32)
    nch, half = r.shape
    pos = pos_ref[0]
    xa = (r + pos[0:1]).astype(BF16)
    xb = (r + pos[1:2]).astype(BF16)
    h1 = jnp.dot(xa, w1_ref[0, :half, :], preferred_element_type=F32)
    h2 = jnp.dot(xb, w1_ref[0, half:, :], preferred_element_type=F32)
    hid = h1 + pltpu.roll(h2, nch - 1, 0) + b1_ref[0]
    out = jnp.dot(_gelu_tanh(hid).astype(BF16), w2_ref[0], preferred_element_type=F32)
    row = lax.broadcasted_iota(jnp.int32, (nch, 1), 0)
    o_ref[0, 0, 0] = jnp.where(row < nch - 1, out, 0.0).astype(o_ref.dtype)


def compress(kv_slabs, pos, w1, b1, w2, *, batch, seq_len):
    G = NSA_KV_GROUPS
    nch = seq_len // CMP_STRIDE
    half = CMP_STRIDE * HEAD_DIM
    r = kv_slabs.reshape(kv_slabs.shape[0], batch, nch, half)
    hidden = w1.shape[-1]
    return pl.pallas_call(
        _compress_kernel,
        grid=(2, batch, G),
        in_specs=[
            pl.BlockSpec((1, 1, nch, half), lambda kv, b, g: (kv * G + g, b, 0, 0)),
            pl.BlockSpec((1, 2, half), lambda kv, b, g: (kv, 0, 0)),
            pl.BlockSpec((1, 2 * half, hidden), lambda kv, b, g: (kv, 0, 0)),
            pl.BlockSpec((1, 1, hidden), lambda kv, b, g: (kv, 0, 0)),
            pl.BlockSpec((1, hidden, HEAD_DIM), lambda kv, b, g: (kv, 0, 0)),
        ],
        out_specs=pl.BlockSpec((1, 1, 1, nch, HEAD_DIM), lambda kv, b, g: (kv, b, g, 0, 0)),
        out_shape=jax.ShapeDtypeStruct((2, batch, G, nch, HEAD_DIM), BF16),
        compiler_params=_params("parallel", "parallel", "parallel"),
        name="nsa_compress",
    )(r, pos.reshape(2, 2, half), w1, b1.reshape(2, 1, hidden), w2)


def _split3(x):
    hi = x.astype(BF16)
    r1 = x - hi.astype(F32)
    mid = r1.astype(BF16)
    lo = (r1 - mid.astype(F32)).astype(BF16)
    return hi, mid, lo


def _nsa_kernel(slope_ref, q_ref, gate_ref, kc_ref, vc_ref, ks_ref, vs_ref, kw_ref, vw_ref, o_ref,
                qa_scr, ka_scr, va_scr, p_scr, e_scr, mx_scr, acc_scr, used_ref, *, kt):
    J, dh = NSA_HPG, HEAD_DIM
    qb = q_ref.shape[0]
    seq_len = ks_ref.shape[2]
    ncp = kc_ref.shape[3]
    ns = seq_len // SEL_BLOCK
    g = pl.program_id(1)
    qi = pl.program_id(2)
    start = qi * qb

    @pl.when(qi == 0)
    def _():
        chunk = min(512, seq_len)

        def body(c, carry):
            r = pl.multiple_of(c * chunk, chunk)
            ka_scr[pl.ds(r, chunk), :dh] = ks_ref[0, 0, pl.ds(r, chunk), :]
            blk = (r + lax.broadcasted_iota(jnp.int32, (chunk, ns), 0)) // SEL_BLOCK
            ka_scr[pl.ds(r, chunk), dh:] = (blk == lax.broadcasted_iota(jnp.int32, (chunk, ns), 1)).astype(BF16)
            va_scr[pl.ds(r, chunk), :dh] = vs_ref[0, 0, pl.ds(r, chunk), :]
            va_scr[pl.ds(r, chunk), dh:] = jnp.ones((chunk, dh), BF16)
            return carry

        lax.fori_loop(0, seq_len // chunk, body, 0)

    q = q_ref[...]
    for j in range(J):
        qa_scr[j * qb:(j + 1) * qb, :dh] = q[:, j * dh:(j + 1) * dh]
    qr = qa_scr[:, :dh]
    slopes = [slope_ref[g * J + j] for j in range(J)]
    t_col = start + lax.broadcasted_iota(jnp.int32, (qb, 1), 0)

    lanes = mx_scr.shape[1]
    ones_sq = jnp.ones((lanes, lanes), BF16)
    kc = kc_ref[0, 0, 0]
    sc = lax.dot_general(qr, kc, _NT, preferred_element_type=F32)
    c_end = lax.broadcasted_iota(jnp.int32, (1, ncp), 1) * CMP_STRIDE + (CMP_LEN - 1)
    mask_c = t_col >= c_end
    rel_c = (c_end - start).astype(F32)
    for j in range(J):
        rows = slice(j * qb, (j + 1) * qb)
        sj = jnp.where(mask_c, sc[rows] + slopes[j] * rel_c, NEG_BIG)
        e = jnp.where(mask_c, jnp.exp(sj - jnp.max(sj, axis=-1, keepdims=True)), 0.0)
        e_scr[rows, :] = e
        p_scr[rows, :ncp] = e.astype(BF16)
        part = e[:, :lanes]
        for c in range(1, ncp // lanes):
            part = part + e[:, c * lanes:(c + 1) * lanes]
        mx_scr[rows] = part
    denom = jnp.zeros((J * qb, lanes), F32)
    for part in _split3(mx_scr[...]):
        denom = denom + jnp.dot(part, ones_sq, preferred_element_type=F32)
    inv_c = 1.0 / jnp.maximum(denom, 1e-30)
    o_cmp = jnp.dot(p_scr[:, :ncp], vc_ref[0, 0, 0], preferred_element_type=F32) * inv_c
    psum_chunks = []
    for c in range(ncp // lanes):
        cols = slice(c * lanes, (c + 1) * lanes)
        acc = e_scr[0:qb, cols] * inv_c[0:qb]
        for j in range(1, J):
            acc = acc + e_scr[j * qb:(j + 1) * qb, cols] * inv_c[j * qb:(j + 1) * qb]
        psum_chunks.append(acc)
    psum = jnp.concatenate(psum_chunks, axis=-1)

    wl = WINDOW + qb
    w0 = pl.multiple_of(jnp.maximum(start - WINDOW, 0), qb)
    kw = kw_ref[0, 0, pl.ds(w0, wl), :]
    vw = jnp.concatenate([vw_ref[0, 0, pl.ds(w0, wl), :], jnp.ones((wl, dh), BF16)], axis=1)
    sw = lax.dot_general(qr, kw, _NT, preferred_element_type=F32)
    pos_w = w0 + lax.broadcasted_iota(jnp.int32, (1, wl), 1)
    mask_w = (pos_w <= t_col) & (pos_w > t_col - WINDOW)
    rel_w = (pos_w - start).astype(F32)
    for j in range(J):
        rows = slice(j * qb, (j + 1) * qb)
        sj = jnp.where(mask_w, sw[rows] + slopes[j] * rel_w, NEG_BIG)
        p_scr[rows, :wl] = jnp.exp(sj - jnp.max(sj, axis=-1, keepdims=True)).astype(BF16)
    ow = jnp.dot(p_scr[:, :wl], vw, preferred_element_type=F32)
    o_win = ow[:, :dh] * (1.0 / jnp.maximum(ow[:, dh:], 1e-30))

    n_row = lax.broadcasted_iota(jnp.int32, (ns, ncp), 0) * SEL_BLOCK
    c_col = lax.broadcasted_iota(jnp.int32, (ns, ncp), 1) * CMP_STRIDE
    overlap_t = ((c_col < n_row + SEL_BLOCK) & (c_col + CMP_LEN > n_row)
                 & (c_col < (ncp - 1) * CMP_STRIDE)).astype(BF16)
    score_t = jnp.zeros((ns, qb), F32)
    for part in _split3(psum):
        score_t = score_t + lax.dot_general(overlap_t, part, _NT, preferred_element_type=F32)
    n_idx = lax.broadcasted_iota(jnp.int32, (ns, qb), 0)
    cur = (start + lax.broadcasted_iota(jnp.int32, (1, qb), 1)) // SEL_BLOCK
    forced = (n_idx == 0) | (n_idx == cur) | (n_idx == cur - 1)
    score_t = jnp.where(n_idx > cur, -1.0, jnp.where(forced, FORCED_SCORE, score_t))
    sel_t = jnp.zeros((ns, qb), F32)
    for _ in range(min(N_SELECT, ns)):
        best = jnp.max(score_t, axis=0, keepdims=True)
        first = jnp.min(jnp.where(score_t == best, n_idx, ns), axis=0, keepdims=True)
        hit = n_idx == first
        sel_t = jnp.where(hit, 1.0, sel_t)
        score_t = jnp.where(hit, -2.0, score_t)
    block_bias = jnp.where(sel_t.T > 0.5, 0.0, NEG_BIG).astype(BF16)
    for j in range(J):
        qa_scr[j * qb:(j + 1) * qb, dh:] = block_bias
    bpt = kt // SEL_BLOCK
    for i in range(ns // bpt):
        used_ref[i] = (jnp.max(sel_t[i * bpt:(i + 1) * bpt, :]) > 0.5).astype(jnp.int32)

    def scores(i, j, s, diagonal):
        pos = i * kt + lax.broadcasted_iota(jnp.int32, (1, kt), 1)
        sj = s[j * qb:(j + 1) * qb] + slopes[j] * (pos - start).astype(F32)
        if diagonal:
            sj = jnp.where(pos <= t_col, sj, NEG_BIG)
        return sj

    def max_tile(i, diagonal):
        k0 = pl.multiple_of(i * kt, kt)
        s = lax.dot_general(qa_scr[...], ka_scr[pl.ds(k0, kt), :], _NT, preferred_element_type=F32)
        for j in range(J):
            rows = slice(j * qb, (j + 1) * qb)
            sj = scores(i, j, s, diagonal)
            mx = mx_scr[rows]
            for c in range(kt // lanes):
                mx = jnp.maximum(mx, sj[:, c * lanes:(c + 1) * lanes])
            mx_scr[rows] = mx

    def pv_tile(i, diagonal):
        k0 = pl.multiple_of(i * kt, kt)
        s = lax.dot_general(qa_scr[...], ka_scr[pl.ds(k0, kt), :], _NT, preferred_element_type=F32)
        for j in range(J):
            rows = slice(j * qb, (j + 1) * qb)
            sj = scores(i, j, s, diagonal)
            m = mx_scr[rows]
            for c in range(kt // lanes):
                p_scr[rows, c * lanes:(c + 1) * lanes] = jnp.exp(sj[:, c * lanes:(c + 1) * lanes] - m).astype(BF16)
        acc_scr[...] += jnp.dot(p_scr[:, :kt], va_scr[pl.ds(k0, kt), :], preferred_element_type=F32)

    def run_pass(tile_fn):
        def full(i, carry):
            @pl.when(used_ref[i] > 0)
            def _():
                tile_fn(i, False)

            return carry

        n_full = start // kt
        lax.fori_loop(0, n_full, full, 0)
        tile_fn(n_full, True)

    mx_scr[...] = jnp.full(mx_scr.shape, NEG_BIG, F32)
    run_pass(max_tile)
    for j in range(J):
        rows = slice(j * qb, (j + 1) * qb)
        mx_scr[rows] = jnp.broadcast_to(jnp.max(mx_scr[rows], axis=-1, keepdims=True), (qb, lanes))
    acc_scr[...] = jnp.zeros(acc_scr.shape, F32)
    run_pass(pv_tile)

    gate = gate_ref[0]
    for j in range(J):
        rows = slice(j * qb, (j + 1) * qb)
        o_sel = acc_scr[rows, :dh] * (1.0 / jnp.maximum(acc_scr[rows, dh:], 1e-30))
        o = (gate[:, j:j + 1] * o_cmp[rows] + gate[:, J + j:J + j + 1] * o_sel
             + gate[:, 2 * J + j:2 * J + j + 1] * o_win[rows])
        o_ref[:, j * dh:(j + 1) * dh] = o.astype(o_ref.dtype)


def nsa_attention(q, gates, kv_cmp, kv_slabs, slopes, *, batch, seq_len, qb=128, kt=512):
    G, J, dh = NSA_KV_GROUPS, NSA_HPG, HEAD_DIM
    M = q.shape[0]
    nq = seq_len // qb
    ncp = kv_cmp.shape[3]
    ns = seq_len // SEL_BLOCK
    assert seq_len % kt == 0 and kt % qb == 0 and seq_len >= WINDOW + qb
    slabs = kv_slabs.reshape(kv_slabs.shape[0], batch, seq_len, dh)

    def slab(branch, is_v):
        base = (branch * 2 + is_v) * G
        return pl.BlockSpec((1, 1, seq_len, dh), lambda b, g, i, sl: (base + g, b, 0, 0))

    def cmp_spec(is_v):
        return pl.BlockSpec((1, 1, 1, ncp, dh), lambda b, g, i, sl: (is_v, b, g, 0, 0))

    grid_spec = pltpu.PrefetchScalarGridSpec(
        num_scalar_prefetch=1,
        grid=(batch, G, nq),
        in_specs=[
            pl.BlockSpec((qb, J * dh), lambda b, g, i, sl: (b * nq + i, g)),
            pl.BlockSpec((1, qb, NSA_BRANCHES * J), lambda b, g, i, sl: (g, b * nq + i, 0)),
            cmp_spec(0), cmp_spec(1),
            slab(1, 0), slab(1, 1), slab(2, 0), slab(2, 1),
        ],
        out_specs=pl.BlockSpec((qb, J * dh), lambda b, g, i, sl: (b * nq + i, g)),
        scratch_shapes=[
            pltpu.VMEM((J * qb, dh + ns), BF16),
            pltpu.VMEM((seq_len, dh + ns), BF16),
            pltpu.VMEM((seq_len, 2 * dh), BF16),
            pltpu.VMEM((J * qb, max(ncp, WINDOW + qb, kt)), BF16),
            pltpu.VMEM((J * qb, ncp), F32),
            pltpu.VMEM((J * qb, LANES), F32),
            pltpu.VMEM((J * qb, 2 * dh), F32),
            pltpu.SMEM((seq_len // kt,), jnp.int32),
        ],
    )
    return pl.pallas_call(
        functools.partial(_nsa_kernel, kt=kt),
        grid_spec=grid_spec,
        out_shape=jax.ShapeDtypeStruct((M, G * J * dh), BF16),
        compiler_params=_params("parallel", "parallel", "arbitrary"),
        name="nsa_attention",
    )(slopes, q, gates, kv_cmp, kv_cmp, slabs, slabs, slabs, slabs)


def _ffn_layer(h, g0, g1, w_gu, conv_w, conv_b, w_down, *, seq_len):
    act = ffn_up(h, g0, w_gu.astype(BF16), conv_w, conv_b, seq_len=seq_len)
    f = matmul(act, w_down.astype(BF16), tm=512, tn=512)
    return residual_norm(h, f, g1)


def _xattn(h, memn_src, mem_norm, g0, g1, wq, wkv, wo, *, batch, seq_len):
    mem_len = memn_src.shape[0] // batch
    kv = norm_matmul(memn_src, mem_norm, wkv.astype(BF16), tm=memn_src.shape[0], tn=wkv.shape[1] // 2,
                     out_dtype=BF16)
    kv = kv.reshape(batch, mem_len, wkv.shape[1])
    return xattn_layer(h, g0, wq.astype(BF16), kv, wo.astype(BF16), g1, seq_len=seq_len)


def _nsa_layer(h, g0, g1, w_in, w_out, cmp_pos, cmp_w1, cmp_b1, cmp_w2, *, batch, seq_len):
    G, J, dh = NSA_KV_GROUPS, NSA_HPG, HEAD_DIM
    qw = G * J * dh
    kvw = NSA_BRANCHES * 2 * G * dh
    q = norm_matmul(h, g0, w_in[:, :qw].astype(BF16), tm=1024, tn=512, out_dtype=BF16,
                    out_scale=dh ** -0.5)
    kv_slabs = norm_matmul(h, g0, w_in[:, qw:qw + kvw].astype(BF16), tm=1024, tn=512, out_dtype=BF16,
                           epilogue="slabs")
    gates = norm_matmul(h, g0, w_in[:, qw + kvw:].astype(BF16), tm=1024, tn=w_in.shape[1] - qw - kvw,
                        out_dtype=F32, epilogue="sigmoid")
    gates = gates.reshape(-1, NSA_BRANCHES, G, J).transpose(2, 0, 1, 3).reshape(G, -1, NSA_BRANCHES * J)
    kv_cmp = compress(kv_slabs, cmp_pos, cmp_w1.astype(BF16), cmp_b1, cmp_w2.astype(BF16),
                      batch=batch, seq_len=seq_len)
    n_heads = G * J
    slopes = 2.0 ** (-8.0 * jnp.arange(1, n_heads + 1, dtype=F32) / n_heads)
    o = nsa_attention(q, gates, kv_cmp, kv_slabs, slopes, batch=batch, seq_len=seq_len)
    a = matmul(o, w_out.astype(BF16), tm=1024, tn=512)
    return residual_norm(h, a, g1)


def kernel(x, mem, ln_mix, ln_xa, ln_ffn, mem_norm, pool_w, pool_scale, nsa_w_in, nsa_w_out, nsa_cmp_pos, nsa_cmp_w1, nsa_cmp_b1, nsa_cmp_w2, xa_wq, xa_wkv, xa_wo, ffn_w_gu, ffn_conv_w, ffn_conv_b, ffn_w_down):
    B, S, D = x.shape
    depth = ln_mix.shape[0]
    h = x.reshape(B * S, D)
    mem2 = mem.reshape(B * mem.shape[1], D)
    for i in range(depth):
        j = i // 2
        if i % 2 == 0:
            h = pool_layer(h, ln_mix[i, 0], pool_w[j].astype(BF16), pool_scale[j], ln_mix[i, 1], seq_len=S)
        else:
            h = _nsa_layer(h, ln_mix[i, 0], ln_mix[i, 1], nsa_w_in[j], nsa_w_out[j], nsa_cmp_pos[j],
                           nsa_cmp_w1[j], nsa_cmp_b1[j], nsa_cmp_w2[j], batch=B, seq_len=S)
        h = _xattn(h, mem2, mem_norm, ln_xa[i, 0], ln_xa[i, 1], xa_wq[i], xa_wkv[i], xa_wo[i],
                   batch=B, seq_len=S)
        h = _ffn_layer(h, ln_ffn[i, 0], ln_ffn[i, 1], ffn_w_gu[i], ffn_conv_w[i], ffn_conv_b[i],
                       ffn_w_down[i], seq_len=S)
    return h.reshape(B, S, D)
```

```python
import functools

import jax
import jax.numpy as jnp
from jax import lax
from jax.experimental import pallas as pl
from jax.experimental.pallas import tpu as pltpu

F32 = jnp.float32
BF16 = jnp.bfloat16

RMS_EPS = 1e-6
NEG_BIG = -1e30
HEAD_DIM = 128
POOL_WINDOWS = (2, 4, 8, 16)
POOL_HALO = 16
NSA_KV_GROUPS = 4
NSA_HPG = 8
NSA_BRANCHES = 3
CMP_LEN = 32
CMP_STRIDE = 16
SEL_BLOCK = 64
N_SELECT = 16
WINDOW = 512
FORCED_SCORE = 1e6
XA_HEADS = 4
CONV_HALO = 8
VMEM_LIMIT = 56 * 1024 * 1024
NORM_CHUNK = 128
LANES = 128

_NT = (((1,), (1,)), ((), ()))


def _params(*sem):
    return pltpu.CompilerParams(dimension_semantics=sem, vmem_limit_bytes=VMEM_LIMIT)


def _rms(x, g):
    ms = jnp.mean(x * x, axis=-1, keepdims=True)
    return x * lax.rsqrt(ms + RMS_EPS) * g


def _norm_rows(h_ref, g_ref, a_scr):
    tm = h_ref.shape[0]
    chunk = min(NORM_CHUNK, tm)

    def body(c, carry):
        r = pl.multiple_of(c * chunk, chunk)
        a_scr[pl.ds(r, chunk), :] = _rms(h_ref[pl.ds(r, chunk), :], g_ref[...]).astype(BF16)
        return carry

    lax.fori_loop(0, tm // chunk, body, 0)


def _once(shape, index_map):
    return pl.BlockSpec(shape, index_map, pipeline_mode=pl.Buffered(1))


def _norm_mm_kernel(h_ref, g_ref, w_ref, o_ref, a_scr, *, epilogue, out_scale):
    @pl.when(pl.program_id(1) == 0)
    def _():
        _norm_rows(h_ref, g_ref, a_scr)

    acc = jnp.dot(a_scr[...], w_ref[...], preferred_element_type=F32)
    if epilogue == "slabs":
        for s in range(o_ref.shape[0]):
            o_ref[s] = acc[:, s * HEAD_DIM:(s + 1) * HEAD_DIM].astype(o_ref.dtype)
    elif epilogue == "sigmoid":
        o_ref[...] = jax.nn.sigmoid(acc).astype(o_ref.dtype)
    else:
        o_ref[...] = (acc * out_scale).astype(o_ref.dtype)


def norm_matmul(h, g, w, layer, *, tm, tn, out_dtype, epilogue="plain", out_scale=1.0, col0=0, n_cols=None):
    M, D = h.shape
    N = w.shape[2] - col0 if n_cols is None else n_cols
    assert M % tm == 0 and N % tn == 0 and col0 % tn == 0
    jb = col0 // tn
    if epilogue == "slabs":
        ns = tn // HEAD_DIM
        out_shape = jax.ShapeDtypeStruct((N // HEAD_DIM, M, HEAD_DIM), out_dtype)
        out_spec = pl.BlockSpec((ns, tm, HEAD_DIM), lambda i, j: (j, i, 0))
    else:
        out_shape = jax.ShapeDtypeStruct((M, N), out_dtype)
        out_spec = pl.BlockSpec((tm, tn), lambda i, j: (i, j))
    return pl.pallas_call(
        functools.partial(_norm_mm_kernel, epilogue=epilogue, out_scale=out_scale),
        grid=(M // tm, N // tn),
        in_specs=[
            _once((tm, D), lambda i, j: (i, 0)),
            pl.BlockSpec((1, D), lambda i, j: (0, 0)),
            pl.BlockSpec((None, D, tn), lambda i, j: (layer, 0, jb + j)),
        ],
        out_specs=out_spec,
        out_shape=out_shape,
        scratch_shapes=[pltpu.VMEM((tm, D), BF16)],
        compiler_params=_params("parallel", "arbitrary"),
        name="norm_matmul_" + epilogue,
    )(h, g.reshape(1, D), w)


def _mm_kernel(a_ref, w_ref, o_ref):
    o_ref[...] = jnp.dot(a_ref[...], w_ref[...], preferred_element_type=F32).astype(o_ref.dtype)


def matmul(a, w, layer, *, tm, tn, out_dtype=F32):
    M, K = a.shape
    N = w.shape[2]
    assert M % tm == 0 and N % tn == 0
    return pl.pallas_call(
        _mm_kernel,
        grid=(M // tm, N // tn),
        in_specs=[
            _once((tm, K), lambda i, j: (i, 0)),
            pl.BlockSpec((None, K, tn), lambda i, j: (layer, 0, j)),
        ],
        out_specs=pl.BlockSpec((tm, tn), lambda i, j: (i, j)),
        out_shape=jax.ShapeDtypeStruct((M, N), out_dtype),
        compiler_params=_params("parallel", "arbitrary"),
        name="matmul",
    )(a, w)


def _resid_kernel(h_ref, f_ref, g_ref, o_ref):
    o_ref[...] = h_ref[...] + _rms(f_ref[...], g_ref[...])


def residual_norm(h, f, g, *, tm=256):
    M, D = h.shape
    row = pl.BlockSpec((tm, D), lambda i: (i, 0))
    return pl.pallas_call(
        _resid_kernel,
        grid=(M // tm,),
        in_specs=[row, row, pl.BlockSpec((1, D), lambda i: (0, 0))],
        out_specs=row,
        out_shape=jax.ShapeDtypeStruct((M, D), F32),
        compiler_params=_params("parallel"),
        name="residual_norm",
    )(h, f, g.reshape(1, D))


def _pool_kernel(x_ref, halo_ref, g0_ref, w_ref, sc_ref, g1_ref, o_ref, *, seq_len):
    tm, D = x_ref.shape
    gc = D // len(POOL_WINDOWS)
    t0 = (pl.program_id(0) * tm) % seq_len
    x = x_ref[...]
    a = _rms(x, g0_ref[...])
    ha = _rms(halo_ref[...], g0_ref[...])
    ha = jnp.where(t0 == 0, 0.0, ha)
    full = jnp.concatenate([ha, a], axis=0)
    t1 = (t0 + 1 + lax.broadcasted_iota(jnp.int32, (tm, 1), 0)).astype(F32)
    ys = []
    for gi, win in enumerate(POOL_WINDOWS):
        s = full[:, gi * gc:(gi + 1) * gc]
        k = 1
        while k < win:
            s = s + pltpu.roll(s, k, 0)
            k *= 2
        inv_cnt = 1.0 / jnp.minimum(t1, float(win))
        pooled = s[POOL_HALO:] * inv_cnt
        d = (pooled - a[:, gi * gc:(gi + 1) * gc]).astype(BF16)
        ys.append(jnp.dot(d, w_ref[gi], preferred_element_type=F32))
    y = jnp.concatenate(ys, axis=-1) * sc_ref[...]
    o_ref[...] = x + _rms(y, g1_ref[...])


def pool_layer(x, g0, w, scale, g1, *, seq_len, tm=256):
    M, D = x.shape
    G, gc, _ = w.shape
    vec = pl.BlockSpec((1, D), lambda i: (0, 0))
    hb = tm // POOL_HALO
    return pl.pallas_call(
        functools.partial(_pool_kernel, seq_len=seq_len),
        grid=(M // tm,),
        in_specs=[
            pl.BlockSpec((tm, D), lambda i: (i, 0)),
            pl.BlockSpec((POOL_HALO, D), lambda i: (jnp.maximum(i * hb - 1, 0), 0)),
            vec,
            _once((G, gc, gc), lambda i: (0, 0, 0)),
            vec,
            vec,
        ],
        out_specs=pl.BlockSpec((tm, D), lambda i: (i, 0)),
        out_shape=jax.ShapeDtypeStruct((M, D), F32),
        compiler_params=_params("parallel"),
        name="pool_layer",
    )(x, x, g0.reshape(1, D), w, scale.reshape(1, D), g1.reshape(1, D))


def _xattn_kernel(h_ref, g0_ref, wq_ref, k_ref, v_ref, wo_ref, g1_ref, o_ref):
    h = h_ref[...]
    n = _rms(h, g0_ref[...]).astype(BF16)
    q = jnp.dot(n, wq_ref[...], preferred_element_type=F32).astype(BF16)
    k = k_ref[0]
    v = v_ref[0]
    scale = HEAD_DIM ** -0.5
    outs = []
    for hh in range(XA_HEADS):
        sl = slice(hh * HEAD_DIM, (hh + 1) * HEAD_DIM)
        s = lax.dot_general(q[:, sl], k[:, sl], _NT, preferred_element_type=F32) * scale
        m = jnp.max(s, axis=-1, keepdims=True)
        e = jnp.exp(s - m)
        p = e * (1.0 / jnp.sum(e, axis=-1, keepdims=True))
        outs.append(jnp.dot(p.astype(BF16), v[:, sl], preferred_element_type=F32))
    o = jnp.concatenate(outs, axis=-1).astype(BF16)
    c = jnp.dot(o, wo_ref[...], preferred_element_type=F32)
    o_ref[...] = h + _rms(c, g1_ref[...])


def xattn_layer(h, g0, wq, kv, wo, g1, *, seq_len, tm=256):
    M, D = h.shape
    xw = wq.shape[1]
    mem_len = kv.shape[1]
    per_seq = seq_len // tm
    vec = pl.BlockSpec((1, D), lambda i: (0, 0))
    return pl.pallas_call(
        _xattn_kernel,
        grid=(M // tm,),
        in_specs=[
            pl.BlockSpec((tm, D), lambda i: (i, 0)),
            vec,
            _once((D, xw), lambda i: (0, 0)),
            pl.BlockSpec((1, mem_len, xw), lambda i: (i // per_seq, 0, 0)),
            pl.BlockSpec((1, mem_len, xw), lambda i: (i // per_seq, 0, 1)),
            _once((xw, D), lambda i: (0, 0)),
            vec,
        ],
        out_specs=pl.BlockSpec((tm, D), lambda i: (i, 0)),
        out_shape=jax.ShapeDtypeStruct((M, D), F32),
        compiler_params=_params("parallel"),
        name="xattn_layer",
    )(h, g0.reshape(1, D), wq, kv, kv, wo, g1.reshape(1, D))


def _ffn_up_kernel(h_ref, halo_ref, g_ref, wg_ref, wu_ref, cw_ref, cb_ref, o_ref, a_scr, halo_scr, *, seq_len):
    tm = h_ref.shape[0]

    @pl.when(pl.program_id(1) == 0)
    def _():
        _norm_rows(h_ref, g_ref, a_scr)
        t0 = (pl.program_id(0) * tm) % seq_len
        hn = _rms(halo_ref[...], g_ref[...])
        halo_scr[...] = jnp.where(t0 == 0, 0.0, hn).astype(BF16)

    a = a_scr[...]
    gate = jnp.dot(a, wg_ref[...], preferred_element_type=F32)
    up = jnp.dot(a, wu_ref[...], preferred_element_type=F32)
    hg = jnp.dot(halo_scr[...], wg_ref[...], preferred_element_type=F32)
    row = lax.broadcasted_iota(jnp.int32, (tm, 1), 0)
    prev1 = hg[CONV_HALO - 1:CONV_HALO]
    prev2 = hg[CONV_HALO - 2:CONV_HALO - 1]
    g1 = jnp.where(row == 0, prev1, pltpu.roll(gate, 1, 0))
    g2 = jnp.where(row == 0, prev2, jnp.where(row == 1, prev1, pltpu.roll(gate, 2, 0)))
    cw = cw_ref[...]
    gc = cw[0:1] * g2 + cw[1:2] * g1 + cw[2:3] * gate + cb_ref[...]
    o_ref[...] = (gc * jax.nn.sigmoid(gc) * up).astype(o_ref.dtype)


def ffn_up(h, g, w_gu, layer, conv_w, conv_b, *, seq_len, tm=1024, tn=256):
    M, D = h.shape
    F = w_gu.shape[2] // 2
    assert F % tn == 0 and M % tm == 0 and seq_len % tm == 0
    nj = F // tn
    hb = tm // CONV_HALO
    return pl.pallas_call(
        functools.partial(_ffn_up_kernel, seq_len=seq_len),
        grid=(M // tm, nj),
        in_specs=[
            _once((tm, D), lambda i, j: (i, 0)),
            _once((CONV_HALO, D), lambda i, j: (jnp.maximum(i * hb - 1, 0), 0)),
            pl.BlockSpec((1, D), lambda i, j: (0, 0)),
            pl.BlockSpec((None, D, tn), lambda i, j: (layer, 0, j)),
            pl.BlockSpec((None, D, tn), lambda i, j: (layer, 0, j + nj)),
            pl.BlockSpec((conv_w.shape[0], tn), lambda i, j: (0, j)),
            pl.BlockSpec((1, tn), lambda i, j: (0, j)),
        ],
        out_specs=pl.BlockSpec((tm, tn), lambda i, j: (i, j)),
        out_shape=jax.ShapeDtypeStruct((M, F), BF16),
        scratch_shapes=[pltpu.VMEM((tm, D), BF16), pltpu.VMEM((CONV_HALO, D), BF16)],
        compiler_params=_params("parallel", "arbitrary"),
        name="ffn_up",
    )(h, h, g.reshape(1, D), w_gu, w_gu, conv_w, conv_b.reshape(1, F))


def _gelu_tanh(x):
    return 0.5 * x * (1.0 + jnp.tanh(0.7978845608028654 * (x + 0.044715 * x * x * x)))


def _compress_kernel(r_ref, pos_ref, w1_ref, b1_ref, w2_ref, o_ref):
    r = r_ref[0, 0].astype(F32)
    nch, half = r.shape
    pos = pos_ref[0]
    xa = (r + pos[0:1]).astype(BF16)
    xb = (r + pos[1:2]).astype(BF16)
    h1 = jnp.dot(xa, w1_ref[0, :half, :], preferred_element_type=F32)
    h2 = jnp.dot(xb, w1_ref[0, half:, :], preferred_element_type=F32)
    hid = h1 + pltpu.roll(h2, nch - 1, 0) + b1_ref[0]
    out = jnp.dot(_gelu_tanh(hid).astype(BF16), w2_ref[0], preferred_element_type=F32)
    row = lax.broadcasted_iota(jnp.int32, (nch, 1), 0)
    o_ref[0, 0, 0] = jnp.where(row < nch - 1, out, 0.0).astype(o_ref.dtype)


def compress(kv_slabs, pos, w1, b1, w2, *, batch, seq_len):
    G = NSA_KV_GROUPS
    nch = seq_len // CMP_STRIDE
    half = CMP_STRIDE * HEAD_DIM
    r = kv_slabs.reshape(kv_slabs.shape[0], batch, nch, half)
    hidden = w1.shape[-1]
    return pl.pallas_call(
        _compress_kernel,
        grid=(2, batch, G),
        in_specs=[
            pl.BlockSpec((1, 1, nch, half), lambda kv, b, g: (kv * G + g, b, 0, 0)),
            pl.BlockSpec((1, 2, half), lambda kv, b, g: (kv, 0, 0)),
            pl.BlockSpec((1, 2 * half, hidden), lambda kv, b, g: (kv, 0, 0)),
            pl.BlockSpec((1, 1, hidden), lambda kv, b, g: (kv, 0, 0)),
            pl.BlockSpec((1, hidden, HEAD_DIM), lambda kv, b, g: (kv, 0, 0)),
        ],
        out_specs=pl.BlockSpec((1, 1, 1, nch, HEAD_DIM), lambda kv, b, g: (kv, b, g, 0, 0)),
        out_shape=jax.ShapeDtypeStruct((2, batch, G, nch, HEAD_DIM), BF16),
        compiler_params=_params("parallel", "parallel", "parallel"),
        name="nsa_compress",
    )(r, pos.reshape(2, 2, half), w1, b1.reshape(2, 1, hidden), w2)


def _nsa_kernel(slope_ref, q_ref, gate_ref, kc_ref, vc_ref, ks_ref, vs_ref, kw_ref, vw_ref, o_ref,
                qa_scr, ka_scr, va_scr, p_scr, oc_scr, mx_scr, acc_scr, used_ref, *, kt):
    J, dh = NSA_HPG, HEAD_DIM
    qb = q_ref.shape[0]
    seq_len = ks_ref.shape[2]
    ncp = kc_ref.shape[3]
    ns = seq_len // SEL_BLOCK
    g = pl.program_id(1)
    qi = pl.program_id(2)
    start = qi * qb

    @pl.when(qi == 0)
    def _():
        chunk = min(512, seq_len)

        def body(c, carry):
            r = pl.multiple_of(c * chunk, chunk)
            ka_scr[pl.ds(r, chunk), :dh] = ks_ref[0, 0, pl.ds(r, chunk), :]
            blk = (r + lax.broadcasted_iota(jnp.int32, (chunk, ns), 0)) // SEL_BLOCK
            ka_scr[pl.ds(r, chunk), dh:] = (blk == lax.broadcasted_iota(jnp.int32, (chunk, ns), 1)).astype(BF16)
            va_scr[pl.ds(r, chunk), :dh] = vs_ref[0, 0, pl.ds(r, chunk), :]
            va_scr[pl.ds(r, chunk), dh:] = jnp.ones((chunk, dh), BF16)
            return carry

        lax.fori_loop(0, seq_len // chunk, body, 0)

    q = q_ref[...]
    for j in range(J):
        qa_scr[j * qb:(j + 1) * qb, :dh] = q[:, j * dh:(j + 1) * dh]
    qr = qa_scr[:, :dh]
    slopes = [slope_ref[g * J + j] for j in range(J)]
    t_col = start + lax.broadcasted_iota(jnp.int32, (qb, 1), 0)

    lanes = mx_scr.shape[1]
    kc = kc_ref[0, 0, 0]
    sc = lax.dot_general(qr, kc, _NT, preferred_element_type=F32)
    c_end = lax.broadcasted_iota(jnp.int32, (1, ncp), 1) * CMP_STRIDE + (CMP_LEN - 1)
    mask_c = t_col >= c_end
    rel_c = (c_end - start).astype(F32)
    for j in range(J):
        rows = slice(j * qb, (j + 1) * qb)
        sj = jnp.where(mask_c, sc[rows] + slopes[j] * rel_c, NEG_BIG)
        p_scr[rows, :ncp] = jnp.exp(sj - jnp.max(sj, axis=-1, keepdims=True)).astype(BF16)
    c_row = lax.broadcasted_iota(jnp.int32, (ncp, ns), 0) * CMP_STRIDE
    n_col = lax.broadcasted_iota(jnp.int32, (ncp, ns), 1) * SEL_BLOCK
    overlap = ((c_row < n_col + SEL_BLOCK) & (c_row + CMP_LEN > n_col)
               & (c_row < (ncp - 1) * CMP_STRIDE)).astype(BF16)
    rhs_c = jnp.concatenate([vc_ref[0, 0, 0], jnp.ones((ncp, lanes), BF16), overlap], axis=1)
    oc = jnp.dot(p_scr[:, :ncp], rhs_c, preferred_element_type=F32)
    has_key = t_col >= CMP_LEN - 1
    score = jnp.zeros((qb, ns), F32)
    for j in range(J):
        rows = slice(j * qb, (j + 1) * qb)
        inv = jnp.where(has_key, 1.0 / jnp.maximum(oc[rows, dh:dh + lanes], 1e-30), 0.0)
        oc_scr[rows] = oc[rows, :dh] * inv
        score = score + oc[rows, dh + lanes:] * inv[:, :ns]

    wl = WINDOW + qb
    w0 = pl.multiple_of(jnp.maximum(start - WINDOW, 0), qb)
    kw = kw_ref[0, 0, pl.ds(w0, wl), :]
    vw = jnp.concatenate([vw_ref[0, 0, pl.ds(w0, wl), :], jnp.ones((wl, dh), BF16)], axis=1)
    sw = lax.dot_general(qr, kw, _NT, preferred_element_type=F32)
    pos_w = w0 + lax.broadcasted_iota(jnp.int32, (1, wl), 1)
    mask_w = (pos_w <= t_col) & (pos_w > t_col - WINDOW)
    rel_w = (pos_w - start).astype(F32)
    for j in range(J):
        rows = slice(j * qb, (j + 1) * qb)
        sj = jnp.where(mask_w, sw[rows] + slopes[j] * rel_w, NEG_BIG)
        p_scr[rows, :wl] = jnp.exp(sj - jnp.max(sj, axis=-1, keepdims=True)).astype(BF16)
    ow = jnp.dot(p_scr[:, :wl], vw, preferred_element_type=F32)
    o_win = ow[:, :dh] * (1.0 / jnp.maximum(ow[:, dh:], 1e-30))

    score_t = score.T
    n_idx = lax.broadcasted_iota(jnp.int32, (ns, qb), 0)
    cur = (start + lax.broadcasted_iota(jnp.int32, (1, qb), 1)) // SEL_BLOCK
    forced = (n_idx == 0) | (n_idx == cur) | (n_idx == cur - 1)
    score_t = jnp.where(n_idx > cur, -1.0, jnp.where(forced, FORCED_SCORE, score_t))
    sel_t = jnp.zeros((ns, qb), F32)
    for _ in range(min(N_SELECT, ns)):
        best = jnp.max(score_t, axis=0, keepdims=True)
        first = jnp.min(jnp.where(score_t == best, n_idx, ns), axis=0, keepdims=True)
        hit = n_idx == first
        sel_t = jnp.where(hit, 1.0, sel_t)
        score_t = jnp.where(hit, -2.0, score_t)
    block_bias = jnp.where(sel_t.T > 0.5, 0.0, NEG_BIG).astype(BF16)
    for j in range(J):
        qa_scr[j * qb:(j + 1) * qb, dh:] = block_bias
    bpt = kt // SEL_BLOCK
    for i in range(ns // bpt):
        used_ref[i] = (jnp.max(sel_t[i * bpt:(i + 1) * bpt, :]) > 0.5).astype(jnp.int32)

    def scores(i, j, s, diagonal):
        pos = i * kt + lax.broadcasted_iota(jnp.int32, (1, kt), 1)
        sj = s[j * qb:(j + 1) * qb] + slopes[j] * (pos - start).astype(F32)
        if diagonal:
            sj = jnp.where(pos <= t_col, sj, NEG_BIG)
        return sj

    def max_tile(i, diagonal):
        k0 = pl.multiple_of(i * kt, kt)
        s = lax.dot_general(qa_scr[...], ka_scr[pl.ds(k0, kt), :], _NT, preferred_element_type=F32)
        for j in range(J):
            rows = slice(j * qb, (j + 1) * qb)
            sj = scores(i, j, s, diagonal)
            mx = mx_scr[rows]
            for c in range(kt // lanes):
                mx = jnp.maximum(mx, sj[:, c * lanes:(c + 1) * lanes])
            mx_scr[rows] = mx

    def pv_tile(i, diagonal):
        k0 = pl.multiple_of(i * kt, kt)
        s = lax.dot_general(qa_scr[...], ka_scr[pl.ds(k0, kt), :], _NT, preferred_element_type=F32)
        for j in range(J):
            rows = slice(j * qb, (j + 1) * qb)
            sj = scores(i, j, s, diagonal)
            m = mx_scr[rows]
            for c in range(kt // lanes):
                p_scr[rows, c * lanes:(c + 1) * lanes] = jnp.exp(sj[:, c * lanes:(c + 1) * lanes] - m).astype(BF16)
        acc_scr[...] += jnp.dot(p_scr[:, :kt], va_scr[pl.ds(k0, kt), :], preferred_element_type=F32)

    def run_pass(tile_fn):
        def full(i, carry):
            @pl.when(used_ref[i] > 0)
            def _():
                tile_fn(i, False)

            return carry

        n_full = start // kt
        lax.fori_loop(0, n_full, full, 0)
        tile_fn(n_full, True)

    mx_scr[...] = jnp.full(mx_scr.shape, NEG_BIG, F32)
    run_pass(max_tile)
    for j in range(J):
        rows = slice(j * qb, (j + 1) * qb)
        mx_scr[rows] = jnp.broadcast_to(jnp.max(mx_scr[rows], axis=-1, keepdims=True), (qb, lanes))
    acc_scr[...] = jnp.zeros(acc_scr.shape, F32)
    run_pass(pv_tile)

    gate = gate_ref[0]
    for j in range(J):
        rows = slice(j * qb, (j + 1) * qb)
        o_sel = acc_scr[rows, :dh] * (1.0 / jnp.maximum(acc_scr[rows, dh:], 1e-30))
        o = (gate[:, j:j + 1] * oc_scr[rows] + gate[:, J + j:J + j + 1] * o_sel
             + gate[:, 2 * J + j:2 * J + j + 1] * o_win[rows])
        o_ref[:, j * dh:(j + 1) * dh] = o.astype(o_ref.dtype)


def nsa_attention(q, gates, kv_cmp, kv_slabs, slopes, *, batch, seq_len, qb=128, kt=512):
    G, J, dh = NSA_KV_GROUPS, NSA_HPG, HEAD_DIM
    M = q.shape[0]
    nq = seq_len // qb
    ncp = kv_cmp.shape[3]
    ns = seq_len // SEL_BLOCK
    assert seq_len % kt == 0 and kt % qb == 0 and seq_len >= WINDOW + qb
    slabs = kv_slabs.reshape(kv_slabs.shape[0], batch, seq_len, dh)

    def slab(branch, is_v):
        base = (branch * 2 + is_v) * G
        return pl.BlockSpec((1, 1, seq_len, dh), lambda b, g, i, sl: (base + g, b, 0, 0))

    def cmp_spec(is_v):
        return pl.BlockSpec((1, 1, 1, ncp, dh), lambda b, g, i, sl: (is_v, b, g, 0, 0))

    grid_spec = pltpu.PrefetchScalarGridSpec(
        num_scalar_prefetch=1,
        grid=(batch, G, nq),
        in_specs=[
            pl.BlockSpec((qb, J * dh), lambda b, g, i, sl: (b * nq + i, g)),
            pl.BlockSpec((1, qb, NSA_BRANCHES * J), lambda b, g, i, sl: (g, b * nq + i, 0)),
            cmp_spec(0), cmp_spec(1),
            slab(1, 0), slab(1, 1), slab(2, 0), slab(2, 1),
        ],
        out_specs=pl.BlockSpec((qb, J * dh), lambda b, g, i, sl: (b * nq + i, g)),
        scratch_shapes=[
            pltpu.VMEM((J * qb, dh + ns), BF16),
            pltpu.VMEM((seq_len, dh + ns), BF16),
            pltpu.VMEM((seq_len, 2 * dh), BF16),
            pltpu.VMEM((J * qb, max(ncp, WINDOW + qb, kt)), BF16),
            pltpu.VMEM((J * qb, dh), F32),
            pltpu.VMEM((J * qb, LANES), F32),
            pltpu.VMEM((J * qb, 2 * dh), F32),
            pltpu.SMEM((seq_len // kt,), jnp.int32),
        ],
    )
    return pl.pallas_call(
        functools.partial(_nsa_kernel, kt=kt),
        grid_spec=grid_spec,
        out_shape=jax.ShapeDtypeStruct((M, G * J * dh), BF16),
        compiler_params=_params("parallel", "parallel", "arbitrary"),
        name="nsa_attention",
    )(slopes, q, gates, kv_cmp, kv_cmp, slabs, slabs, slabs, slabs)


def _ffn_layer(h, g0, g1, w_gu, w_down, layer, conv_w, conv_b, *, seq_len):
    act = ffn_up(h, g0, w_gu, layer, conv_w, conv_b, seq_len=seq_len)
    f = matmul(act, w_down, layer, tm=512, tn=512)
    return residual_norm(h, f, g1)


def _xattn(h, mem_rows, mem_norm, g0, g1, wq, wkv, wo, layer, *, batch, seq_len):
    mem_len = mem_rows.shape[0] // batch
    kv = norm_matmul(mem_rows, mem_norm, wkv, layer, tm=mem_rows.shape[0], tn=wkv.shape[2] // 2, out_dtype=BF16)
    kv = kv.reshape(batch, mem_len, wkv.shape[2])
    return xattn_layer(h, g0, wq[layer], kv, wo[layer], g1, seq_len=seq_len)


def _nsa_layer(h, g0, g1, w_in, w_out, layer, cmp_pos, cmp_w1, cmp_b1, cmp_w2, *, batch, seq_len):
    G, J, dh = NSA_KV_GROUPS, NSA_HPG, HEAD_DIM
    qw = G * J * dh
    kvw = NSA_BRANCHES * 2 * G * dh
    q = norm_matmul(h, g0, w_in, layer, tm=1024, tn=512, out_dtype=BF16, n_cols=qw,
                    out_scale=dh ** -0.5)
    kv_slabs = norm_matmul(h, g0, w_in, layer, tm=1024, tn=512, out_dtype=BF16, col0=qw, n_cols=kvw,
                           epilogue="slabs")
    w_gate = w_in[layer:layer + 1, :, qw + kvw:]
    gates = norm_matmul(h, g0, w_gate, 0, tm=1024, tn=w_gate.shape[2], out_dtype=F32, epilogue="sigmoid")
    gates = gates.reshape(-1, NSA_BRANCHES, G, J).transpose(2, 0, 1, 3).reshape(G, -1, NSA_BRANCHES * J)
    kv_cmp = compress(kv_slabs, cmp_pos, cmp_w1.astype(BF16), cmp_b1, cmp_w2.astype(BF16),
                      batch=batch, seq_len=seq_len)
    n_heads = G * J
    slopes = 2.0 ** (-8.0 * jnp.arange(1, n_heads + 1, dtype=F32) / n_heads)
    o = nsa_attention(q, gates, kv_cmp, kv_slabs, slopes, batch=batch, seq_len=seq_len)
    a = matmul(o, w_out, layer, tm=1024, tn=512)
    return residual_norm(h, a, g1)


def kernel(x, mem, ln_mix, ln_xa, ln_ffn, mem_norm, pool_w, pool_scale, nsa_w_in, nsa_w_out, nsa_cmp_pos, nsa_cmp_w1, nsa_cmp_b1, nsa_cmp_w2, xa_wq, xa_wkv, xa_wo, ffn_w_gu, ffn_conv_w, ffn_conv_b, ffn_w_down):
    B, S, D = x.shape
    depth = ln_mix.shape[0]
    h = x.reshape(B * S, D)
    mem_rows = mem.reshape(B * mem.shape[1], D)
    nsa_w_in, nsa_w_out = nsa_w_in.astype(BF16), nsa_w_out.astype(BF16)
    xa_wq, xa_wkv, xa_wo = xa_wq.astype(BF16), xa_wkv.astype(BF16), xa_wo.astype(BF16)
    ffn_w_gu, ffn_w_down = ffn_w_gu.astype(BF16), ffn_w_down.astype(BF16)
    for i in range(depth):
        j = i // 2
        if i % 2 == 0:
            h = pool_layer(h, ln_mix[i, 0], pool_w[j].astype(BF16), pool_scale[j], ln_mix[i, 1], seq_len=S)
        else:
            h = _nsa_layer(h, ln_mix[i, 0], ln_mix[i, 1], nsa_w_in, nsa_w_out, j, nsa_cmp_pos[j],
                           nsa_cmp_w1[j], nsa_cmp_b1[j], nsa_cmp_w2[j], batch=B, seq_len=S)
        h = _xattn(h, mem_rows, mem_norm, ln_xa[i, 0], ln_xa[i, 1], xa_wq, xa_wkv, xa_wo, i, batch=B, seq_len=S)
        h = _ffn_layer(h, ln_ffn[i, 0], ln_ffn[i, 1], ffn_w_gu, ffn_w_down, i, ffn_conv_w[i], ffn_conv_b[i],
                       seq_len=S)
    return h.reshape(B, S, D)
```

```python
import functools

import jax
import jax.numpy as jnp
from jax import lax
from jax.experimental import pallas as pl
from jax.experimental.pallas import tpu as pltpu

F32 = jnp.float32
BF16 = jnp.bfloat16

RMS_EPS = 1e-6
NEG_BIG = -1e30
HEAD_DIM = 128
POOL_WINDOWS = (2, 4, 8, 16)
POOL_HALO = 16
NSA_KV_GROUPS = 4
NSA_HPG = 8
NSA_BRANCHES = 3
CMP_LEN = 32
CMP_STRIDE = 16
SEL_BLOCK = 64
N_SELECT = 16
WINDOW = 512
FORCED_SCORE = 1e6
XA_HEADS = 4
CONV_HALO = 8
VMEM_LIMIT = 56 * 1024 * 1024
NORM_CHUNK = 128
LANES = 128

_NT = (((1,), (1,)), ((), ()))


def _params(*sem):
    return pltpu.CompilerParams(dimension_semantics=sem, vmem_limit_bytes=VMEM_LIMIT)


def _rms(x, g):
    ms = jnp.mean(x * x, axis=-1, keepdims=True)
    return x * lax.rsqrt(ms + RMS_EPS) * g


def _norm_rows(h_ref, g_ref, a_scr):
    tm = h_ref.shape[0]
    chunk = min(NORM_CHUNK, tm)

    def body(c, carry):
        r = pl.multiple_of(c * chunk, chunk)
        a_scr[pl.ds(r, chunk), :] = _rms(h_ref[pl.ds(r, chunk), :], g_ref[...]).astype(BF16)
        return carry

    lax.fori_loop(0, tm // chunk, body, 0)


def _once(shape, index_map):
    return pl.BlockSpec(shape, index_map, pipeline_mode=pl.Buffered(1))


def _norm_mm_kernel(h_ref, g_ref, w_ref, o_ref, a_scr, *, epilogue, out_scale):
    @pl.when(pl.program_id(1) == 0)
    def _():
        _norm_rows(h_ref, g_ref, a_scr)

    acc = jnp.dot(a_scr[...], w_ref[...], preferred_element_type=F32)
    if epilogue == "slabs":
        for s in range(o_ref.shape[0]):
            o_ref[s] = acc[:, s * HEAD_DIM:(s + 1) * HEAD_DIM].astype(o_ref.dtype)
    elif epilogue == "sigmoid":
        o_ref[...] = jax.nn.sigmoid(acc).astype(o_ref.dtype)
    else:
        o_ref[...] = (acc * out_scale).astype(o_ref.dtype)


def norm_matmul(h, g, w, layer, *, tm, tn, out_dtype, epilogue="plain", out_scale=1.0, col0=0, n_cols=None):
    M, D = h.shape
    N = w.shape[2] - col0 if n_cols is None else n_cols
    assert M % tm == 0 and N % tn == 0 and col0 % tn == 0
    jb = col0 // tn
    if epilogue == "slabs":
        ns = tn // HEAD_DIM
        out_shape = jax.ShapeDtypeStruct((N // HEAD_DIM, M, HEAD_DIM), out_dtype)
        out_spec = pl.BlockSpec((ns, tm, HEAD_DIM), lambda i, j: (j, i, 0))
    else:
        out_shape = jax.ShapeDtypeStruct((M, N), out_dtype)
        out_spec = pl.BlockSpec((tm, tn), lambda i, j: (i, j))
    return pl.pallas_call(
        functools.partial(_norm_mm_kernel, epilogue=epilogue, out_scale=out_scale),
        grid=(M // tm, N // tn),
        in_specs=[
            _once((tm, D), lambda i, j: (i, 0)),
            pl.BlockSpec((1, D), lambda i, j: (0, 0)),
            pl.BlockSpec((None, D, tn), lambda i, j: (layer, 0, jb + j)),
        ],
        out_specs=out_spec,
        out_shape=out_shape,
        scratch_shapes=[pltpu.VMEM((tm, D), BF16)],
        compiler_params=_params("parallel", "arbitrary"),
        name="norm_matmul_" + epilogue,
    )(h, g.reshape(1, D), w)


def _mm_resid_kernel(a_ref, w_ref, h_ref, g_ref, o_ref):
    tm, tn = a_ref.shape[0], w_ref.shape[1]
    j = pl.program_id(1)
    col = pl.multiple_of(j * tn, tn)
    o_ref[:, pl.ds(col, tn)] = jnp.dot(a_ref[...], w_ref[...], preferred_element_type=F32)

    @pl.when(j == pl.num_programs(1) - 1)
    def _():
        chunk = min(NORM_CHUNK, tm)

        def body(c, carry):
            r = pl.multiple_of(c * chunk, chunk)
            o_ref[pl.ds(r, chunk), :] = h_ref[pl.ds(r, chunk), :] + _rms(o_ref[pl.ds(r, chunk), :], g_ref[...])
            return carry

        lax.fori_loop(0, tm // chunk, body, 0)


def matmul_residual(a, w, layer, h, g, *, tm, tn):
    M, K = a.shape
    N = w.shape[2]
    assert M % tm == 0 and N % tn == 0
    return pl.pallas_call(
        _mm_resid_kernel,
        grid=(M // tm, N // tn),
        in_specs=[
            _once((tm, K), lambda i, j: (i, 0)),
            pl.BlockSpec((None, K, tn), lambda i, j: (layer, 0, j)),
            _once((tm, N), lambda i, j: (i, 0)),
            pl.BlockSpec((1, N), lambda i, j: (0, 0)),
        ],
        out_specs=pl.BlockSpec((tm, N), lambda i, j: (i, 0)),
        out_shape=jax.ShapeDtypeStruct((M, N), F32),
        compiler_params=_params("parallel", "arbitrary"),
        name="matmul_residual",
    )(a, w, h, g.reshape(1, N))


def _pool_kernel(x_ref, halo_ref, g0_ref, w_ref, sc_ref, g1_ref, o_ref, *, seq_len):
    tm, D = x_ref.shape
    gc = D // len(POOL_WINDOWS)
    t0 = (pl.program_id(0) * tm) % seq_len
    x = x_ref[...]
    a = _rms(x, g0_ref[...])
    ha = _rms(halo_ref[...], g0_ref[...])
    ha = jnp.where(t0 == 0, 0.0, ha)
    full = jnp.concatenate([ha, a], axis=0)
    t1 = (t0 + 1 + lax.broadcasted_iota(jnp.int32, (tm, 1), 0)).astype(F32)
    ys = []
    for gi, win in enumerate(POOL_WINDOWS):
        s = full[:, gi * gc:(gi + 1) * gc]
        k = 1
        while k < win:
            s = s + pltpu.roll(s, k, 0)
            k *= 2
        inv_cnt = 1.0 / jnp.minimum(t1, float(win))
        pooled = s[POOL_HALO:] * inv_cnt
        d = (pooled - a[:, gi * gc:(gi + 1) * gc]).astype(BF16)
        ys.append(jnp.dot(d, w_ref[gi], preferred_element_type=F32))
    y = jnp.concatenate(ys, axis=-1) * sc_ref[...]
    o_ref[...] = x + _rms(y, g1_ref[...])


def pool_layer(x, g0, w, scale, g1, *, seq_len, tm=256):
    M, D = x.shape
    G, gc, _ = w.shape
    vec = pl.BlockSpec((1, D), lambda i: (0, 0))
    hb = tm // POOL_HALO
    return pl.pallas_call(
        functools.partial(_pool_kernel, seq_len=seq_len),
        grid=(M // tm,),
        in_specs=[
            pl.BlockSpec((tm, D), lambda i: (i, 0)),
            pl.BlockSpec((POOL_HALO, D), lambda i: (jnp.maximum(i * hb - 1, 0), 0)),
            vec,
            _once((G, gc, gc), lambda i: (0, 0, 0)),
            vec,
            vec,
        ],
        out_specs=pl.BlockSpec((tm, D), lambda i: (i, 0)),
        out_shape=jax.ShapeDtypeStruct((M, D), F32),
        compiler_params=_params("parallel"),
        name="pool_layer",
    )(x, x, g0.reshape(1, D), w, scale.reshape(1, D), g1.reshape(1, D))


def _xattn_kernel(h_ref, g0_ref, wq_ref, k_ref, v_ref, wo_ref, g1_ref, o_ref):
    h = h_ref[...]
    n = _rms(h, g0_ref[...]).astype(BF16)
    q = jnp.dot(n, wq_ref[...], preferred_element_type=F32).astype(BF16)
    k = k_ref[0]
    v = v_ref[0]
    scale = HEAD_DIM ** -0.5
    outs = []
    for hh in range(XA_HEADS):
        sl = slice(hh * HEAD_DIM, (hh + 1) * HEAD_DIM)
        s = lax.dot_general(q[:, sl], k[:, sl], _NT, preferred_element_type=F32) * scale
        m = jnp.max(s, axis=-1, keepdims=True)
        e = jnp.exp(s - m)
        p = e * (1.0 / jnp.sum(e, axis=-1, keepdims=True))
        outs.append(jnp.dot(p.astype(BF16), v[:, sl], preferred_element_type=F32))
    o = jnp.concatenate(outs, axis=-1).astype(BF16)
    c = jnp.dot(o, wo_ref[...], preferred_element_type=F32)
    o_ref[...] = h + _rms(c, g1_ref[...])


def xattn_layer(h, g0, wq, kv, wo, g1, *, seq_len, tm=256):
    M, D = h.shape
    xw = wq.shape[1]
    mem_len = kv.shape[1]
    per_seq = seq_len // tm
    vec = pl.BlockSpec((1, D), lambda i: (0, 0))
    return pl.pallas_call(
        _xattn_kernel,
        grid=(M // tm,),
        in_specs=[
            pl.BlockSpec((tm, D), lambda i: (i, 0)),
            vec,
            _once((D, xw), lambda i: (0, 0)),
            pl.BlockSpec((1, mem_len, xw), lambda i: (i // per_seq, 0, 0)),
            pl.BlockSpec((1, mem_len, xw), lambda i: (i // per_seq, 0, 1)),
            _once((xw, D), lambda i: (0, 0)),
            vec,
        ],
        out_specs=pl.BlockSpec((tm, D), lambda i: (i, 0)),
        out_shape=jax.ShapeDtypeStruct((M, D), F32),
        compiler_params=_params("parallel"),
        name="xattn_layer",
    )(h, g0.reshape(1, D), wq, kv, kv, wo, g1.reshape(1, D))


def _ffn_up_kernel(h_ref, halo_ref, g_ref, wg_ref, wu_ref, cw_ref, cb_ref, o_ref, a_scr, halo_scr, *, seq_len):
    tm = h_ref.shape[0]

    @pl.when(pl.program_id(1) == 0)
    def _():
        _norm_rows(h_ref, g_ref, a_scr)
        t0 = (pl.program_id(0) * tm) % seq_len
        hn = _rms(halo_ref[...], g_ref[...])
        halo_scr[...] = jnp.where(t0 == 0, 0.0, hn).astype(BF16)

    a = a_scr[...]
    gate = jnp.dot(a, wg_ref[...], preferred_element_type=F32)
    up = jnp.dot(a, wu_ref[...], preferred_element_type=F32)
    hg = jnp.dot(halo_scr[...], wg_ref[...], preferred_element_type=F32)
    row = lax.broadcasted_iota(jnp.int32, (tm, 1), 0)
    prev1 = hg[CONV_HALO - 1:CONV_HALO]
    prev2 = hg[CONV_HALO - 2:CONV_HALO - 1]
    g1 = jnp.where(row == 0, prev1, pltpu.roll(gate, 1, 0))
    g2 = jnp.where(row == 0, prev2, jnp.where(row == 1, prev1, pltpu.roll(gate, 2, 0)))
    cw = cw_ref[...]
    gc = cw[0:1] * g2 + cw[1:2] * g1 + cw[2:3] * gate + cb_ref[...]
    o_ref[...] = (gc * jax.nn.sigmoid(gc) * up).astype(o_ref.dtype)


def ffn_up(h, g, w_gu, layer, conv_w, conv_b, *, seq_len, tm=1024, tn=256):
    M, D = h.shape
    F = w_gu.shape[2] // 2
    assert F % tn == 0 and M % tm == 0 and seq_len % tm == 0
    nj = F // tn
    hb = tm // CONV_HALO
    return pl.pallas_call(
        functools.partial(_ffn_up_kernel, seq_len=seq_len),
        grid=(M // tm, nj),
        in_specs=[
            _once((tm, D), lambda i, j: (i, 0)),
            _once((CONV_HALO, D), lambda i, j: (jnp.maximum(i * hb - 1, 0), 0)),
            pl.BlockSpec((1, D), lambda i, j: (0, 0)),
            pl.BlockSpec((None, D, tn), lambda i, j: (layer, 0, j)),
            pl.BlockSpec((None, D, tn), lambda i, j: (layer, 0, j + nj)),
            pl.BlockSpec((conv_w.shape[0], tn), lambda i, j: (0, j)),
            pl.BlockSpec((1, tn), lambda i, j: (0, j)),
        ],
        out_specs=pl.BlockSpec((tm, tn), lambda i, j: (i, j)),
        out_shape=jax.ShapeDtypeStruct((M, F), BF16),
        scratch_shapes=[pltpu.VMEM((tm, D), BF16), pltpu.VMEM((CONV_HALO, D), BF16)],
        compiler_params=_params("parallel", "arbitrary"),
        name="ffn_up",
    )(h, h, g.reshape(1, D), w_gu, w_gu, conv_w, conv_b.reshape(1, F))


def _gelu_tanh(x):
    return 0.5 * x * (1.0 + jnp.tanh(0.7978845608028654 * (x + 0.044715 * x * x * x)))


def _compress_kernel(r_ref, pos_ref, w1_ref, b1_ref, w2_ref, o_ref):
    r = r_ref[0, 0].astype(F32)
    nch, half = r.shape
    pos = pos_ref[0]
    xa = (r + pos[0:1]).astype(BF16)
    xb = (r + pos[1:2]).astype(BF16)
    h1 = jnp.dot(xa, w1_ref[0, :half, :], preferred_element_type=F32)
    h2 = jnp.dot(xb, w1_ref[0, half:, :], preferred_element_type=F32)
    hid = h1 + pltpu.roll(h2, nch - 1, 0) + b1_ref[0]
    out = jnp.dot(_gelu_tanh(hid).astype(BF16), w2_ref[0], preferred_element_type=F32)
    row = lax.broadcasted_iota(jnp.int32, (nch, 1), 0)
    o_ref[0, 0, 0] = jnp.where(row < nch - 1, out, 0.0).astype(o_ref.dtype)


def compress(kv_slabs, pos, w1, b1, w2, *, batch, seq_len):
    G = NSA_KV_GROUPS
    nch = seq_len // CMP_STRIDE
    half = CMP_STRIDE * HEAD_DIM
    r = kv_slabs[:2 * G].reshape(2 * G, batch, nch, half)
    hidden = w1.shape[-1]
    return pl.pallas_call(
        _compress_kernel,
        grid=(2, batch, G),
        in_specs=[
            pl.BlockSpec((1, 1, nch, half), lambda kv, b, g: (kv * G + g, b, 0, 0)),
            pl.BlockSpec((1, 2, half), lambda kv, b, g: (kv, 0, 0)),
            pl.BlockSpec((1, 2 * half, hidden), lambda kv, b, g: (kv, 0, 0)),
            pl.BlockSpec((1, 1, hidden), lambda kv, b, g: (kv, 0, 0)),
            pl.BlockSpec((1, hidden, HEAD_DIM), lambda kv, b, g: (kv, 0, 0)),
        ],
        out_specs=pl.BlockSpec((1, 1, 1, nch, HEAD_DIM), lambda kv, b, g: (kv, b, g, 0, 0)),
        out_shape=jax.ShapeDtypeStruct((2, batch, G, nch, HEAD_DIM), BF16),
        compiler_params=_params("parallel", "parallel", "parallel"),
        name="nsa_compress",
    )(r, pos.reshape(2, 2, half), w1, b1.reshape(2, 1, hidden), w2)


def _nsa_kernel(slope_ref, q_ref, gate_ref, kc_ref, vc_ref, ks_ref, vs_ref, kw_ref, vw_ref, o_ref,
                qa_scr, ka_scr, va_scr, p_scr, oc_scr, mx_scr, acc_scr, used_ref, *, kt):
    J, dh = NSA_HPG, HEAD_DIM
    qb = q_ref.shape[0]
    seq_len = ks_ref.shape[2]
    ncp = kc_ref.shape[3]
    ns = seq_len // SEL_BLOCK
    g = pl.program_id(1)
    qi = pl.program_id(2)
    start = qi * qb

    @pl.when(qi == 0)
    def _():
        chunk = min(512, seq_len)

        def body(c, carry):
            r = pl.multiple_of(c * chunk, chunk)
            ka_scr[pl.ds(r, chunk), :dh] = ks_ref[0, 0, pl.ds(r, chunk), :]
            blk = (r + lax.broadcasted_iota(jnp.int32, (chunk, ns), 0)) // SEL_BLOCK
            ka_scr[pl.ds(r, chunk), dh:] = (blk == lax.broadcasted_iota(jnp.int32, (chunk, ns), 1)).astype(BF16)
            va_scr[pl.ds(r, chunk), :dh] = vs_ref[0, 0, pl.ds(r, chunk), :]
            va_scr[pl.ds(r, chunk), dh:] = jnp.ones((chunk, dh), BF16)
            return carry

        lax.fori_loop(0, seq_len // chunk, body, 0)

    q = q_ref[...]
    for j in range(J):
        qa_scr[j * qb:(j + 1) * qb, :dh] = q[:, j * dh:(j + 1) * dh]
    qr = qa_scr[:, :dh]
    slopes = [slope_ref[g * J + j] for j in range(J)]
    t_col = start + lax.broadcasted_iota(jnp.int32, (qb, 1), 0)

    lanes = mx_scr.shape[1]
    kc = kc_ref[0, 0, 0]
    sc = lax.dot_general(qr, kc, _NT, preferred_element_type=F32)
    c_end = lax.broadcasted_iota(jnp.int32, (1, ncp), 1) * CMP_STRIDE + (CMP_LEN - 1)
    bias_c = jnp.where(t_col >= c_end, 0.0, NEG_BIG)
    rel_c = (c_end - start).astype(F32)
    for j in range(J):
        rows = slice(j * qb, (j + 1) * qb)
        sj = sc[rows] + bias_c + slopes[j] * rel_c
        p_scr[rows, :ncp] = jnp.exp(sj - jnp.max(sj, axis=-1, keepdims=True)).astype(BF16)
    c_row = lax.broadcasted_iota(jnp.int32, (ncp, ns), 0) * CMP_STRIDE
    n_col = lax.broadcasted_iota(jnp.int32, (ncp, ns), 1) * SEL_BLOCK
    overlap = ((c_row < n_col + SEL_BLOCK) & (c_row + CMP_LEN > n_col)
               & (c_row < (ncp - 1) * CMP_STRIDE)).astype(BF16)
    rhs_c = jnp.concatenate([vc_ref[0, 0, 0], jnp.ones((ncp, lanes), BF16), overlap], axis=1)
    oc = jnp.dot(p_scr[:, :ncp], rhs_c, preferred_element_type=F32)
    has_key = t_col >= CMP_LEN - 1
    score = jnp.zeros((qb, ns), F32)
    for j in range(J):
        rows = slice(j * qb, (j + 1) * qb)
        inv = jnp.where(has_key, 1.0 / jnp.maximum(oc[rows, dh:dh + lanes], 1e-30), 0.0)
        oc_scr[rows] = oc[rows, :dh] * inv
        score = score + oc[rows, dh + lanes:] * inv[:, :ns]

    wl = WINDOW + qb
    w0 = pl.multiple_of(jnp.maximum(start - WINDOW, 0), qb)
    kw = kw_ref[0, 0, pl.ds(w0, wl), :]
    vw = jnp.concatenate([vw_ref[0, 0, pl.ds(w0, wl), :], jnp.ones((wl, dh), BF16)], axis=1)
    sw = lax.dot_general(qr, kw, _NT, preferred_element_type=F32)
    pos_w = w0 + lax.broadcasted_iota(jnp.int32, (1, wl), 1)
    bias_w = jnp.where((pos_w <= t_col) & (pos_w > t_col - WINDOW), 0.0, NEG_BIG)
    rel_w = (pos_w - start).astype(F32)
    for j in range(J):
        rows = slice(j * qb, (j + 1) * qb)
        sj = sw[rows] + bias_w + slopes[j] * rel_w
        p_scr[rows, :wl] = jnp.exp(sj - jnp.max(sj, axis=-1, keepdims=True)).astype(BF16)
    ow = jnp.dot(p_scr[:, :wl], vw, preferred_element_type=F32)
    o_win = ow[:, :dh] * (1.0 / jnp.maximum(ow[:, dh:], 1e-30))

    score_t = score.T
    n_idx = lax.broadcasted_iota(jnp.int32, (ns, qb), 0)
    cur = (start + lax.broadcasted_iota(jnp.int32, (1, qb), 1)) // SEL_BLOCK
    forced = (n_idx == 0) | (n_idx == cur) | (n_idx == cur - 1)
    score_t = jnp.where(n_idx > cur, -1.0, jnp.where(forced, FORCED_SCORE, score_t))
    sel_t = jnp.zeros((ns, qb), F32)
    for _ in range(min(N_SELECT, ns)):
        best = jnp.max(score_t, axis=0, keepdims=True)
        first = jnp.min(jnp.where(score_t == best, n_idx, ns), axis=0, keepdims=True)
        hit = n_idx == first
        sel_t = jnp.where(hit, 1.0, sel_t)
        score_t = jnp.where(hit, -2.0, score_t)
    block_bias = jnp.where(sel_t.T > 0.5, 0.0, NEG_BIG).astype(BF16)
    for j in range(J):
        qa_scr[j * qb:(j + 1) * qb, dh:] = block_bias
    bpt = kt // SEL_BLOCK
    for i in range(ns // bpt):
        used_ref[i] = (jnp.max(sel_t[i * bpt:(i + 1) * bpt, :]) > 0.5).astype(jnp.int32)

    pos_d = (start // kt) * kt + lax.broadcasted_iota(jnp.int32, (1, kt), 1)
    causal_bias = jnp.where(pos_d <= t_col, 0.0, NEG_BIG)

    def scores(i, j, s, diagonal):
        pos = i * kt + lax.broadcasted_iota(jnp.int32, (1, kt), 1)
        sj = s[j * qb:(j + 1) * qb] + slopes[j] * (pos - start).astype(F32)
        if diagonal:
            sj = sj + causal_bias
        return sj

    def max_tile(i, diagonal):
        k0 = pl.multiple_of(i * kt, kt)
        s = lax.dot_general(qa_scr[...], ka_scr[pl.ds(k0, kt), :], _NT, preferred_element_type=F32)
        for j in range(J):
            rows = slice(j * qb, (j + 1) * qb)
            sj = scores(i, j, s, diagonal)
            mx = mx_scr[rows]
            for c in range(kt // lanes):
                mx = jnp.maximum(mx, sj[:, c * lanes:(c + 1) * lanes])
            mx_scr[rows] = mx

    def pv_tile(i, diagonal):
        k0 = pl.multiple_of(i * kt, kt)
        s = lax.dot_general(qa_scr[...], ka_scr[pl.ds(k0, kt), :], _NT, preferred_element_type=F32)
        for j in range(J):
            rows = slice(j * qb, (j + 1) * qb)
            sj = scores(i, j, s, diagonal)
            m = mx_scr[rows]
            for c in range(kt // lanes):
                p_scr[rows, c * lanes:(c + 1) * lanes] = jnp.exp(sj[:, c * lanes:(c + 1) * lanes] - m).astype(BF16)
        acc_scr[...] += jnp.dot(p_scr[:, :kt], va_scr[pl.ds(k0, kt), :], preferred_element_type=F32)

    def run_pass(tile_fn):
        def full(i, carry):
            @pl.when(used_ref[i] > 0)
            def _():
                tile_fn(i, False)

            return carry

        n_full = start // kt
        lax.fori_loop(0, n_full, full, 0)
        tile_fn(n_full, True)

    mx_scr[...] = jnp.full(mx_scr.shape, NEG_BIG, F32)
    run_pass(max_tile)
    for j in range(J):
        rows = slice(j * qb, (j + 1) * qb)
        mx_scr[rows] = jnp.broadcast_to(jnp.max(mx_scr[rows], axis=-1, keepdims=True), (qb, lanes))
    acc_scr[...] = jnp.zeros(acc_scr.shape, F32)
    run_pass(pv_tile)

    gate = gate_ref[0]
    for j in range(J):
        rows = slice(j * qb, (j + 1) * qb)
        o_sel = acc_scr[rows, :dh] * (1.0 / jnp.maximum(acc_scr[rows, dh:], 1e-30))
        o = (gate[:, j:j + 1] * oc_scr[rows] + gate[:, J + j:J + j + 1] * o_sel
             + gate[:, 2 * J + j:2 * J + j + 1] * o_win[rows])
        o_ref[:, j * dh:(j + 1) * dh] = o.astype(o_ref.dtype)


def nsa_attention(q, gates, kv_cmp, kv_slabs, slopes, *, batch, seq_len, qb=128, kt=512):
    G, J, dh = NSA_KV_GROUPS, NSA_HPG, HEAD_DIM
    M = q.shape[0]
    nq = seq_len // qb
    ncp = kv_cmp.shape[3]
    ns = seq_len // SEL_BLOCK
    assert seq_len % kt == 0 and kt % qb == 0 and seq_len >= WINDOW + qb
    slabs = kv_slabs.reshape(kv_slabs.shape[0], batch, seq_len, dh)

    def slab(branch, is_v):
        base = (branch * 2 + is_v) * G
        return pl.BlockSpec((1, 1, seq_len, dh), lambda b, g, i, sl: (base + g, b, 0, 0))

    def cmp_spec(is_v):
        return pl.BlockSpec((1, 1, 1, ncp, dh), lambda b, g, i, sl: (is_v, b, g, 0, 0))

    grid_spec = pltpu.PrefetchScalarGridSpec(
        num_scalar_prefetch=1,
        grid=(batch, G, nq),
        in_specs=[
            pl.BlockSpec((qb, J * dh), lambda b, g, i, sl: (b * nq + i, g)),
            pl.BlockSpec((1, qb, NSA_BRANCHES * J), lambda b, g, i, sl: (g, b * nq + i, 0)),
            cmp_spec(0), cmp_spec(1),
            slab(1, 0), slab(1, 1), slab(2, 0), slab(2, 1),
        ],
        out_specs=pl.BlockSpec((qb, J * dh), lambda b, g, i, sl: (b * nq + i, g)),
        scratch_shapes=[
            pltpu.VMEM((J * qb, dh + ns), BF16),
            pltpu.VMEM((seq_len, dh + ns), BF16),
            pltpu.VMEM((seq_len, 2 * dh), BF16),
            pltpu.VMEM((J * qb, max(ncp, WINDOW + qb, kt)), BF16),
            pltpu.VMEM((J * qb, dh), F32),
            pltpu.VMEM((J * qb, LANES), F32),
            pltpu.VMEM((J * qb, 2 * dh), F32),
            pltpu.SMEM((seq_len // kt,), jnp.int32),
        ],
    )
    return pl.pallas_call(
        functools.partial(_nsa_kernel, kt=kt),
        grid_spec=grid_spec,
        out_shape=jax.ShapeDtypeStruct((M, G * J * dh), BF16),
        compiler_params=_params("parallel", "parallel", "arbitrary"),
        name="nsa_attention",
    )(slopes, q, gates, kv_cmp, kv_cmp, slabs, slabs, slabs, slabs)


def _ffn_layer(h, g0, g1, w_gu, w_down, layer, conv_w, conv_b, *, seq_len):
    act = ffn_up(h, g0, w_gu, layer, conv_w, conv_b, seq_len=seq_len)
    return matmul_residual(act, w_down, layer, h, g1, tm=512, tn=256)


def _xattn(h, mem_rows, mem_norm, g0, g1, wq, wkv, wo, layer, *, batch, seq_len):
    mem_len = mem_rows.shape[0] // batch
    kv = norm_matmul(mem_rows, mem_norm, wkv, layer, tm=mem_rows.shape[0], tn=wkv.shape[2] // 2, out_dtype=BF16)
    kv = kv.reshape(batch, mem_len, wkv.shape[2])
    return xattn_layer(h, g0, wq[layer], kv, wo[layer], g1, seq_len=seq_len)


def _nsa_layer(h, g0, g1, w_in, w_out, layer, cmp_pos, cmp_w1, cmp_b1, cmp_w2, *, batch, seq_len):
    G, J, dh = NSA_KV_GROUPS, NSA_HPG, HEAD_DIM
    qw = G * J * dh
    kvw = NSA_BRANCHES * 2 * G * dh
    q = norm_matmul(h, g0, w_in, layer, tm=1024, tn=512, out_dtype=BF16, n_cols=qw,
                    out_scale=dh ** -0.5)
    kv_slabs = norm_matmul(h, g0, w_in, layer, tm=1024, tn=512, out_dtype=BF16, col0=qw, n_cols=kvw,
                           epilogue="slabs")
    w_gate = w_in[layer:layer + 1, :, qw + kvw:]
    gates = norm_matmul(h, g0, w_gate, 0, tm=1024, tn=w_gate.shape[2], out_dtype=F32, epilogue="sigmoid")
    gates = gates.reshape(-1, NSA_BRANCHES, G, J).transpose(2, 0, 1, 3).reshape(G, -1, NSA_BRANCHES * J)
    kv_cmp = compress(kv_slabs, cmp_pos, cmp_w1.astype(BF16), cmp_b1, cmp_w2.astype(BF16),
                      batch=batch, seq_len=seq_len)
    n_heads = G * J
    slopes = 2.0 ** (-8.0 * jnp.arange(1, n_heads + 1, dtype=F32) / n_heads)
    o = nsa_attention(q, gates, kv_cmp, kv_slabs, slopes, batch=batch, seq_len=seq_len)
    return matmul_residual(o, w_out, layer, h, g1, tm=512, tn=512)


def kernel(x, mem, ln_mix, ln_xa, ln_ffn, mem_norm, pool_w, pool_scale, nsa_w_in, nsa_w_out, nsa_cmp_pos, nsa_cmp_w1, nsa_cmp_b1, nsa_cmp_w2, xa_wq, xa_wkv, xa_wo, ffn_w_gu, ffn_conv_w, ffn_conv_b, ffn_w_down):
    B, S, D = x.shape
    depth = ln_mix.shape[0]
    h = x.reshape(B * S, D)
    mem_rows = mem.reshape(B * mem.shape[1], D)
    nsa_w_in, nsa_w_out = nsa_w_in.astype(BF16), nsa_w_out.astype(BF16)
    xa_wq, xa_wkv, xa_wo = xa_wq.astype(BF16), xa_wkv.astype(BF16), xa_wo.astype(BF16)
    ffn_w_gu, ffn_w_down = ffn_w_gu.astype(BF16), ffn_w_down.astype(BF16)
    for i in range(depth):
        j = i // 2
        if i % 2 == 0:
            h = pool_layer(h, ln_mix[i, 0], pool_w[j].astype(BF16), pool_scale[j], ln_mix[i, 1], seq_len=S)
        else:
            h = _nsa_layer(h, ln_mix[i, 0], ln_mix[i, 1], nsa_w_in, nsa_w_out, j, nsa_cmp_pos[j],
                           nsa_cmp_w1[j], nsa_cmp_b1[j], nsa_cmp_w2[j], batch=B, seq_len=S)
        h = _xattn(h, mem_rows, mem_norm, ln_xa[i, 0], ln_xa[i, 1], xa_wq, xa_wkv, xa_wo, i, batch=B, seq_len=S)
        h = _ffn_layer(h, ln_ffn[i, 0], ln_ffn[i, 1], ffn_w_gu, ffn_w_down, i, ffn_conv_w[i], ffn_conv_b[i],
                       seq_len=S)
    return h.reshape(B, S, D)
```

```python
import functools

import jax
import jax.numpy as jnp
from jax import lax
from jax.experimental import pallas as pl
from jax.experimental.pallas import tpu as pltpu

F32 = jnp.float32
BF16 = jnp.bfloat16

RMS_EPS = 1e-6
NEG_BIG = -1e30
HEAD_DIM = 128
POOL_WINDOWS = (2, 4, 8, 16)
POOL_HALO = 16
NSA_KV_GROUPS = 4
NSA_HPG = 8
NSA_BRANCHES = 3
CMP_LEN = 32
CMP_STRIDE = 16
SEL_BLOCK = 64
N_SELECT = 16
WINDOW = 512
FORCED_SCORE = 1e6
XA_HEADS = 4
CONV_HALO = 8
VMEM_LIMIT = 56 * 1024 * 1024
NORM_CHUNK = 128
LANES = 128

_NT = (((1,), (1,)), ((), ()))


def _params(*sem):
    return pltpu.CompilerParams(dimension_semantics=sem, vmem_limit_bytes=VMEM_LIMIT)


def _rms(x, g):
    ms = jnp.mean(x * x, axis=-1, keepdims=True)
    return x * lax.rsqrt(ms + RMS_EPS) * g


def _norm_rows(h_ref, g_ref, a_scr):
    tm = h_ref.shape[0]
    chunk = min(NORM_CHUNK, tm)

    def body(c, carry):
        r = pl.multiple_of(c * chunk, chunk)
        a_scr[pl.ds(r, chunk), :] = _rms(h_ref[pl.ds(r, chunk), :], g_ref[...]).astype(BF16)
        return carry

    lax.fori_loop(0, tm // chunk, body, 0)


def _once(shape, index_map):
    return pl.BlockSpec(shape, index_map, pipeline_mode=pl.Buffered(1))


def _norm_mm_kernel(h_ref, g_ref, w_ref, o_ref, a_scr):
    @pl.when(pl.program_id(1) == 0)
    def _():
        _norm_rows(h_ref, g_ref, a_scr)

    o_ref[...] = jnp.dot(a_scr[...], w_ref[...], preferred_element_type=F32).astype(o_ref.dtype)


def norm_matmul(h, g, w, layer, *, tm, tn, out_dtype):
    M, D = h.shape
    N = w.shape[2]
    assert M % tm == 0 and N % tn == 0
    return pl.pallas_call(
        _norm_mm_kernel,
        grid=(M // tm, N // tn),
        in_specs=[
            _once((tm, D), lambda i, j: (i, 0)),
            pl.BlockSpec((1, D), lambda i, j: (0, 0)),
            pl.BlockSpec((None, D, tn), lambda i, j: (layer, 0, j)),
        ],
        out_specs=pl.BlockSpec((tm, tn), lambda i, j: (i, j)),
        out_shape=jax.ShapeDtypeStruct((M, N), out_dtype),
        scratch_shapes=[pltpu.VMEM((tm, D), BF16)],
        compiler_params=_params("parallel", "arbitrary"),
        name="norm_matmul",
    )(h, g.reshape(1, D), w)


def _nsa_proj_kernel(h_ref, g_ref, w_ref, wg_ref, q_ref, kv_ref, gate_ref, a_scr, *, nq, nkv, q_scale):
    j = pl.program_id(1)

    @pl.when(j == 0)
    def _():
        _norm_rows(h_ref, g_ref, a_scr)

    @pl.when(j < nq)
    def _():
        acc = jnp.dot(a_scr[...], w_ref[...], preferred_element_type=F32)
        q_ref[...] = (acc * q_scale).astype(q_ref.dtype)

    @pl.when((j >= nq) & (j < nq + nkv))
    def _():
        acc = jnp.dot(a_scr[...], w_ref[...], preferred_element_type=F32)
        for s in range(kv_ref.shape[0]):
            kv_ref[s] = acc[:, s * HEAD_DIM:(s + 1) * HEAD_DIM].astype(kv_ref.dtype)

    @pl.when(j == nq + nkv)
    def _():
        sig = jax.nn.sigmoid(jnp.dot(a_scr[...], wg_ref[...], preferred_element_type=F32))
        per_group = gate_ref.shape[2]
        for grp in range(gate_ref.shape[0]):
            gate_ref[grp] = sig[:, grp * per_group:(grp + 1) * per_group]


def nsa_in_proj(h, g, w_in, layer, w_gate, *, q_width, kv_width, tm=1024, tn=512):
    M, D = h.shape
    G = NSA_KV_GROUPS
    n_gate = w_gate.shape[1]
    assert M % tm == 0 and q_width % tn == 0 and kv_width % tn == 0
    nq, nkv = q_width // tn, kv_width // tn
    spt = tn // HEAD_DIM
    return pl.pallas_call(
        functools.partial(_nsa_proj_kernel, nq=nq, nkv=nkv, q_scale=HEAD_DIM ** -0.5),
        grid=(M // tm, nq + nkv + 1),
        in_specs=[
            _once((tm, D), lambda i, j: (i, 0)),
            pl.BlockSpec((1, D), lambda i, j: (0, 0)),
            pl.BlockSpec((None, D, tn), lambda i, j: (layer, 0, jnp.minimum(j, nq + nkv - 1))),
            pl.BlockSpec((D, n_gate), lambda i, j: (0, 0)),
        ],
        out_specs=[
            pl.BlockSpec((tm, tn), lambda i, j: (i, jnp.minimum(j, nq - 1))),
            pl.BlockSpec((spt, tm, HEAD_DIM), lambda i, j: (jnp.clip(j - nq, 0, nkv - 1), i, 0)),
            pl.BlockSpec((G, tm, n_gate // G), lambda i, j: (0, i, 0)),
        ],
        out_shape=[
            jax.ShapeDtypeStruct((M, q_width), BF16),
            jax.ShapeDtypeStruct((kv_width // HEAD_DIM, M, HEAD_DIM), BF16),
            jax.ShapeDtypeStruct((G, M, n_gate // G), F32),
        ],
        scratch_shapes=[pltpu.VMEM((tm, D), BF16)],
        compiler_params=_params("parallel", "arbitrary"),
        name="nsa_in_proj",
    )(h, g.reshape(1, D), w_in, w_gate)


def _mm_kernel(a_ref, w_ref, o_ref):
    o_ref[...] = jnp.dot(a_ref[...], w_ref[...], preferred_element_type=F32).astype(o_ref.dtype)


def matmul(a, w, layer, *, tm, tn, out_dtype=F32):
    M, K = a.shape
    N = w.shape[2]
    assert M % tm == 0 and N % tn == 0
    return pl.pallas_call(
        _mm_kernel,
        grid=(M // tm, N // tn),
        in_specs=[
            _once((tm, K), lambda i, j: (i, 0)),
            pl.BlockSpec((None, K, tn), lambda i, j: (layer, 0, j)),
        ],
        out_specs=pl.BlockSpec((tm, tn), lambda i, j: (i, j)),
        out_shape=jax.ShapeDtypeStruct((M, N), out_dtype),
        compiler_params=_params("parallel", "arbitrary"),
        name="matmul",
    )(a, w)


def _resid_kernel(h_ref, f_ref, g_ref, o_ref):
    o_ref[...] = h_ref[...] + _rms(f_ref[...], g_ref[...])


def residual_norm(h, f, g, *, tm=256):
    M, D = h.shape
    row = pl.BlockSpec((tm, D), lambda i: (i, 0))
    return pl.pallas_call(
        _resid_kernel,
        grid=(M // tm,),
        in_specs=[row, row, pl.BlockSpec((1, D), lambda i: (0, 0))],
        out_specs=row,
        out_shape=jax.ShapeDtypeStruct((M, D), F32),
        compiler_params=_params("parallel"),
        name="residual_norm",
    )(h, f, g.reshape(1, D))


def _pool_kernel(x_ref, halo_ref, g0_ref, w_ref, sc_ref, g1_ref, o_ref, *, seq_len):
    tm, D = x_ref.shape
    gc = D // len(POOL_WINDOWS)
    t0 = (pl.program_id(0) * tm) % seq_len
    x = x_ref[...]
    a = _rms(x, g0_ref[...])
    ha = _rms(halo_ref[...], g0_ref[...])
    ha = jnp.where(t0 == 0, 0.0, ha)
    full = jnp.concatenate([ha, a], axis=0)
    t1 = (t0 + 1 + lax.broadcasted_iota(jnp.int32, (tm, 1), 0)).astype(F32)
    ys = []
    for gi, win in enumerate(POOL_WINDOWS):
        s = full[:, gi * gc:(gi + 1) * gc]
        k = 1
        while k < win:
            s = s + pltpu.roll(s, k, 0)
            k *= 2
        inv_cnt = 1.0 / jnp.minimum(t1, float(win))
        pooled = s[POOL_HALO:] * inv_cnt
        d = (pooled - a[:, gi * gc:(gi + 1) * gc]).astype(BF16)
        ys.append(jnp.dot(d, w_ref[gi], preferred_element_type=F32))
    y = jnp.concatenate(ys, axis=-1) * sc_ref[...]
    o_ref[...] = x + _rms(y, g1_ref[...])


def pool_layer(x, g0, w, scale, g1, *, seq_len, tm=256):
    M, D = x.shape
    G, gc, _ = w.shape
    vec = pl.BlockSpec((1, D), lambda i: (0, 0))
    hb = tm // POOL_HALO
    return pl.pallas_call(
        functools.partial(_pool_kernel, seq_len=seq_len),
        grid=(M // tm,),
        in_specs=[
            pl.BlockSpec((tm, D), lambda i: (i, 0)),
            pl.BlockSpec((POOL_HALO, D), lambda i: (jnp.maximum(i * hb - 1, 0), 0)),
            vec,
            _once((G, gc, gc), lambda i: (0, 0, 0)),
            vec,
            vec,
        ],
        out_specs=pl.BlockSpec((tm, D), lambda i: (i, 0)),
        out_shape=jax.ShapeDtypeStruct((M, D), F32),
        compiler_params=_params("parallel"),
        name="pool_layer",
    )(x, x, g0.reshape(1, D), w, scale.reshape(1, D), g1.reshape(1, D))


def _xattn_kernel(h_ref, g0_ref, wq_ref, k_ref, v_ref, wo_ref, g1_ref, o_ref):
    h = h_ref[...]
    n = _rms(h, g0_ref[...]).astype(BF16)
    q = jnp.dot(n, wq_ref[...], preferred_element_type=F32).astype(BF16)
    k = k_ref[0]
    v = v_ref[0]
    scale = HEAD_DIM ** -0.5
    outs = []
    for hh in range(XA_HEADS):
        sl = slice(hh * HEAD_DIM, (hh + 1) * HEAD_DIM)
        s = lax.dot_general(q[:, sl], k[:, sl], _NT, preferred_element_type=F32) * scale
        m = jnp.max(s, axis=-1, keepdims=True)
        e = jnp.exp(s - m)
        p = e * (1.0 / jnp.sum(e, axis=-1, keepdims=True))
        outs.append(jnp.dot(p.astype(BF16), v[:, sl], preferred_element_type=F32))
    o = jnp.concatenate(outs, axis=-1).astype(BF16)
    c = jnp.dot(o, wo_ref[...], preferred_element_type=F32)
    o_ref[...] = h + _rms(c, g1_ref[...])


def xattn_layer(h, g0, wq, kv, wo, g1, *, seq_len, tm=256):
    M, D = h.shape
    xw = wq.shape[1]
    mem_len = kv.shape[1]
    per_seq = seq_len // tm
    vec = pl.BlockSpec((1, D), lambda i: (0, 0))
    return pl.pallas_call(
        _xattn_kernel,
        grid=(M // tm,),
        in_specs=[
            pl.BlockSpec((tm, D), lambda i: (i, 0)),
            vec,
            _once((D, xw), lambda i: (0, 0)),
            pl.BlockSpec((1, mem_len, xw), lambda i: (i // per_seq, 0, 0)),
            pl.BlockSpec((1, mem_len, xw), lambda i: (i // per_seq, 0, 1)),
            _once((xw, D), lambda i: (0, 0)),
            vec,
        ],
        out_specs=pl.BlockSpec((tm, D), lambda i: (i, 0)),
        out_shape=jax.ShapeDtypeStruct((M, D), F32),
        compiler_params=_params("parallel"),
        name="xattn_layer",
    )(h, g0.reshape(1, D), wq, kv, kv, wo, g1.reshape(1, D))


def _ffn_up_kernel(h_ref, halo_ref, g_ref, wg_ref, wu_ref, cw_ref, cb_ref, o_ref, a_scr, halo_scr, *, seq_len):
    tm = h_ref.shape[0]

    @pl.when(pl.program_id(1) == 0)
    def _():
        _norm_rows(h_ref, g_ref, a_scr)
        t0 = (pl.program_id(0) * tm) % seq_len
        hn = _rms(halo_ref[...], g_ref[...])
        halo_scr[...] = jnp.where(t0 == 0, 0.0, hn).astype(BF16)

    a = a_scr[...]
    gate = jnp.dot(a, wg_ref[...], preferred_element_type=F32)
    up = jnp.dot(a, wu_ref[...], preferred_element_type=F32)
    hg = jnp.dot(halo_scr[...], wg_ref[...], preferred_element_type=F32)
    row = lax.broadcasted_iota(jnp.int32, (tm, 1), 0)
    prev1 = hg[CONV_HALO - 1:CONV_HALO]
    prev2 = hg[CONV_HALO - 2:CONV_HALO - 1]
    g1 = jnp.where(row == 0, prev1, pltpu.roll(gate, 1, 0))
    g2 = jnp.where(row == 0, prev2, jnp.where(row == 1, prev1, pltpu.roll(gate, 2, 0)))
    cw = cw_ref[...]
    gc = cw[0:1] * g2 + cw[1:2] * g1 + cw[2:3] * gate + cb_ref[...]
    o_ref[...] = (gc * jax.nn.sigmoid(gc) * up).astype(o_ref.dtype)


def ffn_up(h, g, w_gu, layer, conv_w, conv_b, *, seq_len, tm=1024, tn=256):
    M, D = h.shape
    F = w_gu.shape[2] // 2
    assert F % tn == 0 and M % tm == 0 and seq_len % tm == 0
    nj = F // tn
    hb = tm // CONV_HALO
    return pl.pallas_call(
        functools.partial(_ffn_up_kernel, seq_len=seq_len),
        grid=(M // tm, nj),
        in_specs=[
            _once((tm, D), lambda i, j: (i, 0)),
            _once((CONV_HALO, D), lambda i, j: (jnp.maximum(i * hb - 1, 0), 0)),
            pl.BlockSpec((1, D), lambda i, j: (0, 0)),
            pl.BlockSpec((None, D, tn), lambda i, j: (layer, 0, j)),
            pl.BlockSpec((None, D, tn), lambda i, j: (layer, 0, j + nj)),
            pl.BlockSpec((conv_w.shape[0], tn), lambda i, j: (0, j)),
            pl.BlockSpec((1, tn), lambda i, j: (0, j)),
        ],
        out_specs=pl.BlockSpec((tm, tn), lambda i, j: (i, j)),
        out_shape=jax.ShapeDtypeStruct((M, F), BF16),
        scratch_shapes=[pltpu.VMEM((tm, D), BF16), pltpu.VMEM((CONV_HALO, D), BF16)],
        compiler_params=_params("parallel", "arbitrary"),
        name="ffn_up",
    )(h, h, g.reshape(1, D), w_gu, w_gu, conv_w, conv_b.reshape(1, F))


def _gelu_tanh(x):
    return 0.5 * x * (1.0 + jnp.tanh(0.7978845608028654 * (x + 0.044715 * x * x * x)))


def _compress_kernel(r_ref, pos_ref, w1_ref, b1_ref, w2_ref, o_ref):
    r = r_ref[0, 0].astype(F32)
    nch, half = r.shape
    pos = pos_ref[0]
    xa = (r + pos[0:1]).astype(BF16)
    xb = (r + pos[1:2]).astype(BF16)
    h1 = jnp.dot(xa, w1_ref[0, :half, :], preferred_element_type=F32)
    h2 = jnp.dot(xb, w1_ref[0, half:, :], preferred_element_type=F32)
    hid = h1 + pltpu.roll(h2, nch - 1, 0) + b1_ref[0]
    out = jnp.dot(_gelu_tanh(hid).astype(BF16), w2_ref[0], preferred_element_type=F32)
    row = lax.broadcasted_iota(jnp.int32, (nch, 1), 0)
    o_ref[0, 0, 0] = jnp.where(row < nch - 1, out, 0.0).astype(o_ref.dtype)


def compress(kv_slabs, pos, w1, b1, w2, *, batch, seq_len):
    G = NSA_KV_GROUPS
    nch = seq_len // CMP_STRIDE
    half = CMP_STRIDE * HEAD_DIM
    r = kv_slabs[:2 * G].reshape(2 * G, batch, nch, half)
    hidden = w1.shape[-1]
    return pl.pallas_call(
        _compress_kernel,
        grid=(2, batch, G),
        in_specs=[
            pl.BlockSpec((1, 1, nch, half), lambda kv, b, g: (kv * G + g, b, 0, 0)),
            pl.BlockSpec((1, 2, half), lambda kv, b, g: (kv, 0, 0)),
            pl.BlockSpec((1, 2 * half, hidden), lambda kv, b, g: (kv, 0, 0)),
            pl.BlockSpec((1, 1, hidden), lambda kv, b, g: (kv, 0, 0)),
            pl.BlockSpec((1, hidden, HEAD_DIM), lambda kv, b, g: (kv, 0, 0)),
        ],
        out_specs=pl.BlockSpec((1, 1, 1, nch, HEAD_DIM), lambda kv, b, g: (kv, b, g, 0, 0)),
        out_shape=jax.ShapeDtypeStruct((2, batch, G, nch, HEAD_DIM), BF16),
        compiler_params=_params("parallel", "parallel", "parallel"),
        name="nsa_compress",
    )(r, pos.reshape(2, 2, half), w1, b1.reshape(2, 1, hidden), w2)


def _dot_row_halves(p_ref, width, rhs):
    half = p_ref.shape[0] // 2
    return jnp.concatenate([jnp.dot(p_ref[:half, :width], rhs, preferred_element_type=F32),
                            jnp.dot(p_ref[half:, :width], rhs, preferred_element_type=F32)], axis=0)


def _nsa_kernel(slope_ref, q_ref, gate_ref, kc_ref, vc_ref, ks_ref, vs_ref, kw_ref, vw_ref, o_ref,
                qa_scr, ka_scr, va_scr, p_scr, oc_scr, mx_scr, acc_scr, used_ref, *, kt):
    J, dh = NSA_HPG, HEAD_DIM
    qb = q_ref.shape[0]
    seq_len = ks_ref.shape[2]
    ncp = kc_ref.shape[3]
    ns = seq_len // SEL_BLOCK
    g = pl.program_id(1)
    qi = pl.program_id(2)
    start = qi * qb

    @pl.when(qi == 0)
    def _():
        chunk = min(512, seq_len)

        def body(c, carry):
            r = pl.multiple_of(c * chunk, chunk)
            ka_scr[pl.ds(r, chunk), :dh] = ks_ref[0, 0, pl.ds(r, chunk), :]
            blk = (r + lax.broadcasted_iota(jnp.int32, (chunk, ns), 0)) // SEL_BLOCK
            ka_scr[pl.ds(r, chunk), dh:] = (blk == lax.broadcasted_iota(jnp.int32, (chunk, ns), 1)).astype(BF16)
            va_scr[pl.ds(r, chunk), :dh] = vs_ref[0, 0, pl.ds(r, chunk), :]
            va_scr[pl.ds(r, chunk), dh:] = jnp.ones((chunk, dh), BF16)
            return carry

        lax.fori_loop(0, seq_len // chunk, body, 0)

    q = q_ref[...]
    for j in range(J):
        qa_scr[j * qb:(j + 1) * qb, :dh] = q[:, j * dh:(j + 1) * dh]
    qr = qa_scr[:, :dh]
    slopes = [slope_ref[g * J + j] for j in range(J)]
    t_col = start + lax.broadcasted_iota(jnp.int32, (qb, 1), 0)

    lanes = mx_scr.shape[1]
    kc = kc_ref[0, 0, 0]
    sc = lax.dot_general(qr, kc, _NT, preferred_element_type=F32)
    c_end = lax.broadcasted_iota(jnp.int32, (1, ncp), 1) * CMP_STRIDE + (CMP_LEN - 1)
    bias_c = jnp.where(t_col >= c_end, 0.0, NEG_BIG)
    rel_c = (c_end - start).astype(F32)
    for j in range(J):
        rows = slice(j * qb, (j + 1) * qb)
        sj = sc[rows] + bias_c + slopes[j] * rel_c
        p_scr[rows, :ncp] = jnp.exp(sj - jnp.max(sj, axis=-1, keepdims=True)).astype(BF16)
    c_row = lax.broadcasted_iota(jnp.int32, (ncp, ns), 0) * CMP_STRIDE
    n_col = lax.broadcasted_iota(jnp.int32, (ncp, ns), 1) * SEL_BLOCK
    overlap = ((c_row < n_col + SEL_BLOCK) & (c_row + CMP_LEN > n_col)
               & (c_row < (ncp - 1) * CMP_STRIDE)).astype(BF16)
    rhs_c = jnp.concatenate([vc_ref[0, 0, 0], jnp.ones((ncp, lanes), BF16), overlap], axis=1)
    oc = _dot_row_halves(p_scr, ncp, rhs_c)
    has_key = t_col >= CMP_LEN - 1
    score = jnp.zeros((qb, ns), F32)
    for j in range(J):
        rows = slice(j * qb, (j + 1) * qb)
        inv = jnp.where(has_key, 1.0 / jnp.maximum(oc[rows, dh:dh + lanes], 1e-30), 0.0)
        oc_scr[rows] = oc[rows, :dh] * inv
        score = score + oc[rows, dh + lanes:] * inv[:, :ns]

    wl = WINDOW + qb
    w0 = pl.multiple_of(jnp.maximum(start - WINDOW, 0), qb)
    kw = kw_ref[0, 0, pl.ds(w0, wl), :]
    vw = jnp.concatenate([vw_ref[0, 0, pl.ds(w0, wl), :], jnp.ones((wl, dh), BF16)], axis=1)
    sw = lax.dot_general(qr, kw, _NT, preferred_element_type=F32)
    pos_w = w0 + lax.broadcasted_iota(jnp.int32, (1, wl), 1)
    bias_w = jnp.where((pos_w <= t_col) & (pos_w > t_col - WINDOW), 0.0, NEG_BIG)
    rel_w = (pos_w - start).astype(F32)
    for j in range(J):
        rows = slice(j * qb, (j + 1) * qb)
        sj = sw[rows] + bias_w + slopes[j] * rel_w
        p_scr[rows, :wl] = jnp.exp(sj - jnp.max(sj, axis=-1, keepdims=True)).astype(BF16)
    ow = _dot_row_halves(p_scr, wl, vw)
    o_win = ow[:, :dh] * (1.0 / jnp.maximum(ow[:, dh:], 1e-30))

    score_t = score.T
    n_idx = lax.broadcasted_iota(jnp.int32, (ns, qb), 0)
    cur = (start + lax.broadcasted_iota(jnp.int32, (1, qb), 1)) // SEL_BLOCK
    forced = (n_idx == 0) | (n_idx == cur) | (n_idx == cur - 1)
    score_t = jnp.where(n_idx > cur, -1.0, jnp.where(forced, FORCED_SCORE, score_t))
    sel_t = jnp.zeros((ns, qb), F32)
    for _ in range(min(N_SELECT, ns)):
        best = jnp.max(score_t, axis=0, keepdims=True)
        first = jnp.min(jnp.where(score_t == best, n_idx, ns), axis=0, keepdims=True)
        hit = n_idx == first
        sel_t = jnp.where(hit, 1.0, sel_t)
        score_t = jnp.where(hit, -2.0, score_t)
    block_bias = jnp.where(sel_t.T > 0.5, 0.0, NEG_BIG).astype(BF16)
    for j in range(J):
        qa_scr[j * qb:(j + 1) * qb, dh:] = block_bias
    bpt = kt // SEL_BLOCK
    for i in range(ns // bpt):
        used_ref[i] = (jnp.max(sel_t[i * bpt:(i + 1) * bpt, :]) > 0.5).astype(jnp.int32)

    pos_d = (start // kt) * kt + lax.broadcasted_iota(jnp.int32, (1, kt), 1)
    causal_bias = jnp.where(pos_d <= t_col, 0.0, NEG_BIG)

    def scores(i, j, s, diagonal):
        pos = i * kt + lax.broadcasted_iota(jnp.int32, (1, kt), 1)
        sj = s[j * qb:(j + 1) * qb] + slopes[j] * (pos - start).astype(F32)
        if diagonal:
            sj = sj + causal_bias
        return sj

    def max_tile(i, diagonal):
        k0 = pl.multiple_of(i * kt, kt)
        s = lax.dot_general(qa_scr[...], ka_scr[pl.ds(k0, kt), :], _NT, preferred_element_type=F32)
        for j in range(J):
            rows = slice(j * qb, (j + 1) * qb)
            sj = scores(i, j, s, diagonal)
            mx = mx_scr[rows]
            for c in range(kt // lanes):
                mx = jnp.maximum(mx, sj[:, c * lanes:(c + 1) * lanes])
            mx_scr[rows] = mx

    def pv_tile(i, diagonal):
        k0 = pl.multiple_of(i * kt, kt)
        s = lax.dot_general(qa_scr[...], ka_scr[pl.ds(k0, kt), :], _NT, preferred_element_type=F32)
        for j in range(J):
            rows = slice(j * qb, (j + 1) * qb)
            sj = scores(i, j, s, diagonal)
            m = mx_scr[rows]
            for c in range(kt // lanes):
                p_scr[rows, c * lanes:(c + 1) * lanes] = jnp.exp(sj[:, c * lanes:(c + 1) * lanes] - m).astype(BF16)
        acc_scr[...] += _dot_row_halves(p_scr, kt, va_scr[pl.ds(k0, kt), :])

    def run_pass(tile_fn):
        def full(i, carry):
            @pl.when(used_ref[i] > 0)
            def _():
                tile_fn(i, False)

            return carry

        n_full = start // kt
        lax.fori_loop(0, n_full, full, 0)
        tile_fn(n_full, True)

    mx_scr[...] = jnp.full(mx_scr.shape, NEG_BIG, F32)
    run_pass(max_tile)
    for j in range(J):
        rows = slice(j * qb, (j + 1) * qb)
        mx_scr[rows] = jnp.broadcast_to(jnp.max(mx_scr[rows], axis=-1, keepdims=True), (qb, lanes))
    acc_scr[...] = jnp.zeros(acc_scr.shape, F32)
    run_pass(pv_tile)

    gate = gate_ref[0]
    for j in range(J):
        rows = slice(j * qb, (j + 1) * qb)
        o_sel = acc_scr[rows, :dh] * (1.0 / jnp.maximum(acc_scr[rows, dh:], 1e-30))
        o = (gate[:, j:j + 1] * oc_scr[rows] + gate[:, J + j:J + j + 1] * o_sel
             + gate[:, 2 * J + j:2 * J + j + 1] * o_win[rows])
        o_ref[:, j * dh:(j + 1) * dh] = o.astype(o_ref.dtype)


def nsa_attention(q, gates, kv_cmp, kv_slabs, slopes, *, batch, seq_len, qb=128, kt=512):
    G, J, dh = NSA_KV_GROUPS, NSA_HPG, HEAD_DIM
    M = q.shape[0]
    nq = seq_len // qb
    ncp = kv_cmp.shape[3]
    ns = seq_len // SEL_BLOCK
    assert seq_len % kt == 0 and kt % qb == 0 and seq_len >= WINDOW + qb
    slabs = kv_slabs.reshape(kv_slabs.shape[0], batch, seq_len, dh)

    def slab(branch, is_v):
        base = (branch * 2 + is_v) * G
        return pl.BlockSpec((1, 1, seq_len, dh), lambda b, g, i, sl: (base + g, b, 0, 0))

    def cmp_spec(is_v):
        return pl.BlockSpec((1, 1, 1, ncp, dh), lambda b, g, i, sl: (is_v, b, g, 0, 0))

    grid_spec = pltpu.PrefetchScalarGridSpec(
        num_scalar_prefetch=1,
        grid=(batch, G, nq),
        in_specs=[
            pl.BlockSpec((qb, J * dh), lambda b, g, i, sl: (b * nq + i, g)),
            pl.BlockSpec((1, qb, NSA_BRANCHES * J), lambda b, g, i, sl: (g, b * nq + i, 0)),
            cmp_spec(0), cmp_spec(1),
            slab(1, 0), slab(1, 1), slab(2, 0), slab(2, 1),
        ],
        out_specs=pl.BlockSpec((qb, J * dh), lambda b, g, i, sl: (b * nq + i, g)),
        scratch_shapes=[
            pltpu.VMEM((J * qb, dh + ns), BF16),
            pltpu.VMEM((seq_len, dh + ns), BF16),
            pltpu.VMEM((seq_len, 2 * dh), BF16),
            pltpu.VMEM((J * qb, max(ncp, WINDOW + qb, kt)), BF16),
            pltpu.VMEM((J * qb, dh), F32),
            pltpu.VMEM((J * qb, LANES), F32),
            pltpu.VMEM((J * qb, 2 * dh), F32),
            pltpu.SMEM((seq_len // kt,), jnp.int32),
        ],
    )
    return pl.pallas_call(
        functools.partial(_nsa_kernel, kt=kt),
        grid_spec=grid_spec,
        out_shape=jax.ShapeDtypeStruct((M, G * J * dh), BF16),
        compiler_params=_params("parallel", "parallel", "arbitrary"),
        name="nsa_attention",
    )(slopes, q, gates, kv_cmp, kv_cmp, slabs, slabs, slabs, slabs)


def _ffn_layer(h, g0, g1, w_gu, w_down, layer, conv_w, conv_b, *, seq_len):
    act = ffn_up(h, g0, w_gu, layer, conv_w, conv_b, seq_len=seq_len)
    f = matmul(act, w_down, layer, tm=512, tn=512)
    return residual_norm(h, f, g1)


def _xattn(h, mem_rows, mem_norm, g0, g1, wq, wkv, wo, layer, *, batch, seq_len):
    mem_len = mem_rows.shape[0] // batch
    kv = norm_matmul(mem_rows, mem_norm, wkv, layer, tm=mem_rows.shape[0], tn=wkv.shape[2] // 2, out_dtype=BF16)
    kv = kv.reshape(batch, mem_len, wkv.shape[2])
    return xattn_layer(h, g0, wq[layer], kv, wo[layer], g1, seq_len=seq_len)


def _nsa_layer(h, g0, g1, w_in, w_out, layer, cmp_pos, cmp_w1, cmp_b1, cmp_w2, *, batch, seq_len):
    G, J, dh = NSA_KV_GROUPS, NSA_HPG, HEAD_DIM
    qw = G * J * dh
    kvw = NSA_BRANCHES * 2 * G * dh
    w_gate = w_in[layer, :, qw + kvw:].reshape(-1, NSA_BRANCHES, G, J).transpose(0, 2, 1, 3)
    w_gate = w_gate.reshape(-1, G * NSA_BRANCHES * J)
    q, kv_slabs, gates = nsa_in_proj(h, g0, w_in, layer, w_gate, q_width=qw, kv_width=kvw)
    kv_cmp = compress(kv_slabs, cmp_pos, cmp_w1.astype(BF16), cmp_b1, cmp_w2.astype(BF16),
                      batch=batch, seq_len=seq_len)
    n_heads = G * J
    slopes = 2.0 ** (-8.0 * jnp.arange(1, n_heads + 1, dtype=F32) / n_heads)
    o = nsa_attention(q, gates, kv_cmp, kv_slabs, slopes, batch=batch, seq_len=seq_len)
    a = matmul(o, w_out, layer, tm=1024, tn=512)
    return residual_norm(h, a, g1)


def kernel(x, mem, ln_mix, ln_xa, ln_ffn, mem_norm, pool_w, pool_scale, nsa_w_in, nsa_w_out, nsa_cmp_pos, nsa_cmp_w1, nsa_cmp_b1, nsa_cmp_w2, xa_wq, xa_wkv, xa_wo, ffn_w_gu, ffn_conv_w, ffn_conv_b, ffn_w_down):
    B, S, D = x.shape
    depth = ln_mix.shape[0]
    h = x.reshape(B * S, D)
    mem_rows = mem.reshape(B * mem.shape[1], D)
    nsa_w_in, nsa_w_out = nsa_w_in.astype(BF16), nsa_w_out.astype(BF16)
    xa_wq, xa_wkv, xa_wo = xa_wq.astype(BF16), xa_wkv.astype(BF16), xa_wo.astype(BF16)
    ffn_w_gu, ffn_w_down = ffn_w_gu.astype(BF16), ffn_w_down.astype(BF16)
    for i in range(depth):
        j = i // 2
        if i % 2 == 0:
            h = pool_layer(h, ln_mix[i, 0], pool_w[j].astype(BF16), pool_scale[j], ln_mix[i, 1], seq_len=S)
        else:
            h = _nsa_layer(h, ln_mix[i, 0], ln_mix[i, 1], nsa_w_in, nsa_w_out, j, nsa_cmp_pos[j],
                           nsa_cmp_w1[j], nsa_cmp_b1[j], nsa_cmp_w2[j], batch=B, seq_len=S)
        h = _xattn(h, mem_rows, mem_norm, ln_xa[i, 0], ln_xa[i, 1], xa_wq, xa_wkv, xa_wo, i, batch=B, seq_len=S)
        h = _ffn_layer(h, ln_ffn[i, 0], ln_ffn[i, 1], ffn_w_gu, ffn_w_down, i, ffn_conv_w[i], ffn_conv_b[i],
                       seq_len=S)
    return h.reshape(B, S, D)
```

```python
import functools

import jax
import jax.numpy as jnp
from jax import lax
from jax.experimental import pallas as pl
from jax.experimental.pallas import tpu as pltpu

F32 = jnp.float32
BF16 = jnp.bfloat16

RMS_EPS = 1e-6
NEG_BIG = -1e30
HEAD_DIM = 128
POOL_WINDOWS = (2, 4, 8, 16)
POOL_HALO = 16
NSA_KV_GROUPS = 4
NSA_HPG = 8
NSA_BRANCHES = 3
CMP_LEN = 32
CMP_STRIDE = 16
SEL_BLOCK = 64
N_SELECT = 16
WINDOW = 512
FORCED_SCORE = 1e6
XA_HEADS = 4
CONV_HALO = 8
VMEM_LIMIT = 56 * 1024 * 1024
NORM_CHUNK = 128
LANES = 128

_NT = (((1,), (1,)), ((), ()))


def _params(*sem):
    return pltpu.CompilerParams(dimension_semantics=sem, vmem_limit_bytes=VMEM_LIMIT)


def _rms(x, g):
    ms = jnp.mean(x * x, axis=-1, keepdims=True)
    return x * lax.rsqrt(ms + RMS_EPS) * g


def _norm_rows(h_ref, g_ref, a_scr):
    tm = h_ref.shape[0]
    chunk = min(NORM_CHUNK, tm)

    def body(c, carry):
        r = pl.multiple_of(c * chunk, chunk)
        a_scr[pl.ds(r, chunk), :] = _rms(h_ref[pl.ds(r, chunk), :], g_ref[...]).astype(BF16)
        return carry

    lax.fori_loop(0, tm // chunk, body, 0)


def _once(shape, index_map):
    return pl.BlockSpec(shape, index_map, pipeline_mode=pl.Buffered(1))


def _norm_mm_kernel(h_ref, g_ref, w_ref, o_ref, a_scr):
    @pl.when(pl.program_id(1) == 0)
    def _():
        _norm_rows(h_ref, g_ref, a_scr)

    o_ref[...] = jnp.dot(a_scr[...], w_ref[...], preferred_element_type=F32).astype(o_ref.dtype)


def norm_matmul(h, g, w, layer, *, tm, tn, out_dtype):
    M, D = h.shape
    N = w.shape[2]
    assert M % tm == 0 and N % tn == 0
    return pl.pallas_call(
        _norm_mm_kernel,
        grid=(M // tm, N // tn),
        in_specs=[
            _once((tm, D), lambda i, j: (i, 0)),
            pl.BlockSpec((1, D), lambda i, j: (0, 0)),
            pl.BlockSpec((None, D, tn), lambda i, j: (layer, 0, j)),
        ],
        out_specs=pl.BlockSpec((tm, tn), lambda i, j: (i, j)),
        out_shape=jax.ShapeDtypeStruct((M, N), out_dtype),
        scratch_shapes=[pltpu.VMEM((tm, D), BF16)],
        compiler_params=_params("parallel", "arbitrary"),
        name="norm_matmul",
    )(h, g.reshape(1, D), w)


def _nsa_proj_kernel(h_ref, g_ref, w_ref, wg_ref, q_ref, kv_ref, gate_ref, a_scr, *, nq, nkv, q_scale):
    j = pl.program_id(1)

    @pl.when(j == 0)
    def _():
        _norm_rows(h_ref, g_ref, a_scr)

    @pl.when(j < nq)
    def _():
        acc = jnp.dot(a_scr[...], w_ref[...], preferred_element_type=F32)
        q_ref[...] = (acc * q_scale).astype(q_ref.dtype)

    @pl.when((j >= nq) & (j < nq + nkv))
    def _():
        acc = jnp.dot(a_scr[...], w_ref[...], preferred_element_type=F32)
        for s in range(kv_ref.shape[0]):
            kv_ref[s] = acc[:, s * HEAD_DIM:(s + 1) * HEAD_DIM].astype(kv_ref.dtype)

    @pl.when(j == nq + nkv)
    def _():
        sig = jax.nn.sigmoid(jnp.dot(a_scr[...], wg_ref[...], preferred_element_type=F32))
        per_group = gate_ref.shape[2]
        for grp in range(gate_ref.shape[0]):
            gate_ref[grp] = sig[:, grp * per_group:(grp + 1) * per_group]


def nsa_in_proj(h, g, w_in, layer, w_gate, *, q_width, kv_width, tm=1024, tn=512):
    M, D = h.shape
    G = NSA_KV_GROUPS
    n_gate = w_gate.shape[1]
    assert M % tm == 0 and q_width % tn == 0 and kv_width % tn == 0
    nq, nkv = q_width // tn, kv_width // tn
    spt = tn // HEAD_DIM
    return pl.pallas_call(
        functools.partial(_nsa_proj_kernel, nq=nq, nkv=nkv, q_scale=HEAD_DIM ** -0.5),
        grid=(M // tm, nq + nkv + 1),
        in_specs=[
            _once((tm, D), lambda i, j: (i, 0)),
            pl.BlockSpec((1, D), lambda i, j: (0, 0)),
            pl.BlockSpec((None, D, tn), lambda i, j: (layer, 0, jnp.minimum(j, nq + nkv - 1))),
            pl.BlockSpec((D, n_gate), lambda i, j: (0, 0)),
        ],
        out_specs=[
            pl.BlockSpec((tm, tn), lambda i, j: (i, jnp.minimum(j, nq - 1))),
            pl.BlockSpec((spt, tm, HEAD_DIM), lambda i, j: (jnp.clip(j - nq, 0, nkv - 1), i, 0)),
            pl.BlockSpec((G, tm, n_gate // G), lambda i, j: (0, i, 0)),
        ],
        out_shape=[
            jax.ShapeDtypeStruct((M, q_width), BF16),
            jax.ShapeDtypeStruct((kv_width // HEAD_DIM, M, HEAD_DIM), BF16),
            jax.ShapeDtypeStruct((G, M, n_gate // G), F32),
        ],
        scratch_shapes=[pltpu.VMEM((tm, D), BF16)],
        compiler_params=_params("parallel", "arbitrary"),
        name="nsa_in_proj",
    )(h, g.reshape(1, D), w_in, w_gate)


def _mm_kernel(a_ref, w_ref, o_ref):
    o_ref[...] = jnp.dot(a_ref[...], w_ref[...], preferred_element_type=F32).astype(o_ref.dtype)


def matmul(a, w, layer, *, tm, tn, out_dtype=F32):
    M, K = a.shape
    N = w.shape[2]
    assert M % tm == 0 and N % tn == 0
    return pl.pallas_call(
        _mm_kernel,
        grid=(M // tm, N // tn),
        in_specs=[
            _once((tm, K), lambda i, j: (i, 0)),
            pl.BlockSpec((None, K, tn), lambda i, j: (layer, 0, j)),
        ],
        out_specs=pl.BlockSpec((tm, tn), lambda i, j: (i, j)),
        out_shape=jax.ShapeDtypeStruct((M, N), out_dtype),
        compiler_params=_params("parallel", "arbitrary"),
        name="matmul",
    )(a, w)


def _resid_kernel(h_ref, f_ref, g_ref, o_ref):
    o_ref[...] = h_ref[...] + _rms(f_ref[...], g_ref[...])


def residual_norm(h, f, g, *, tm=256):
    M, D = h.shape
    row = pl.BlockSpec((tm, D), lambda i: (i, 0))
    return pl.pallas_call(
        _resid_kernel,
        grid=(M // tm,),
        in_specs=[row, row, pl.BlockSpec((1, D), lambda i: (0, 0))],
        out_specs=row,
        out_shape=jax.ShapeDtypeStruct((M, D), F32),
        compiler_params=_params("parallel"),
        name="residual_norm",
    )(h, f, g.reshape(1, D))


def _pool_kernel(x_ref, halo_ref, g0_ref, w_ref, sc_ref, g1_ref, o_ref, *, seq_len):
    tm, D = x_ref.shape
    gc = D // len(POOL_WINDOWS)
    t0 = (pl.program_id(0) * tm) % seq_len
    x = x_ref[...]
    a = _rms(x, g0_ref[...])
    ha = _rms(halo_ref[...], g0_ref[...])
    ha = jnp.where(t0 == 0, 0.0, ha)
    full = jnp.concatenate([ha, a], axis=0)
    t1 = (t0 + 1 + lax.broadcasted_iota(jnp.int32, (tm, 1), 0)).astype(F32)
    ys = []
    for gi, win in enumerate(POOL_WINDOWS):
        s = full[:, gi * gc:(gi + 1) * gc]
        k = 1
        while k < win:
            s = s + pltpu.roll(s, k, 0)
            k *= 2
        inv_cnt = 1.0 / jnp.minimum(t1, float(win))
        pooled = s[POOL_HALO:] * inv_cnt
        d = (pooled - a[:, gi * gc:(gi + 1) * gc]).astype(BF16)
        ys.append(jnp.dot(d, w_ref[gi], preferred_element_type=F32))
    y = jnp.concatenate(ys, axis=-1) * sc_ref[...]
    o_ref[...] = x + _rms(y, g1_ref[...])


def pool_layer(x, g0, w, scale, g1, *, seq_len, tm=256):
    M, D = x.shape
    G, gc, _ = w.shape
    vec = pl.BlockSpec((1, D), lambda i: (0, 0))
    hb = tm // POOL_HALO
    return pl.pallas_call(
        functools.partial(_pool_kernel, seq_len=seq_len),
        grid=(M // tm,),
        in_specs=[
            pl.BlockSpec((tm, D), lambda i: (i, 0)),
            pl.BlockSpec((POOL_HALO, D), lambda i: (jnp.maximum(i * hb - 1, 0), 0)),
            vec,
            _once((G, gc, gc), lambda i: (0, 0, 0)),
            vec,
            vec,
        ],
        out_specs=pl.BlockSpec((tm, D), lambda i: (i, 0)),
        out_shape=jax.ShapeDtypeStruct((M, D), F32),
        compiler_params=_params("parallel"),
        name="pool_layer",
    )(x, x, g0.reshape(1, D), w, scale.reshape(1, D), g1.reshape(1, D))


def _xattn_kernel(h_ref, *refs, pending):
    h = h_ref[...]
    if pending:
        a_ref, ga_ref, *refs = refs
        h = h + _rms(a_ref[...], ga_ref[...])
    g0_ref, wq_ref, k_ref, v_ref, wo_ref, g1_ref, o_ref = refs
    n = _rms(h, g0_ref[...]).astype(BF16)
    q = jnp.dot(n, wq_ref[...], preferred_element_type=F32).astype(BF16)
    k = k_ref[0]
    v = v_ref[0]
    scale = HEAD_DIM ** -0.5
    outs = []
    for hh in range(XA_HEADS):
        sl = slice(hh * HEAD_DIM, (hh + 1) * HEAD_DIM)
        s = lax.dot_general(q[:, sl], k[:, sl], _NT, preferred_element_type=F32) * scale
        m = jnp.max(s, axis=-1, keepdims=True)
        e = jnp.exp(s - m)
        p = e * (1.0 / jnp.sum(e, axis=-1, keepdims=True))
        outs.append(jnp.dot(p.astype(BF16), v[:, sl], preferred_element_type=F32))
    o = jnp.concatenate(outs, axis=-1).astype(BF16)
    c = jnp.dot(o, wo_ref[...], preferred_element_type=F32)
    o_ref[...] = h + _rms(c, g1_ref[...])


def xattn_layer(h, g0, wq, kv, wo, g1, *, seq_len, pending=None, tm=256):
    M, D = h.shape
    xw = wq.shape[1]
    mem_len = kv.shape[1]
    per_seq = seq_len // tm
    vec = pl.BlockSpec((1, D), lambda i: (0, 0))
    row = pl.BlockSpec((tm, D), lambda i: (i, 0))
    pend_specs, pend_args = ([row, vec], [pending[0], pending[1].reshape(1, D)]) if pending is not None else ([], [])
    return pl.pallas_call(
        functools.partial(_xattn_kernel, pending=pending is not None),
        grid=(M // tm,),
        in_specs=[
            row,
            *pend_specs,
            vec,
            _once((D, xw), lambda i: (0, 0)),
            pl.BlockSpec((1, mem_len, xw), lambda i: (i // per_seq, 0, 0)),
            pl.BlockSpec((1, mem_len, xw), lambda i: (i // per_seq, 0, 1)),
            _once((xw, D), lambda i: (0, 0)),
            vec,
        ],
        out_specs=pl.BlockSpec((tm, D), lambda i: (i, 0)),
        out_shape=jax.ShapeDtypeStruct((M, D), F32),
        compiler_params=_params("parallel"),
        name="xattn_layer",
    )(h, *pend_args, g0.reshape(1, D), wq, kv, kv, wo, g1.reshape(1, D))


def _ffn_up_kernel(h_ref, halo_ref, g_ref, wg_ref, wu_ref, cw_ref, cb_ref, o_ref, a_scr, halo_scr, *, seq_len):
    tm = h_ref.shape[0]

    @pl.when(pl.program_id(1) == 0)
    def _():
        _norm_rows(h_ref, g_ref, a_scr)
        t0 = (pl.program_id(0) * tm) % seq_len
        hn = _rms(halo_ref[...], g_ref[...])
        halo_scr[...] = jnp.where(t0 == 0, 0.0, hn).astype(BF16)

    a = a_scr[...]
    wg = wg_ref[...].astype(BF16)
    gate = jnp.dot(a, wg, preferred_element_type=F32)
    up = jnp.dot(a, wu_ref[...].astype(BF16), preferred_element_type=F32)
    hg = jnp.dot(halo_scr[...], wg, preferred_element_type=F32)
    row = lax.broadcasted_iota(jnp.int32, (tm, 1), 0)
    prev1 = hg[CONV_HALO - 1:CONV_HALO]
    prev2 = hg[CONV_HALO - 2:CONV_HALO - 1]
    g1 = jnp.where(row == 0, prev1, pltpu.roll(gate, 1, 0))
    g2 = jnp.where(row == 0, prev2, jnp.where(row == 1, prev1, pltpu.roll(gate, 2, 0)))
    cw = cw_ref[...]
    gc = cw[0:1] * g2 + cw[1:2] * g1 + cw[2:3] * gate + cb_ref[...]
    o_ref[...] = (gc * jax.nn.sigmoid(gc) * up).astype(o_ref.dtype)


def ffn_up(h, g, w_gu, layer, conv_w, conv_b, *, seq_len, tm=1024, tn=256):
    M, D = h.shape
    F = w_gu.shape[2] // 2
    assert F % tn == 0 and M % tm == 0 and seq_len % tm == 0
    nj = F // tn
    hb = tm // CONV_HALO
    return pl.pallas_call(
        functools.partial(_ffn_up_kernel, seq_len=seq_len),
        grid=(M // tm, nj),
        in_specs=[
            _once((tm, D), lambda i, j: (i, 0)),
            _once((CONV_HALO, D), lambda i, j: (jnp.maximum(i * hb - 1, 0), 0)),
            pl.BlockSpec((1, D), lambda i, j: (0, 0)),
            pl.BlockSpec((None, D, tn), lambda i, j: (layer, 0, j)),
            pl.BlockSpec((None, D, tn), lambda i, j: (layer, 0, j + nj)),
            pl.BlockSpec((conv_w.shape[0], tn), lambda i, j: (0, j)),
            pl.BlockSpec((1, tn), lambda i, j: (0, j)),
        ],
        out_specs=pl.BlockSpec((tm, tn), lambda i, j: (i, j)),
        out_shape=jax.ShapeDtypeStruct((M, F), BF16),
        scratch_shapes=[pltpu.VMEM((tm, D), BF16), pltpu.VMEM((CONV_HALO, D), BF16)],
        compiler_params=_params("parallel", "arbitrary"),
        name="ffn_up",
    )(h, h, g.reshape(1, D), w_gu, w_gu, conv_w, conv_b.reshape(1, F))


def _gelu_tanh(x):
    return 0.5 * x * (1.0 + jnp.tanh(0.7978845608028654 * (x + 0.044715 * x * x * x)))


def _compress_kernel(r_ref, pos_ref, w1_ref, b1_ref, w2_ref, o_ref):
    r = r_ref[0, 0].astype(F32)
    nch, half = r.shape
    pos = pos_ref[0]
    xa = (r + pos[0:1]).astype(BF16)
    xb = (r + pos[1:2]).astype(BF16)
    h1 = jnp.dot(xa, w1_ref[0, :half, :], preferred_element_type=F32)
    h2 = jnp.dot(xb, w1_ref[0, half:, :], preferred_element_type=F32)
    hid = h1 + pltpu.roll(h2, nch - 1, 0) + b1_ref[0]
    out = jnp.dot(_gelu_tanh(hid).astype(BF16), w2_ref[0], preferred_element_type=F32)
    row = lax.broadcasted_iota(jnp.int32, (nch, 1), 0)
    o_ref[0, 0, 0] = jnp.where(row < nch - 1, out, 0.0).astype(o_ref.dtype)


def compress(kv_slabs, pos, w1, b1, w2, *, batch, seq_len):
    G = NSA_KV_GROUPS
    nch = seq_len // CMP_STRIDE
    half = CMP_STRIDE * HEAD_DIM
    r = kv_slabs[:2 * G].reshape(2 * G, batch, nch, half)
    hidden = w1.shape[-1]
    return pl.pallas_call(
        _compress_kernel,
        grid=(2, batch, G),
        in_specs=[
            pl.BlockSpec((1, 1, nch, half), lambda kv, b, g: (kv * G + g, b, 0, 0)),
            pl.BlockSpec((1, 2, half), lambda kv, b, g: (kv, 0, 0)),
            pl.BlockSpec((1, 2 * half, hidden), lambda kv, b, g: (kv, 0, 0)),
            pl.BlockSpec((1, 1, hidden), lambda kv, b, g: (kv, 0, 0)),
            pl.BlockSpec((1, hidden, HEAD_DIM), lambda kv, b, g: (kv, 0, 0)),
        ],
        out_specs=pl.BlockSpec((1, 1, 1, nch, HEAD_DIM), lambda kv, b, g: (kv, b, g, 0, 0)),
        out_shape=jax.ShapeDtypeStruct((2, batch, G, nch, HEAD_DIM), BF16),
        compiler_params=_params("parallel", "parallel", "parallel"),
        name="nsa_compress",
    )(r, pos.reshape(2, 2, half), w1, b1.reshape(2, 1, hidden), w2)


def _dot_row_halves(p_ref, width, rhs):
    half = p_ref.shape[0] // 2
    return jnp.concatenate([jnp.dot(p_ref[:half, :width], rhs, preferred_element_type=F32),
                            jnp.dot(p_ref[half:, :width], rhs, preferred_element_type=F32)], axis=0)


def _nsa_kernel(slope_ref, q_ref, gate_ref, kc_ref, vc_ref, ks_ref, vs_ref, kw_ref, vw_ref, o_ref,
                qa_scr, ka_scr, va_scr, p_scr, oc_scr, mx_scr, acc_scr, used_ref, *, kt):
    J, dh = NSA_HPG, HEAD_DIM
    qb = q_ref.shape[0]
    seq_len = ks_ref.shape[2]
    ncp = kc_ref.shape[3]
    ns = seq_len // SEL_BLOCK
    g = pl.program_id(1)
    qi = pl.program_id(2)
    start = qi * qb

    @pl.when(qi == 0)
    def _():
        chunk = min(512, seq_len)

        def body(c, carry):
            r = pl.multiple_of(c * chunk, chunk)
            ka_scr[pl.ds(r, chunk), :dh] = ks_ref[0, 0, pl.ds(r, chunk), :]
            blk = (r + lax.broadcasted_iota(jnp.int32, (chunk, ns), 0)) // SEL_BLOCK
            ka_scr[pl.ds(r, chunk), dh:] = (blk == lax.broadcasted_iota(jnp.int32, (chunk, ns), 1)).astype(BF16)
            va_scr[pl.ds(r, chunk), :dh] = vs_ref[0, 0, pl.ds(r, chunk), :]
            va_scr[pl.ds(r, chunk), dh:] = jnp.ones((chunk, dh), BF16)
            return carry

        lax.fori_loop(0, seq_len // chunk, body, 0)

    q = q_ref[...]
    for j in range(J):
        qa_scr[j * qb:(j + 1) * qb, :dh] = q[:, j * dh:(j + 1) * dh]
    qr = qa_scr[:, :dh]
    slopes = [slope_ref[g * J + j] for j in range(J)]
    t_col = start + lax.broadcasted_iota(jnp.int32, (qb, 1), 0)

    lanes = mx_scr.shape[1]
    kc = kc_ref[0, 0, 0]
    sc = lax.dot_general(qr, kc, _NT, preferred_element_type=F32)
    c_end = lax.broadcasted_iota(jnp.int32, (1, ncp), 1) * CMP_STRIDE + (CMP_LEN - 1)
    bias_c = jnp.where(t_col >= c_end, 0.0, NEG_BIG)
    rel_c = (c_end - start).astype(F32)
    for j in range(J):
        rows = slice(j * qb, (j + 1) * qb)
        sj = sc[rows] + bias_c + slopes[j] * rel_c
        p_scr[rows, :ncp] = jnp.exp(sj - jnp.max(sj, axis=-1, keepdims=True)).astype(BF16)
    c_row = lax.broadcasted_iota(jnp.int32, (ncp, ns), 0) * CMP_STRIDE
    n_col = lax.broadcasted_iota(jnp.int32, (ncp, ns), 1) * SEL_BLOCK
    overlap = ((c_row < n_col + SEL_BLOCK) & (c_row + CMP_LEN > n_col)
               & (c_row < (ncp - 1) * CMP_STRIDE)).astype(BF16)
    rhs_c = jnp.concatenate([vc_ref[0, 0, 0], jnp.ones((ncp, lanes), BF16), overlap], axis=1)
    oc = _dot_row_halves(p_scr, ncp, rhs_c)
    has_key = t_col >= CMP_LEN - 1
    score = jnp.zeros((qb, ns), F32)
    for j in range(J):
        rows = slice(j * qb, (j + 1) * qb)
        inv = jnp.where(has_key, 1.0 / jnp.maximum(oc[rows, dh:dh + lanes], 1e-30), 0.0)
        oc_scr[rows] = oc[rows, :dh] * inv
        score = score + oc[rows, dh + lanes:] * inv[:, :ns]

    wl = WINDOW + qb
    w0 = pl.multiple_of(jnp.maximum(start - WINDOW, 0), qb)
    kw = kw_ref[0, 0, pl.ds(w0, wl), :]
    vw = jnp.concatenate([vw_ref[0, 0, pl.ds(w0, wl), :], jnp.ones((wl, dh), BF16)], axis=1)
    sw = lax.dot_general(qr, kw, _NT, preferred_element_type=F32)
    pos_w = w0 + lax.broadcasted_iota(jnp.int32, (1, wl), 1)
    bias_w = jnp.where((pos_w <= t_col) & (pos_w > t_col - WINDOW), 0.0, NEG_BIG)
    rel_w = (pos_w - start).astype(F32)
    for j in range(J):
        rows = slice(j * qb, (j + 1) * qb)
        sj = sw[rows] + bias_w + slopes[j] * rel_w
        p_scr[rows, :wl] = jnp.exp(sj - jnp.max(sj, axis=-1, keepdims=True)).astype(BF16)
    ow = _dot_row_halves(p_scr, wl, vw)
    o_win = ow[:, :dh] * (1.0 / jnp.maximum(ow[:, dh:], 1e-30))

    score_t = score.T
    n_idx = lax.broadcasted_iota(jnp.int32, (ns, qb), 0)
    cur = (start + lax.broadcasted_iota(jnp.int32, (1, qb), 1)) // SEL_BLOCK
    forced = (n_idx == 0) | (n_idx == cur) | (n_idx == cur - 1)
    score_t = jnp.where(n_idx > cur, -1.0, jnp.where(forced, FORCED_SCORE, score_t))
    sel_t = jnp.zeros((ns, qb), F32)
    for _ in range(min(N_SELECT, ns)):
        best = jnp.max(score_t, axis=0, keepdims=True)
        first = jnp.min(jnp.where(score_t == best, n_idx, ns), axis=0, keepdims=True)
        hit = n_idx == first
        sel_t = jnp.where(hit, 1.0, sel_t)
        score_t = jnp.where(hit, -2.0, score_t)
    block_bias = jnp.where(sel_t.T > 0.5, 0.0, NEG_BIG).astype(BF16)
    for j in range(J):
        qa_scr[j * qb:(j + 1) * qb, dh:] = block_bias
    bpt = kt // SEL_BLOCK
    for i in range(ns // bpt):
        used_ref[i] = (jnp.max(sel_t[i * bpt:(i + 1) * bpt, :]) > 0.5).astype(jnp.int32)

    pos_d = (start // kt) * kt + lax.broadcasted_iota(jnp.int32, (1, kt), 1)
    causal_bias = jnp.where(pos_d <= t_col, 0.0, NEG_BIG)

    def scores(i, j, s, diagonal):
        pos = i * kt + lax.broadcasted_iota(jnp.int32, (1, kt), 1)
        sj = s[j * qb:(j + 1) * qb] + slopes[j] * (pos - start).astype(F32)
        if diagonal:
            sj = sj + causal_bias
        return sj

    def max_tile(i, diagonal):
        k0 = pl.multiple_of(i * kt, kt)
        s = lax.dot_general(qa_scr[...], ka_scr[pl.ds(k0, kt), :], _NT, preferred_element_type=F32)
        for j in range(J):
            rows = slice(j * qb, (j + 1) * qb)
            sj = scores(i, j, s, diagonal)
            mx = mx_scr[rows]
            for c in range(kt // lanes):
                mx = jnp.maximum(mx, sj[:, c * lanes:(c + 1) * lanes])
            mx_scr[rows] = mx

    def pv_tile(i, diagonal):
        k0 = pl.multiple_of(i * kt, kt)
        s = lax.dot_general(qa_scr[...], ka_scr[pl.ds(k0, kt), :], _NT, preferred_element_type=F32)
        for j in range(J):
            rows = slice(j * qb, (j + 1) * qb)
            sj = scores(i, j, s, diagonal)
            m = mx_scr[rows]
            for c in range(kt // lanes):
                p_scr[rows, c * lanes:(c + 1) * lanes] = jnp.exp(sj[:, c * lanes:(c + 1) * lanes] - m).astype(BF16)
        acc_scr[...] += _dot_row_halves(p_scr, kt, va_scr[pl.ds(k0, kt), :])

    def run_pass(tile_fn):
        def full(i, carry):
            @pl.when(used_ref[i] > 0)
            def _():
                tile_fn(i, False)

            return carry

        n_full = start // kt
        lax.fori_loop(0, n_full, full, 0)
        tile_fn(n_full, True)

    mx_scr[...] = jnp.full(mx_scr.shape, NEG_BIG, F32)
    run_pass(max_tile)
    for j in range(J):
        rows = slice(j * qb, (j + 1) * qb)
        mx_scr[rows] = jnp.broadcast_to(jnp.max(mx_scr[rows], axis=-1, keepdims=True), (qb, lanes))
    acc_scr[...] = jnp.zeros(acc_scr.shape, F32)
    run_pass(pv_tile)

    gate = gate_ref[0]
    for j in range(J):
        rows = slice(j * qb, (j + 1) * qb)
        o_sel = acc_scr[rows, :dh] * (1.0 / jnp.maximum(acc_scr[rows, dh:], 1e-30))
        o = (gate[:, j:j + 1] * oc_scr[rows] + gate[:, J + j:J + j + 1] * o_sel
             + gate[:, 2 * J + j:2 * J + j + 1] * o_win[rows])
        o_ref[:, j * dh:(j + 1) * dh] = o.astype(o_ref.dtype)


def nsa_attention(q, gates, kv_cmp, kv_slabs, slopes, *, batch, seq_len, qb=128, kt=512):
    G, J, dh = NSA_KV_GROUPS, NSA_HPG, HEAD_DIM
    M = q.shape[0]
    nq = seq_len // qb
    ncp = kv_cmp.shape[3]
    ns = seq_len // SEL_BLOCK
    assert seq_len % kt == 0 and kt % qb == 0 and seq_len >= WINDOW + qb
    slabs = kv_slabs.reshape(kv_slabs.shape[0], batch, seq_len, dh)

    def slab(branch, is_v):
        base = (branch * 2 + is_v) * G
        return pl.BlockSpec((1, 1, seq_len, dh), lambda b, g, i, sl: (base + g, b, 0, 0))

    def cmp_spec(is_v):
        return pl.BlockSpec((1, 1, 1, ncp, dh), lambda b, g, i, sl: (is_v, b, g, 0, 0))

    grid_spec = pltpu.PrefetchScalarGridSpec(
        num_scalar_prefetch=1,
        grid=(batch, G, nq),
        in_specs=[
            pl.BlockSpec((qb, J * dh), lambda b, g, i, sl: (b * nq + i, g)),
            pl.BlockSpec((1, qb, NSA_BRANCHES * J), lambda b, g, i, sl: (g, b * nq + i, 0)),
            cmp_spec(0), cmp_spec(1),
            slab(1, 0), slab(1, 1), slab(2, 0), slab(2, 1),
        ],
        out_specs=pl.BlockSpec((qb, J * dh), lambda b, g, i, sl: (b * nq + i, g)),
        scratch_shapes=[
            pltpu.VMEM((J * qb, dh + ns), BF16),
            pltpu.VMEM((seq_len, dh + ns), BF16),
            pltpu.VMEM((seq_len, 2 * dh), BF16),
            pltpu.VMEM((J * qb, max(ncp, WINDOW + qb, kt)), BF16),
            pltpu.VMEM((J * qb, dh), F32),
            pltpu.VMEM((J * qb, LANES), F32),
            pltpu.VMEM((J * qb, 2 * dh), F32),
            pltpu.SMEM((seq_len // kt,), jnp.int32),
        ],
    )
    return pl.pallas_call(
        functools.partial(_nsa_kernel, kt=kt),
        grid_spec=grid_spec,
        out_shape=jax.ShapeDtypeStruct((M, G * J * dh), BF16),
        compiler_params=_params("parallel", "parallel", "arbitrary"),
        name="nsa_attention",
    )(slopes, q, gates, kv_cmp, kv_cmp, slabs, slabs, slabs, slabs)


def _ffn_layer(h, g0, g1, w_gu, w_down, layer, conv_w, conv_b, *, seq_len):
    act = ffn_up(h, g0, w_gu, layer, conv_w, conv_b, seq_len=seq_len)
    f = matmul(act, w_down, layer, tm=512, tn=512)
    return residual_norm(h, f, g1)


def _xattn(h, mem_rows, mem_norm, g0, g1, wq, wkv, wo, layer, *, batch, seq_len, pending=None):
    mem_len = mem_rows.shape[0] // batch
    kv = norm_matmul(mem_rows, mem_norm, wkv, layer, tm=mem_rows.shape[0], tn=wkv.shape[2] // 2, out_dtype=BF16)
    kv = kv.reshape(batch, mem_len, wkv.shape[2])
    return xattn_layer(h, g0, wq[layer], kv, wo[layer], g1, seq_len=seq_len, pending=pending)


def _nsa_mixer(h, g0, w_in, w_out, layer, cmp_pos, cmp_w1, cmp_b1, cmp_w2, *, batch, seq_len):
    G, J, dh = NSA_KV_GROUPS, NSA_HPG, HEAD_DIM
    qw = G * J * dh
    kvw = NSA_BRANCHES * 2 * G * dh
    w_gate = w_in[layer, :, qw + kvw:].reshape(-1, NSA_BRANCHES, G, J).transpose(0, 2, 1, 3)
    w_gate = w_gate.reshape(-1, G * NSA_BRANCHES * J)
    q, kv_slabs, gates = nsa_in_proj(h, g0, w_in, layer, w_gate, q_width=qw, kv_width=kvw)
    kv_cmp = compress(kv_slabs, cmp_pos, cmp_w1.astype(BF16), cmp_b1, cmp_w2.astype(BF16),
                      batch=batch, seq_len=seq_len)
    n_heads = G * J
    slopes = 2.0 ** (-8.0 * jnp.arange(1, n_heads + 1, dtype=F32) / n_heads)
    o = nsa_attention(q, gates, kv_cmp, kv_slabs, slopes, batch=batch, seq_len=seq_len)
    return matmul(o, w_out, layer, tm=1024, tn=512)


def kernel(x, mem, ln_mix, ln_xa, ln_ffn, mem_norm, pool_w, pool_scale, nsa_w_in, nsa_w_out, nsa_cmp_pos, nsa_cmp_w1, nsa_cmp_b1, nsa_cmp_w2, xa_wq, xa_wkv, xa_wo, ffn_w_gu, ffn_conv_w, ffn_conv_b, ffn_w_down):
    B, S, D = x.shape
    depth = ln_mix.shape[0]
    h = x.reshape(B * S, D)
    mem_rows = mem.reshape(B * mem.shape[1], D)
    nsa_w_in, nsa_w_out = nsa_w_in.astype(BF16), nsa_w_out.astype(BF16)
    xa_wq, xa_wkv, xa_wo = xa_wq.astype(BF16), xa_wkv.astype(BF16), xa_wo.astype(BF16)
    ffn_w_down = ffn_w_down.astype(BF16)
    for i in range(depth):
        j = i // 2
        pending = None
        if i % 2 == 0:
            h = pool_layer(h, ln_mix[i, 0], pool_w[j].astype(BF16), pool_scale[j], ln_mix[i, 1], seq_len=S)
        else:
            a = _nsa_mixer(h, ln_mix[i, 0], nsa_w_in, nsa_w_out, j, nsa_cmp_pos[j],
                           nsa_cmp_w1[j], nsa_cmp_b1[j], nsa_cmp_w2[j], batch=B, seq_len=S)
            pending = (a, ln_mix[i, 1])
        h = _xattn(h, mem_rows, mem_norm, ln_xa[i, 0], ln_xa[i, 1], xa_wq, xa_wkv, xa_wo, i, batch=B, seq_len=S,
                   pending=pending)
        h = _ffn_layer(h, ln_ffn[i, 0], ln_ffn[i, 1], ffn_w_gu, ffn_w_down, i, ffn_conv_w[i], ffn_conv_b[i],
                       seq_len=S)
    return h.reshape(B, S, D)
```

```python
import functools

import jax
import jax.numpy as jnp
from jax import lax
from jax.experimental import pallas as pl
from jax.experimental.pallas import tpu as pltpu

F32 = jnp.float32
BF16 = jnp.bfloat16

RMS_EPS = 1e-6
NEG_BIG = -1e30
HEAD_DIM = 128
POOL_WINDOWS = (2, 4, 8, 16)
POOL_HALO = 16
NSA_KV_GROUPS = 4
NSA_HPG = 8
NSA_BRANCHES = 3
CMP_LEN = 32
CMP_STRIDE = 16
SEL_BLOCK = 64
N_SELECT = 16
WINDOW = 512
FORCED_SCORE = 1e6
XA_HEADS = 4
CONV_HALO = 8
VMEM_LIMIT = 56 * 1024 * 1024
NORM_CHUNK = 128
LANES = 128

_NT = (((1,), (1,)), ((), ()))


def _params(*sem):
    return pltpu.CompilerParams(dimension_semantics=sem, vmem_limit_bytes=VMEM_LIMIT)


def _rms(x, g):
    ms = jnp.mean(x * x, axis=-1, keepdims=True)
    return x * lax.rsqrt(ms + RMS_EPS) * g


def _norm_rows(h_ref, g_ref, a_scr):
    tm = h_ref.shape[0]
    chunk = min(NORM_CHUNK, tm)

    def body(c, carry):
        r = pl.multiple_of(c * chunk, chunk)
        a_scr[pl.ds(r, chunk), :] = _rms(h_ref[pl.ds(r, chunk), :], g_ref[...]).astype(BF16)
        return carry

    lax.fori_loop(0, tm // chunk, body, 0)


def _once(shape, index_map):
    return pl.BlockSpec(shape, index_map, pipeline_mode=pl.Buffered(1))


def _norm_mm_kernel(h_ref, g_ref, w_ref, o_ref, a_scr):
    @pl.when(pl.program_id(1) == 0)
    def _():
        _norm_rows(h_ref, g_ref, a_scr)

    o_ref[...] = jnp.dot(a_scr[...], w_ref[...], preferred_element_type=F32).astype(o_ref.dtype)


def norm_matmul(h, g, w, layer, *, tm, tn, out_dtype):
    M, D = h.shape
    N = w.shape[2]
    assert M % tm == 0 and N % tn == 0
    return pl.pallas_call(
        _norm_mm_kernel,
        grid=(M // tm, N // tn),
        in_specs=[
            _once((tm, D), lambda i, j: (i, 0)),
            pl.BlockSpec((1, D), lambda i, j: (0, 0)),
            pl.BlockSpec((None, D, tn), lambda i, j: (layer, 0, j)),
        ],
        out_specs=pl.BlockSpec((tm, tn), lambda i, j: (i, j)),
        out_shape=jax.ShapeDtypeStruct((M, N), out_dtype),
        scratch_shapes=[pltpu.VMEM((tm, D), BF16)],
        compiler_params=_params("parallel", "arbitrary"),
        name="norm_matmul",
    )(h, g.reshape(1, D), w)


def _nsa_proj_kernel(h_ref, g_ref, w_ref, wg_ref, q_ref, kv_ref, gate_ref, a_scr, *, nq, nkv, q_scale):
    j = pl.program_id(1)

    @pl.when(j == 0)
    def _():
        _norm_rows(h_ref, g_ref, a_scr)

    @pl.when(j < nq)
    def _():
        acc = jnp.dot(a_scr[...], w_ref[...], preferred_element_type=F32)
        q_ref[...] = (acc * q_scale).astype(q_ref.dtype)

    @pl.when((j >= nq) & (j < nq + nkv))
    def _():
        acc = jnp.dot(a_scr[...], w_ref[...], preferred_element_type=F32)
        for s in range(kv_ref.shape[0]):
            kv_ref[s] = acc[:, s * HEAD_DIM:(s + 1) * HEAD_DIM].astype(kv_ref.dtype)

    @pl.when(j == nq + nkv)
    def _():
        sig = jax.nn.sigmoid(jnp.dot(a_scr[...], wg_ref[...], preferred_element_type=F32))
        per_group = gate_ref.shape[2]
        for grp in range(gate_ref.shape[0]):
            gate_ref[grp] = sig[:, grp * per_group:(grp + 1) * per_group]


def nsa_in_proj(h, g, w_in, layer, w_gate, *, q_width, kv_width, tm=1024, tn=512):
    M, D = h.shape
    G = NSA_KV_GROUPS
    n_gate = w_gate.shape[1]
    assert M % tm == 0 and q_width % tn == 0 and kv_width % tn == 0
    nq, nkv = q_width // tn, kv_width // tn
    spt = tn // HEAD_DIM
    return pl.pallas_call(
        functools.partial(_nsa_proj_kernel, nq=nq, nkv=nkv, q_scale=HEAD_DIM ** -0.5),
        grid=(M // tm, nq + nkv + 1),
        in_specs=[
            _once((tm, D), lambda i, j: (i, 0)),
            pl.BlockSpec((1, D), lambda i, j: (0, 0)),
            pl.BlockSpec((None, D, tn), lambda i, j: (layer, 0, jnp.minimum(j, nq + nkv - 1))),
            pl.BlockSpec((D, n_gate), lambda i, j: (0, 0)),
        ],
        out_specs=[
            pl.BlockSpec((tm, tn), lambda i, j: (i, jnp.minimum(j, nq - 1))),
            pl.BlockSpec((spt, tm, HEAD_DIM), lambda i, j: (jnp.clip(j - nq, 0, nkv - 1), i, 0)),
            pl.BlockSpec((G, tm, n_gate // G), lambda i, j: (0, i, 0)),
        ],
        out_shape=[
            jax.ShapeDtypeStruct((M, q_width), BF16),
            jax.ShapeDtypeStruct((kv_width // HEAD_DIM, M, HEAD_DIM), BF16),
            jax.ShapeDtypeStruct((G, M, n_gate // G), F32),
        ],
        scratch_shapes=[pltpu.VMEM((tm, D), BF16)],
        compiler_params=_params("parallel", "arbitrary"),
        name="nsa_in_proj",
    )(h, g.reshape(1, D), w_in, w_gate)


def _mm_kernel(a_ref, w_ref, o_ref):
    o_ref[...] = jnp.dot(a_ref[...], w_ref[...], preferred_element_type=F32).astype(o_ref.dtype)


def matmul(a, w, layer, *, tm, tn, out_dtype=F32):
    M, K = a.shape
    N = w.shape[2]
    assert M % tm == 0 and N % tn == 0
    return pl.pallas_call(
        _mm_kernel,
        grid=(M // tm, N // tn),
        in_specs=[
            _once((tm, K), lambda i, j: (i, 0)),
            pl.BlockSpec((None, K, tn), lambda i, j: (layer, 0, j)),
        ],
        out_specs=pl.BlockSpec((tm, tn), lambda i, j: (i, j)),
        out_shape=jax.ShapeDtypeStruct((M, N), out_dtype),
        compiler_params=_params("parallel", "arbitrary"),
        name="matmul",
    )(a, w)


def _resid_kernel(h_ref, f_ref, g_ref, o_ref):
    o_ref[...] = h_ref[...] + _rms(f_ref[...], g_ref[...])


def residual_norm(h, f, g, *, tm=256):
    M, D = h.shape
    row = pl.BlockSpec((tm, D), lambda i: (i, 0))
    return pl.pallas_call(
        _resid_kernel,
        grid=(M // tm,),
        in_specs=[row, row, pl.BlockSpec((1, D), lambda i: (0, 0))],
        out_specs=row,
        out_shape=jax.ShapeDtypeStruct((M, D), F32),
        compiler_params=_params("parallel"),
        name="residual_norm",
    )(h, f, g.reshape(1, D))


def _pool_kernel(x_ref, halo_ref, g0_ref, w_ref, sc_ref, g1_ref, o_ref, *, seq_len):
    tm, D = x_ref.shape
    gc = D // len(POOL_WINDOWS)
    t0 = (pl.program_id(0) * tm) % seq_len
    x = x_ref[...]
    a = _rms(x, g0_ref[...])
    ha = _rms(halo_ref[...], g0_ref[...])
    ha = jnp.where(t0 == 0, 0.0, ha)
    full = jnp.concatenate([ha, a], axis=0)
    t1 = (t0 + 1 + lax.broadcasted_iota(jnp.int32, (tm, 1), 0)).astype(F32)
    ys = []
    for gi, win in enumerate(POOL_WINDOWS):
        s = full[:, gi * gc:(gi + 1) * gc]
        k = 1
        while k < win:
            s = s + pltpu.roll(s, k, 0)
            k *= 2
        inv_cnt = 1.0 / jnp.minimum(t1, float(win))
        pooled = s[POOL_HALO:] * inv_cnt
        d = (pooled - a[:, gi * gc:(gi + 1) * gc]).astype(BF16)
        ys.append(jnp.dot(d, w_ref[gi], preferred_element_type=F32))
    y = jnp.concatenate(ys, axis=-1) * sc_ref[...]
    o_ref[...] = x + _rms(y, g1_ref[...])


def pool_layer(x, g0, w, scale, g1, *, seq_len, tm=256):
    M, D = x.shape
    G, gc, _ = w.shape
    vec = pl.BlockSpec((1, D), lambda i: (0, 0))
    hb = tm // POOL_HALO
    return pl.pallas_call(
        functools.partial(_pool_kernel, seq_len=seq_len),
        grid=(M // tm,),
        in_specs=[
            pl.BlockSpec((tm, D), lambda i: (i, 0)),
            pl.BlockSpec((POOL_HALO, D), lambda i: (jnp.maximum(i * hb - 1, 0), 0)),
            vec,
            _once((G, gc, gc), lambda i: (0, 0, 0)),
            vec,
            vec,
        ],
        out_specs=pl.BlockSpec((tm, D), lambda i: (i, 0)),
        out_shape=jax.ShapeDtypeStruct((M, D), F32),
        compiler_params=_params("parallel"),
        name="pool_layer",
    )(x, x, g0.reshape(1, D), w, scale.reshape(1, D), g1.reshape(1, D))


def _xattn_kernel(h_ref, *refs, pending):
    h = h_ref[...]
    if pending:
        a_ref, ga_ref, *refs = refs
        h = h + _rms(a_ref[...], ga_ref[...])
    g0_ref, wq_ref, k_ref, v_ref, wo_ref, g1_ref, o_ref = refs
    n = _rms(h, g0_ref[...]).astype(BF16)
    q = jnp.dot(n, wq_ref[...], preferred_element_type=F32).astype(BF16)
    k = k_ref[0]
    v = v_ref[0]
    scale = HEAD_DIM ** -0.5
    outs = []
    for hh in range(XA_HEADS):
        sl = slice(hh * HEAD_DIM, (hh + 1) * HEAD_DIM)
        s = lax.dot_general(q[:, sl], k[:, sl], _NT, preferred_element_type=F32) * scale
        m = jnp.max(s, axis=-1, keepdims=True)
        e = jnp.exp(s - m)
        p = e * (1.0 / jnp.sum(e, axis=-1, keepdims=True))
        outs.append(jnp.dot(p.astype(BF16), v[:, sl], preferred_element_type=F32))
    o = jnp.concatenate(outs, axis=-1).astype(BF16)
    c = jnp.dot(o, wo_ref[...], preferred_element_type=F32)
    o_ref[...] = h + _rms(c, g1_ref[...])


def xattn_layer(h, g0, wq, kv, wo, g1, *, seq_len, pending=None, tm=256):
    M, D = h.shape
    xw = wq.shape[1]
    mem_len = kv.shape[1]
    per_seq = seq_len // tm
    vec = pl.BlockSpec((1, D), lambda i: (0, 0))
    row = pl.BlockSpec((tm, D), lambda i: (i, 0))
    pend_specs, pend_args = ([row, vec], [pending[0], pending[1].reshape(1, D)]) if pending is not None else ([], [])
    return pl.pallas_call(
        functools.partial(_xattn_kernel, pending=pending is not None),
        grid=(M // tm,),
        in_specs=[
            row,
            *pend_specs,
            vec,
            _once((D, xw), lambda i: (0, 0)),
            pl.BlockSpec((1, mem_len, xw), lambda i: (i // per_seq, 0, 0)),
            pl.BlockSpec((1, mem_len, xw), lambda i: (i // per_seq, 0, 1)),
            _once((xw, D), lambda i: (0, 0)),
            vec,
        ],
        out_specs=pl.BlockSpec((tm, D), lambda i: (i, 0)),
        out_shape=jax.ShapeDtypeStruct((M, D), F32),
        compiler_params=_params("parallel"),
        name="xattn_layer",
    )(h, *pend_args, g0.reshape(1, D), wq, kv, kv, wo, g1.reshape(1, D))


def _ffn_up_kernel(h_ref, halo_ref, g_ref, wg_ref, wu_ref, cw_ref, cb_ref, o_ref, a_scr, halo_scr, *, seq_len):
    tm = h_ref.shape[0]

    @pl.when(pl.program_id(1) == 0)
    def _():
        _norm_rows(h_ref, g_ref, a_scr)
        t0 = (pl.program_id(0) * tm) % seq_len
        hn = _rms(halo_ref[...], g_ref[...])
        halo_scr[...] = jnp.where(t0 == 0, 0.0, hn).astype(BF16)

    a = a_scr[...]
    wg = wg_ref[...].astype(BF16)
    gate = jnp.dot(a, wg, preferred_element_type=F32)
    up = jnp.dot(a, wu_ref[...].astype(BF16), preferred_element_type=F32)
    hg = jnp.dot(halo_scr[...], wg, preferred_element_type=F32)
    row = lax.broadcasted_iota(jnp.int32, (tm, 1), 0)
    prev1 = hg[CONV_HALO - 1:CONV_HALO]
    prev2 = hg[CONV_HALO - 2:CONV_HALO - 1]
    g1 = jnp.where(row == 0, prev1, pltpu.roll(gate, 1, 0))
    g2 = jnp.where(row == 0, prev2, jnp.where(row == 1, prev1, pltpu.roll(gate, 2, 0)))
    cw = cw_ref[...]
    gc = cw[0:1] * g2 + cw[1:2] * g1 + cw[2:3] * gate + cb_ref[...]
    o_ref[...] = (gc * jax.nn.sigmoid(gc) * up).astype(o_ref.dtype)


def ffn_up(h, g, w_gu, layer, conv_w, conv_b, *, seq_len, tm=1024, tn=256):
    M, D = h.shape
    F = w_gu.shape[2] // 2
    assert F % tn == 0 and M % tm == 0 and seq_len % tm == 0
    nj = F // tn
    hb = tm // CONV_HALO
    return pl.pallas_call(
        functools.partial(_ffn_up_kernel, seq_len=seq_len),
        grid=(M // tm, nj),
        in_specs=[
            _once((tm, D), lambda i, j: (i, 0)),
            _once((CONV_HALO, D), lambda i, j: (jnp.maximum(i * hb - 1, 0), 0)),
            pl.BlockSpec((1, D), lambda i, j: (0, 0)),
            pl.BlockSpec((None, D, tn), lambda i, j: (layer, 0, j)),
            pl.BlockSpec((None, D, tn), lambda i, j: (layer, 0, j + nj)),
            pl.BlockSpec((conv_w.shape[0], tn), lambda i, j: (0, j)),
            pl.BlockSpec((1, tn), lambda i, j: (0, j)),
        ],
        out_specs=pl.BlockSpec((tm, tn), lambda i, j: (i, j)),
        out_shape=jax.ShapeDtypeStruct((M, F), BF16),
        scratch_shapes=[pltpu.VMEM((tm, D), BF16), pltpu.VMEM((CONV_HALO, D), BF16)],
        compiler_params=_params("parallel", "arbitrary"),
        name="ffn_up",
    )(h, h, g.reshape(1, D), w_gu, w_gu, conv_w, conv_b.reshape(1, F))


def _gelu_tanh(x):
    return 0.5 * x * (1.0 + jnp.tanh(0.7978845608028654 * (x + 0.044715 * x * x * x)))


def _compress_kernel(r_ref, pos_ref, w1_ref, b1_ref, w2_ref, o_ref):
    r = r_ref[0, 0].astype(F32)
    nch, half = r.shape
    pos = pos_ref[0]
    xa = (r + pos[0:1]).astype(BF16)
    xb = (r + pos[1:2]).astype(BF16)
    h1 = jnp.dot(xa, w1_ref[0, :half, :], preferred_element_type=F32)
    h2 = jnp.dot(xb, w1_ref[0, half:, :], preferred_element_type=F32)
    hid = h1 + pltpu.roll(h2, nch - 1, 0) + b1_ref[0]
    out = jnp.dot(_gelu_tanh(hid).astype(BF16), w2_ref[0], preferred_element_type=F32)
    row = lax.broadcasted_iota(jnp.int32, (nch, 1), 0)
    o_ref[0, 0, 0] = jnp.where(row < nch - 1, out, 0.0).astype(o_ref.dtype)


def compress(kv_slabs, pos, w1, b1, w2, *, batch, seq_len):
    G = NSA_KV_GROUPS
    nch = seq_len // CMP_STRIDE
    half = CMP_STRIDE * HEAD_DIM
    r = kv_slabs[:2 * G].reshape(2 * G, batch, nch, half)
    hidden = w1.shape[-1]
    return pl.pallas_call(
        _compress_kernel,
        grid=(2, batch, G),
        in_specs=[
            pl.BlockSpec((1, 1, nch, half), lambda kv, b, g: (kv * G + g, b, 0, 0)),
            pl.BlockSpec((1, 2, half), lambda kv, b, g: (kv, 0, 0)),
            pl.BlockSpec((1, 2 * half, hidden), lambda kv, b, g: (kv, 0, 0)),
            pl.BlockSpec((1, 1, hidden), lambda kv, b, g: (kv, 0, 0)),
            pl.BlockSpec((1, hidden, HEAD_DIM), lambda kv, b, g: (kv, 0, 0)),
        ],
        out_specs=pl.BlockSpec((1, 1, 1, nch, HEAD_DIM), lambda kv, b, g: (kv, b, g, 0, 0)),
        out_shape=jax.ShapeDtypeStruct((2, batch, G, nch, HEAD_DIM), BF16),
        compiler_params=_params("parallel", "parallel", "parallel"),
        name="nsa_compress",
    )(r, pos.reshape(2, 2, half), w1, b1.reshape(2, 1, hidden), w2)


def _dot_row_halves(p_ref, width, rhs):
    half = p_ref.shape[0] // 2
    return jnp.concatenate([jnp.dot(p_ref[:half, :width], rhs, preferred_element_type=F32),
                            jnp.dot(p_ref[half:, :width], rhs, preferred_element_type=F32)], axis=0)


def _nsa_kernel(slope_ref, q_ref, gate_ref, kc_ref, vc_ref, ks_ref, vs_ref, kw_ref, vw_ref, o_ref,
                qa_scr, ka_scr, va_scr, p_scr, oc_scr, mx_scr, acc_scr, used_ref, *, kt):
    J, dh = NSA_HPG, HEAD_DIM
    qb = q_ref.shape[0]
    seq_len = ks_ref.shape[2]
    ncp = kc_ref.shape[3]
    ns = seq_len // SEL_BLOCK
    g = pl.program_id(1)
    qi = pl.program_id(2)
    start = qi * qb

    @pl.when(qi == 0)
    def _():
        chunk = min(512, seq_len)

        def body(c, carry):
            r = pl.multiple_of(c * chunk, chunk)
            ka_scr[pl.ds(r, chunk), :dh] = ks_ref[0, 0, pl.ds(r, chunk), :]
            blk = (r + lax.broadcasted_iota(jnp.int32, (chunk, ns), 0)) // SEL_BLOCK
            ka_scr[pl.ds(r, chunk), dh:] = (blk == lax.broadcasted_iota(jnp.int32, (chunk, ns), 1)).astype(BF16)
            va_scr[pl.ds(r, chunk), :dh] = vs_ref[0, 0, pl.ds(r, chunk), :]
            va_scr[pl.ds(r, chunk), dh:] = jnp.ones((chunk, dh), BF16)
            return carry

        lax.fori_loop(0, seq_len // chunk, body, 0)

    q = q_ref[...]
    for j in range(J):
        qa_scr[j * qb:(j + 1) * qb, :dh] = q[:, j * dh:(j + 1) * dh]
    qr = qa_scr[:, :dh]
    slopes = [slope_ref[g * J + j] for j in range(J)]
    t_col = start + lax.broadcasted_iota(jnp.int32, (qb, 1), 0)

    lanes = mx_scr.shape[1]
    kc = kc_ref[0, 0, 0]
    sc = lax.dot_general(qr, kc, _NT, preferred_element_type=F32)
    c_end = lax.broadcasted_iota(jnp.int32, (1, ncp), 1) * CMP_STRIDE + (CMP_LEN - 1)
    bias_c = jnp.where(t_col >= c_end, 0.0, NEG_BIG)
    rel_c = (c_end - start).astype(F32)
    for j in range(J):
        rows = slice(j * qb, (j + 1) * qb)
        sj = sc[rows] + bias_c + slopes[j] * rel_c
        p_scr[rows, :ncp] = jnp.exp(sj - jnp.max(sj, axis=-1, keepdims=True)).astype(BF16)
    c_row = lax.broadcasted_iota(jnp.int32, (ncp, ns), 0) * CMP_STRIDE
    n_col = lax.broadcasted_iota(jnp.int32, (ncp, ns), 1) * SEL_BLOCK
    overlap = ((c_row < n_col + SEL_BLOCK) & (c_row + CMP_LEN > n_col)
               & (c_row < (ncp - 1) * CMP_STRIDE)).astype(BF16)
    rhs_c = jnp.concatenate([vc_ref[0, 0, 0], jnp.ones((ncp, lanes), BF16), overlap], axis=1)
    oc = _dot_row_halves(p_scr, ncp, rhs_c)
    has_key = t_col >= CMP_LEN - 1
    score = jnp.zeros((qb, ns), F32)
    for j in range(J):
        rows = slice(j * qb, (j + 1) * qb)
        inv = jnp.where(has_key, 1.0 / jnp.maximum(oc[rows, dh:dh + lanes], 1e-30), 0.0)
        oc_scr[rows] = oc[rows, :dh] * inv
        score = score + oc[rows, dh + lanes:] * inv[:, :ns]

    wl = WINDOW + qb
    w0 = pl.multiple_of(jnp.maximum(start - WINDOW, 0), qb)
    kw = kw_ref[0, 0, pl.ds(w0, wl), :]
    vw = jnp.concatenate([vw_ref[0, 0, pl.ds(w0, wl), :], jnp.ones((wl, dh), BF16)], axis=1)
    sw = lax.dot_general(qr, kw, _NT, preferred_element_type=F32)
    pos_w = w0 + lax.broadcasted_iota(jnp.int32, (1, wl), 1)
    bias_w = jnp.where((pos_w <= t_col) & (pos_w > t_col - WINDOW), 0.0, NEG_BIG)
    rel_w = (pos_w - start).astype(F32)
    for j in range(J):
        rows = slice(j * qb, (j + 1) * qb)
        sj = sw[rows] + bias_w + slopes[j] * rel_w
        p_scr[rows, :wl] = jnp.exp(sj - jnp.max(sj, axis=-1, keepdims=True)).astype(BF16)
    ow = _dot_row_halves(p_scr, wl, vw)
    o_win = ow[:, :dh] * (1.0 / jnp.maximum(ow[:, dh:], 1e-30))

    score_t = score.T
    n_idx = lax.broadcasted_iota(jnp.int32, (ns, qb), 0)
    cur = (start + lax.broadcasted_iota(jnp.int32, (1, qb), 1)) // SEL_BLOCK
    forced = (n_idx == 0) | (n_idx == cur) | (n_idx == cur - 1)
    score_t = jnp.where(n_idx > cur, -1.0, jnp.where(forced, FORCED_SCORE, score_t))
    sel_t = jnp.zeros((ns, qb), F32)
    for _ in range(min(N_SELECT, ns)):
        best = jnp.max(score_t, axis=0, keepdims=True)
        first = jnp.min(jnp.where(score_t == best, n_idx, ns), axis=0, keepdims=True)
        hit = n_idx == first
        sel_t = jnp.where(hit, 1.0, sel_t)
        score_t = jnp.where(hit, -2.0, score_t)
    block_bias = jnp.where(sel_t.T > 0.5, 0.0, NEG_BIG).astype(BF16)
    for j in range(J):
        qa_scr[j * qb:(j + 1) * qb, dh:] = block_bias
    bpt = kt // SEL_BLOCK
    for i in range(ns // bpt):
        used_ref[i] = (jnp.max(sel_t[i * bpt:(i + 1) * bpt, :]) > 0.5).astype(jnp.int32)

    pos_d = (start // kt) * kt + lax.broadcasted_iota(jnp.int32, (1, kt), 1)
    causal_bias = jnp.where(pos_d <= t_col, 0.0, NEG_BIG)

    def scores(i, j, s, diagonal):
        pos = i * kt + lax.broadcasted_iota(jnp.int32, (1, kt), 1)
        sj = s[j * qb:(j + 1) * qb] + slopes[j] * (pos - start).astype(F32)
        if diagonal:
            sj = sj + causal_bias
        return sj

    def max_tile(i, diagonal):
        k0 = pl.multiple_of(i * kt, kt)
        s = lax.dot_general(qa_scr[...], ka_scr[pl.ds(k0, kt), :], _NT, preferred_element_type=F32)
        for j in range(J):
            rows = slice(j * qb, (j + 1) * qb)
            sj = scores(i, j, s, diagonal)
            mx = mx_scr[rows]
            for c in range(kt // lanes):
                mx = jnp.maximum(mx, sj[:, c * lanes:(c + 1) * lanes])
            mx_scr[rows] = mx

    def pv_tile(i, diagonal):
        k0 = pl.multiple_of(i * kt, kt)
        s = lax.dot_general(qa_scr[...], ka_scr[pl.ds(k0, kt), :], _NT, preferred_element_type=F32)
        for j in range(J):
            rows = slice(j * qb, (j + 1) * qb)
            sj = scores(i, j, s, diagonal)
            m = mx_scr[rows]
            for c in range(kt // lanes):
                p_scr[rows, c * lanes:(c + 1) * lanes] = jnp.exp(sj[:, c * lanes:(c + 1) * lanes] - m).astype(BF16)
        acc_scr[...] += _dot_row_halves(p_scr, kt, va_scr[pl.ds(k0, kt), :])

    def run_pass(tile_fn):
        def full(i, carry):
            @pl.when(used_ref[i] > 0)
            def _():
                tile_fn(i, False)

            return carry

        n_full = start // kt

        @pl.when(n_full == 0)
        def _():
            tile_fn(0, True)

        @pl.when(n_full > 0)
        def _():
            tile_fn(n_full, True)
            tile_fn(0, False)

        lax.fori_loop(1, n_full, full, 0)

    mx_scr[...] = jnp.full(mx_scr.shape, NEG_BIG, F32)
    run_pass(max_tile)
    for j in range(J):
        rows = slice(j * qb, (j + 1) * qb)
        mx_scr[rows] = jnp.broadcast_to(jnp.max(mx_scr[rows], axis=-1, keepdims=True), (qb, lanes))
    acc_scr[...] = jnp.zeros(acc_scr.shape, F32)
    run_pass(pv_tile)

    gate = gate_ref[0]
    for j in range(J):
        rows = slice(j * qb, (j + 1) * qb)
        o_sel = acc_scr[rows, :dh] * (1.0 / jnp.maximum(acc_scr[rows, dh:], 1e-30))
        o = (gate[:, j:j + 1] * oc_scr[rows] + gate[:, J + j:J + j + 1] * o_sel
             + gate[:, 2 * J + j:2 * J + j + 1] * o_win[rows])
        o_ref[:, j * dh:(j + 1) * dh] = o.astype(o_ref.dtype)


def nsa_attention(q, gates, kv_cmp, kv_slabs, slopes, *, batch, seq_len, qb=128, kt=512):
    G, J, dh = NSA_KV_GROUPS, NSA_HPG, HEAD_DIM
    M = q.shape[0]
    nq = seq_len // qb
    ncp = kv_cmp.shape[3]
    ns = seq_len // SEL_BLOCK
    assert seq_len % kt == 0 and kt % qb == 0 and seq_len >= WINDOW + qb
    slabs = kv_slabs.reshape(kv_slabs.shape[0], batch, seq_len, dh)

    def slab(branch, is_v):
        base = (branch * 2 + is_v) * G
        return pl.BlockSpec((1, 1, seq_len, dh), lambda b, g, i, sl: (base + g, b, 0, 0))

    def cmp_spec(is_v):
        return pl.BlockSpec((1, 1, 1, ncp, dh), lambda b, g, i, sl: (is_v, b, g, 0, 0))

    grid_spec = pltpu.PrefetchScalarGridSpec(
        num_scalar_prefetch=1,
        grid=(batch, G, nq),
        in_specs=[
            pl.BlockSpec((qb, J * dh), lambda b, g, i, sl: (b * nq + i, g)),
            pl.BlockSpec((1, qb, NSA_BRANCHES * J), lambda b, g, i, sl: (g, b * nq + i, 0)),
            cmp_spec(0), cmp_spec(1),
            slab(1, 0), slab(1, 1), slab(2, 0), slab(2, 1),
        ],
        out_specs=pl.BlockSpec((qb, J * dh), lambda b, g, i, sl: (b * nq + i, g)),
        scratch_shapes=[
            pltpu.VMEM((J * qb, dh + ns), BF16),
            pltpu.VMEM((seq_len, dh + ns), BF16),
            pltpu.VMEM((seq_len, 2 * dh), BF16),
            pltpu.VMEM((J * qb, max(ncp, WINDOW + qb, kt)), BF16),
            pltpu.VMEM((J * qb, dh), F32),
            pltpu.VMEM((J * qb, LANES), F32),
            pltpu.VMEM((J * qb, 2 * dh), F32),
            pltpu.SMEM((seq_len // kt,), jnp.int32),
        ],
    )
    return pl.pallas_call(
        functools.partial(_nsa_kernel, kt=kt),
        grid_spec=grid_spec,
        out_shape=jax.ShapeDtypeStruct((M, G * J * dh), BF16),
        compiler_params=_params("parallel", "parallel", "arbitrary"),
        name="nsa_attention",
    )(slopes, q, gates, kv_cmp, kv_cmp, slabs, slabs, slabs, slabs)


def _ffn_layer(h, g0, g1, w_gu, w_down, layer, conv_w, conv_b, *, seq_len):
    act = ffn_up(h, g0, w_gu, layer, conv_w, conv_b, seq_len=seq_len)
    f = matmul(act, w_down, layer, tm=512, tn=512)
    return residual_norm(h, f, g1)


def _xattn(h, mem_rows, mem_norm, g0, g1, wq, wkv, wo, layer, *, batch, seq_len, pending=None):
    mem_len = mem_rows.shape[0] // batch
    kv = norm_matmul(mem_rows, mem_norm, wkv, layer, tm=mem_rows.shape[0], tn=wkv.shape[2] // 2, out_dtype=BF16)
    kv = kv.reshape(batch, mem_len, wkv.shape[2])
    return xattn_layer(h, g0, wq[layer], kv, wo[layer], g1, seq_len=seq_len, pending=pending)


def _nsa_mixer(h, g0, w_in, w_out, layer, cmp_pos, cmp_w1, cmp_b1, cmp_w2, *, batch, seq_len):
    G, J, dh = NSA_KV_GROUPS, NSA_HPG, HEAD_DIM
    qw = G * J * dh
    kvw = NSA_BRANCHES * 2 * G * dh
    w_qkv = w_in[:, :, :qw + kvw].astype(BF16)
    w_gate = w_in[layer, :, qw + kvw:].reshape(-1, NSA_BRANCHES, G, J).transpose(0, 2, 1, 3)
    w_gate = w_gate.reshape(-1, G * NSA_BRANCHES * J).astype(BF16)
    q, kv_slabs, gates = nsa_in_proj(h, g0, w_qkv, layer, w_gate, q_width=qw, kv_width=kvw)
    kv_cmp = compress(kv_slabs, cmp_pos, cmp_w1.astype(BF16), cmp_b1, cmp_w2.astype(BF16),
                      batch=batch, seq_len=seq_len)
    n_heads = G * J
    slopes = 2.0 ** (-8.0 * jnp.arange(1, n_heads + 1, dtype=F32) / n_heads)
    o = nsa_attention(q, gates, kv_cmp, kv_slabs, slopes, batch=batch, seq_len=seq_len)
    return matmul(o, w_out, layer, tm=1024, tn=512)


def kernel(x, mem, ln_mix, ln_xa, ln_ffn, mem_norm, pool_w, pool_scale, nsa_w_in, nsa_w_out, nsa_cmp_pos, nsa_cmp_w1, nsa_cmp_b1, nsa_cmp_w2, xa_wq, xa_wkv, xa_wo, ffn_w_gu, ffn_conv_w, ffn_conv_b, ffn_w_down):
    B, S, D = x.shape
    depth = ln_mix.shape[0]
    h = x.reshape(B * S, D)
    mem_rows = mem.reshape(B * mem.shape[1], D)
    nsa_w_out = nsa_w_out.astype(BF16)
    xa_wq, xa_wkv, xa_wo = xa_wq.astype(BF16), xa_wkv.astype(BF16), xa_wo.astype(BF16)
    ffn_w_down = ffn_w_down.astype(BF16)
    for i in range(depth):
        j = i // 2
        pending = None
        if i % 2 == 0:
            h = pool_layer(h, ln_mix[i, 0], pool_w[j].astype(BF16), pool_scale[j], ln_mix[i, 1], seq_len=S)
        else:
            a = _nsa_mixer(h, ln_mix[i, 0], nsa_w_in, nsa_w_out, j, nsa_cmp_pos[j],
                           nsa_cmp_w1[j], nsa_cmp_b1[j], nsa_cmp_w2[j], batch=B, seq_len=S)
            pending = (a, ln_mix[i, 1])
        h = _xattn(h, mem_rows, mem_norm, ln_xa[i, 0], ln_xa[i, 1], xa_wq, xa_wkv, xa_wo, i, batch=B, seq_len=S,
                   pending=pending)
        h = _ffn_layer(h, ln_ffn[i, 0], ln_ffn[i, 1], ffn_w_gu, ffn_w_down, i, ffn_conv_w[i], ffn_conv_b[i],
                       seq_len=S)
    return h.reshape(B, S, D)
```

```python
import functools

import jax
import jax.numpy as jnp
from jax import lax
from jax.experimental import pallas as pl
from jax.experimental.pallas import tpu as pltpu

F32 = jnp.float32
BF16 = jnp.bfloat16

RMS_EPS = 1e-6
NEG_BIG = -1e30
HEAD_DIM = 128
POOL_WINDOWS = (2, 4, 8, 16)
POOL_HALO = 16
NSA_KV_GROUPS = 4
NSA_HPG = 8
NSA_BRANCHES = 3
CMP_LEN = 32
CMP_STRIDE = 16
SEL_BLOCK = 64
N_SELECT = 16
WINDOW = 512
FORCED_SCORE = 1e6
XA_HEADS = 4
CONV_HALO = 8
VMEM_LIMIT = 56 * 1024 * 1024
NORM_CHUNK = 128
LANES = 128

_NT = (((1,), (1,)), ((), ()))


def _params(*sem):
    return pltpu.CompilerParams(dimension_semantics=sem, vmem_limit_bytes=VMEM_LIMIT)


def _rms(x, g):
    ms = jnp.mean(x * x, axis=-1, keepdims=True)
    return x * lax.rsqrt(ms + RMS_EPS) * g


def _norm_rows(h_ref, g_ref, a_scr):
    tm = h_ref.shape[0]
    chunk = min(NORM_CHUNK, tm)

    def body(c, carry):
        r = pl.multiple_of(c * chunk, chunk)
        a_scr[pl.ds(r, chunk), :] = _rms(h_ref[pl.ds(r, chunk), :], g_ref[...]).astype(BF16)
        return carry

    lax.fori_loop(0, tm // chunk, body, 0)


def _once(shape, index_map):
    return pl.BlockSpec(shape, index_map, pipeline_mode=pl.Buffered(1))


def _norm_mm_kernel(h_ref, g_ref, w_ref, o_ref, a_scr):
    @pl.when(pl.program_id(1) == 0)
    def _():
        _norm_rows(h_ref, g_ref, a_scr)

    o_ref[...] = jnp.dot(a_scr[...], w_ref[...], preferred_element_type=F32).astype(o_ref.dtype)


def norm_matmul(h, g, w, layer, *, tm, tn, out_dtype):
    M, D = h.shape
    N = w.shape[2]
    assert M % tm == 0 and N % tn == 0
    return pl.pallas_call(
        _norm_mm_kernel,
        grid=(M // tm, N // tn),
        in_specs=[
            _once((tm, D), lambda i, j: (i, 0)),
            pl.BlockSpec((1, D), lambda i, j: (0, 0)),
            pl.BlockSpec((None, D, tn), lambda i, j: (layer, 0, j)),
        ],
        out_specs=pl.BlockSpec((tm, tn), lambda i, j: (i, j)),
        out_shape=jax.ShapeDtypeStruct((M, N), out_dtype),
        scratch_shapes=[pltpu.VMEM((tm, D), BF16)],
        compiler_params=_params("parallel", "arbitrary"),
        name="norm_matmul",
    )(h, g.reshape(1, D), w)


def _nsa_proj_kernel(h_ref, g_ref, w_ref, wg_ref, q_ref, kv_ref, gate_ref, a_scr, *, nq, nkv, q_scale):
    j = pl.program_id(1)

    @pl.when(j == 0)
    def _():
        _norm_rows(h_ref, g_ref, a_scr)

    @pl.when(j < nq)
    def _():
        acc = jnp.dot(a_scr[...], w_ref[...], preferred_element_type=F32)
        q_ref[...] = (acc * q_scale).astype(q_ref.dtype)

    @pl.when((j >= nq) & (j < nq + nkv))
    def _():
        acc = jnp.dot(a_scr[...], w_ref[...], preferred_element_type=F32)
        for s in range(kv_ref.shape[0]):
            kv_ref[s] = acc[:, s * HEAD_DIM:(s + 1) * HEAD_DIM].astype(kv_ref.dtype)

    @pl.when(j == nq + nkv)
    def _():
        sig = jax.nn.sigmoid(jnp.dot(a_scr[...], wg_ref[...], preferred_element_type=F32))
        per_group = gate_ref.shape[2]
        for grp in range(gate_ref.shape[0]):
            gate_ref[grp] = sig[:, grp * per_group:(grp + 1) * per_group]


def nsa_in_proj(h, g, w_in, layer, w_gate, *, q_width, kv_width, tm=1024, tn=512):
    M, D = h.shape
    G = NSA_KV_GROUPS
    n_gate = w_gate.shape[1]
    assert M % tm == 0 and q_width % tn == 0 and kv_width % tn == 0
    nq, nkv = q_width // tn, kv_width // tn
    spt = tn // HEAD_DIM
    return pl.pallas_call(
        functools.partial(_nsa_proj_kernel, nq=nq, nkv=nkv, q_scale=HEAD_DIM ** -0.5),
        grid=(M // tm, nq + nkv + 1),
        in_specs=[
            _once((tm, D), lambda i, j: (i, 0)),
            pl.BlockSpec((1, D), lambda i, j: (0, 0)),
            pl.BlockSpec((None, D, tn), lambda i, j: (layer, 0, jnp.minimum(j, nq + nkv - 1))),
            pl.BlockSpec((D, n_gate), lambda i, j: (0, 0)),
        ],
        out_specs=[
            pl.BlockSpec((tm, tn), lambda i, j: (i, jnp.minimum(j, nq - 1))),
            pl.BlockSpec((spt, tm, HEAD_DIM), lambda i, j: (jnp.clip(j - nq, 0, nkv - 1), i, 0)),
            pl.BlockSpec((G, tm, n_gate // G), lambda i, j: (0, i, 0)),
        ],
        out_shape=[
            jax.ShapeDtypeStruct((M, q_width), BF16),
            jax.ShapeDtypeStruct((kv_width // HEAD_DIM, M, HEAD_DIM), BF16),
            jax.ShapeDtypeStruct((G, M, n_gate // G), F32),
        ],
        scratch_shapes=[pltpu.VMEM((tm, D), BF16)],
        compiler_params=_params("parallel", "arbitrary"),
        name="nsa_in_proj",
    )(h, g.reshape(1, D), w_in, w_gate)


def _mm_kernel(a_ref, w_ref, o_ref):
    o_ref[...] = jnp.dot(a_ref[...], w_ref[...], preferred_element_type=F32).astype(o_ref.dtype)


def matmul(a, w, layer, *, tm, tn, out_dtype=F32):
    M, K = a.shape
    N = w.shape[2]
    assert M % tm == 0 and N % tn == 0
    return pl.pallas_call(
        _mm_kernel,
        grid=(M // tm, N // tn),
        in_specs=[
            _once((tm, K), lambda i, j: (i, 0)),
            pl.BlockSpec((None, K, tn), lambda i, j: (layer, 0, j)),
        ],
        out_specs=pl.BlockSpec((tm, tn), lambda i, j: (i, j)),
        out_shape=jax.ShapeDtypeStruct((M, N), out_dtype),
        compiler_params=_params("parallel", "arbitrary"),
        name="matmul",
    )(a, w)


def _resid_kernel(h_ref, f_ref, g_ref, o_ref):
    o_ref[...] = h_ref[...] + _rms(f_ref[...], g_ref[...])


def residual_norm(h, f, g, *, tm=256):
    M, D = h.shape
    row = pl.BlockSpec((tm, D), lambda i: (i, 0))
    return pl.pallas_call(
        _resid_kernel,
        grid=(M // tm,),
        in_specs=[row, row, pl.BlockSpec((1, D), lambda i: (0, 0))],
        out_specs=row,
        out_shape=jax.ShapeDtypeStruct((M, D), F32),
        compiler_params=_params("parallel"),
        name="residual_norm",
    )(h, f, g.reshape(1, D))


def _pool_kernel(x_ref, halo_ref, g0_ref, w_ref, sc_ref, g1_ref, o_ref, *, seq_len):
    tm, D = x_ref.shape
    gc = D // len(POOL_WINDOWS)
    t0 = (pl.program_id(0) * tm) % seq_len
    x = x_ref[...]
    a = _rms(x, g0_ref[...])
    ha = _rms(halo_ref[...], g0_ref[...])
    ha = jnp.where(t0 == 0, 0.0, ha)
    full = jnp.concatenate([ha, a], axis=0)
    t1 = (t0 + 1 + lax.broadcasted_iota(jnp.int32, (tm, 1), 0)).astype(F32)
    ys = []
    for gi, win in enumerate(POOL_WINDOWS):
        s = full[:, gi * gc:(gi + 1) * gc]
        k = 1
        while k < win:
            s = s + pltpu.roll(s, k, 0)
            k *= 2
        inv_cnt = 1.0 / jnp.minimum(t1, float(win))
        pooled = s[POOL_HALO:] * inv_cnt
        d = (pooled - a[:, gi * gc:(gi + 1) * gc]).astype(BF16)
        ys.append(jnp.dot(d, w_ref[gi], preferred_element_type=F32))
    y = jnp.concatenate(ys, axis=-1) * sc_ref[...]
    o_ref[...] = x + _rms(y, g1_ref[...])


def pool_layer(x, g0, w, scale, g1, *, seq_len, tm=256):
    M, D = x.shape
    G, gc, _ = w.shape
    vec = pl.BlockSpec((1, D), lambda i: (0, 0))
    hb = tm // POOL_HALO
    return pl.pallas_call(
        functools.partial(_pool_kernel, seq_len=seq_len),
        grid=(M // tm,),
        in_specs=[
            pl.BlockSpec((tm, D), lambda i: (i, 0)),
            pl.BlockSpec((POOL_HALO, D), lambda i: (jnp.maximum(i * hb - 1, 0), 0)),
            vec,
            _once((G, gc, gc), lambda i: (0, 0, 0)),
            vec,
            vec,
        ],
        out_specs=pl.BlockSpec((tm, D), lambda i: (i, 0)),
        out_shape=jax.ShapeDtypeStruct((M, D), F32),
        compiler_params=_params("parallel"),
        name="pool_layer",
    )(x, x, g0.reshape(1, D), w, scale.reshape(1, D), g1.reshape(1, D))


def _xattn_kernel(h_ref, *refs, pending):
    h = h_ref[...]
    if pending:
        a_ref, ga_ref, *refs = refs
        h = h + _rms(a_ref[...], ga_ref[...])
    g0_ref, wq_ref, k_ref, v_ref, wo_ref, g1_ref, o_ref = refs
    n = _rms(h, g0_ref[...]).astype(BF16)
    q = jnp.dot(n, wq_ref[...], preferred_element_type=F32).astype(BF16)
    k = k_ref[0]
    v = v_ref[0]
    scale = HEAD_DIM ** -0.5
    outs = []
    for hh in range(XA_HEADS):
        sl = slice(hh * HEAD_DIM, (hh + 1) * HEAD_DIM)
        s = lax.dot_general(q[:, sl], k[:, sl], _NT, preferred_element_type=F32) * scale
        m = jnp.max(s, axis=-1, keepdims=True)
        e = jnp.exp(s - m)
        p = e * (1.0 / jnp.sum(e, axis=-1, keepdims=True))
        outs.append(jnp.dot(p.astype(BF16), v[:, sl], preferred_element_type=F32))
    o = jnp.concatenate(outs, axis=-1).astype(BF16)
    c = jnp.dot(o, wo_ref[...], preferred_element_type=F32)
    o_ref[...] = h + _rms(c, g1_ref[...])


def xattn_layer(h, g0, wq, kv, wo, g1, *, seq_len, pending=None, tm=256):
    M, D = h.shape
    xw = wq.shape[1]
    mem_len = kv.shape[1]
    per_seq = seq_len // tm
    vec = pl.BlockSpec((1, D), lambda i: (0, 0))
    row = pl.BlockSpec((tm, D), lambda i: (i, 0))
    pend_specs, pend_args = ([row, vec], [pending[0], pending[1].reshape(1, D)]) if pending is not None else ([], [])
    return pl.pallas_call(
        functools.partial(_xattn_kernel, pending=pending is not None),
        grid=(M // tm,),
        in_specs=[
            row,
            *pend_specs,
            vec,
            _once((D, xw), lambda i: (0, 0)),
            pl.BlockSpec((1, mem_len, xw), lambda i: (i // per_seq, 0, 0)),
            pl.BlockSpec((1, mem_len, xw), lambda i: (i // per_seq, 0, 1)),
            _once((xw, D), lambda i: (0, 0)),
            vec,
        ],
        out_specs=pl.BlockSpec((tm, D), lambda i: (i, 0)),
        out_shape=jax.ShapeDtypeStruct((M, D), F32),
        compiler_params=_params("parallel"),
        name="xattn_layer",
    )(h, *pend_args, g0.reshape(1, D), wq, kv, kv, wo, g1.reshape(1, D))


def _ffn_up_kernel(h_ref, halo_ref, g_ref, wg_ref, wu_ref, cw_ref, cb_ref, o_ref, a_scr, halo_scr, *, seq_len):
    tm = h_ref.shape[0]

    @pl.when(pl.program_id(1) == 0)
    def _():
        _norm_rows(h_ref, g_ref, a_scr)
        t0 = (pl.program_id(0) * tm) % seq_len
        hn = _rms(halo_ref[...], g_ref[...])
        halo_scr[...] = jnp.where(t0 == 0, 0.0, hn).astype(BF16)

    a = a_scr[...]
    wg = wg_ref[...].astype(BF16)
    gate = jnp.dot(a, wg, preferred_element_type=F32)
    up = jnp.dot(a, wu_ref[...].astype(BF16), preferred_element_type=F32)
    hg = jnp.dot(halo_scr[...], wg, preferred_element_type=F32)
    row = lax.broadcasted_iota(jnp.int32, (tm, 1), 0)
    prev1 = hg[CONV_HALO - 1:CONV_HALO]
    prev2 = hg[CONV_HALO - 2:CONV_HALO - 1]
    g1 = jnp.where(row == 0, prev1, pltpu.roll(gate, 1, 0))
    g2 = jnp.where(row == 0, prev2, jnp.where(row == 1, prev1, pltpu.roll(gate, 2, 0)))
    cw = cw_ref[...]
    gc = cw[0:1] * g2 + cw[1:2] * g1 + cw[2:3] * gate + cb_ref[...]
    o_ref[...] = (gc * jax.nn.sigmoid(gc) * up).astype(o_ref.dtype)


def ffn_up(h, g, w_gu, layer, conv_w, conv_b, *, seq_len, tm=1024, tn=256):
    M, D = h.shape
    F = w_gu.shape[2] // 2
    assert F % tn == 0 and M % tm == 0 and seq_len % tm == 0
    nj = F // tn
    hb = tm // CONV_HALO
    return pl.pallas_call(
        functools.partial(_ffn_up_kernel, seq_len=seq_len),
        grid=(M // tm, nj),
        in_specs=[
            _once((tm, D), lambda i, j: (i, 0)),
            _once((CONV_HALO, D), lambda i, j: (jnp.maximum(i * hb - 1, 0), 0)),
            pl.BlockSpec((1, D), lambda i, j: (0, 0)),
            pl.BlockSpec((None, D, tn), lambda i, j: (layer, 0, j)),
            pl.BlockSpec((None, D, tn), lambda i, j: (layer, 0, j + nj)),
            pl.BlockSpec((conv_w.shape[0], tn), lambda i, j: (0, j)),
            pl.BlockSpec((1, tn), lambda i, j: (0, j)),
        ],
        out_specs=pl.BlockSpec((tm, tn), lambda i, j: (i, j)),
        out_shape=jax.ShapeDtypeStruct((M, F), BF16),
        scratch_shapes=[pltpu.VMEM((tm, D), BF16), pltpu.VMEM((CONV_HALO, D), BF16)],
        compiler_params=_params("parallel", "arbitrary"),
        name="ffn_up",
    )(h, h, g.reshape(1, D), w_gu, w_gu, conv_w, conv_b.reshape(1, F))


def _gelu_tanh(x):
    return 0.5 * x * (1.0 + jnp.tanh(0.7978845608028654 * (x + 0.044715 * x * x * x)))


def _compress_kernel(r_ref, pos_ref, w1_ref, b1_ref, w2_ref, o_ref):
    r = r_ref[0, 0].astype(F32)
    nch, half = r.shape
    pos = pos_ref[0]
    xa = (r + pos[0:1]).astype(BF16)
    xb = (r + pos[1:2]).astype(BF16)
    h1 = jnp.dot(xa, w1_ref[0, :half, :], preferred_element_type=F32)
    h2 = jnp.dot(xb, w1_ref[0, half:, :], preferred_element_type=F32)
    hid = h1 + pltpu.roll(h2, nch - 1, 0) + b1_ref[0]
    out = jnp.dot(_gelu_tanh(hid).astype(BF16), w2_ref[0], preferred_element_type=F32)
    row = lax.broadcasted_iota(jnp.int32, (nch, 1), 0)
    o_ref[0, 0, 0] = jnp.where(row < nch - 1, out, 0.0).astype(o_ref.dtype)


def compress(kv_slabs, pos, w1, b1, w2, *, batch, seq_len):
    G = NSA_KV_GROUPS
    nch = seq_len // CMP_STRIDE
    half = CMP_STRIDE * HEAD_DIM
    r = kv_slabs[:2 * G].reshape(2 * G, batch, nch, half)
    hidden = w1.shape[-1]
    return pl.pallas_call(
        _compress_kernel,
        grid=(2, batch, G),
        in_specs=[
            pl.BlockSpec((1, 1, nch, half), lambda kv, b, g: (kv * G + g, b, 0, 0)),
            pl.BlockSpec((1, 2, half), lambda kv, b, g: (kv, 0, 0)),
            pl.BlockSpec((1, 2 * half, hidden), lambda kv, b, g: (kv, 0, 0)),
            pl.BlockSpec((1, 1, hidden), lambda kv, b, g: (kv, 0, 0)),
            pl.BlockSpec((1, hidden, HEAD_DIM), lambda kv, b, g: (kv, 0, 0)),
        ],
        out_specs=pl.BlockSpec((1, 1, 1, nch, HEAD_DIM), lambda kv, b, g: (kv, b, g, 0, 0)),
        out_shape=jax.ShapeDtypeStruct((2, batch, G, nch, HEAD_DIM), BF16),
        compiler_params=_params("parallel", "parallel", "parallel"),
        name="nsa_compress",
    )(r, pos.reshape(2, 2, half), w1, b1.reshape(2, 1, hidden), w2)


def _dot_row_halves(p_ref, width, rhs):
    half = p_ref.shape[0] // 2
    return jnp.concatenate([jnp.dot(p_ref[:half, :width], rhs, preferred_element_type=F32),
                            jnp.dot(p_ref[half:, :width], rhs, preferred_element_type=F32)], axis=0)


def _nsa_kernel(slope_ref, q_ref, gate_ref, kc_ref, vc_ref, ks_ref, vs_ref, kw_ref, vw_ref, o_ref,
                qa_scr, ka_scr, va_scr, p_scr, oc_scr, mx_scr, acc_scr, used_ref, *, kt):
    J, dh = NSA_HPG, HEAD_DIM
    qb = q_ref.shape[0]
    seq_len = ks_ref.shape[2]
    ncp = kc_ref.shape[3]
    ns = seq_len // SEL_BLOCK
    g = pl.program_id(1)
    qi = pl.program_id(2)
    start = qi * qb

    @pl.when(qi == 0)
    def _():
        chunk = min(512, seq_len)

        def body(c, carry):
            r = pl.multiple_of(c * chunk, chunk)
            ka_scr[pl.ds(r, chunk), :dh] = ks_ref[0, 0, pl.ds(r, chunk), :]
            blk = (r + lax.broadcasted_iota(jnp.int32, (chunk, ns), 0)) // SEL_BLOCK
            ka_scr[pl.ds(r, chunk), dh:] = (blk == lax.broadcasted_iota(jnp.int32, (chunk, ns), 1)).astype(BF16)
            va_scr[pl.ds(r, chunk), :dh] = vs_ref[0, 0, pl.ds(r, chunk), :]
            va_scr[pl.ds(r, chunk), dh:] = jnp.ones((chunk, dh), BF16)
            return carry

        lax.fori_loop(0, seq_len // chunk, body, 0)

    q = q_ref[...]
    for j in range(J):
        qa_scr[j * qb:(j + 1) * qb, :dh] = q[:, j * dh:(j + 1) * dh]
    qr = qa_scr[:, :dh]
    slopes = [slope_ref[g * J + j] for j in range(J)]
    t_col = start + lax.broadcasted_iota(jnp.int32, (qb, 1), 0)

    lanes = mx_scr.shape[1]
    kc = kc_ref[0, 0, 0]
    sc = lax.dot_general(qr, kc, _NT, preferred_element_type=F32)
    c_end = lax.broadcasted_iota(jnp.int32, (1, ncp), 1) * CMP_STRIDE + (CMP_LEN - 1)
    bias_c = jnp.where(t_col >= c_end, 0.0, NEG_BIG)
    rel_c = (c_end - start).astype(F32)
    for j in range(J):
        rows = slice(j * qb, (j + 1) * qb)
        sj = sc[rows] + bias_c + slopes[j] * rel_c
        p_scr[rows, :ncp] = jnp.exp(sj - jnp.max(sj, axis=-1, keepdims=True)).astype(BF16)
    c_row = lax.broadcasted_iota(jnp.int32, (ncp, ns), 0) * CMP_STRIDE
    n_col = lax.broadcasted_iota(jnp.int32, (ncp, ns), 1) * SEL_BLOCK
    overlap = ((c_row < n_col + SEL_BLOCK) & (c_row + CMP_LEN > n_col)
               & (c_row < (ncp - 1) * CMP_STRIDE)).astype(BF16)
    rhs_c = jnp.concatenate([vc_ref[0, 0, 0], jnp.ones((ncp, lanes), BF16), overlap], axis=1)
    oc = _dot_row_halves(p_scr, ncp, rhs_c)
    has_key = t_col >= CMP_LEN - 1
    score = jnp.zeros((qb, ns), F32)
    for j in range(J):
        rows = slice(j * qb, (j + 1) * qb)
        inv = jnp.where(has_key, 1.0 / jnp.maximum(oc[rows, dh:dh + lanes], 1e-30), 0.0)
        oc_scr[rows] = oc[rows, :dh] * inv
        score = score + oc[rows, dh + lanes:] * inv[:, :ns]

    wl = WINDOW + qb
    w0 = pl.multiple_of(jnp.maximum(start - WINDOW, 0), qb)
    kw = kw_ref[0, 0, pl.ds(w0, wl), :]
    vw = jnp.concatenate([vw_ref[0, 0, pl.ds(w0, wl), :], jnp.ones((wl, dh), BF16)], axis=1)
    sw = lax.dot_general(qr, kw, _NT, preferred_element_type=F32)
    pos_w = w0 + lax.broadcasted_iota(jnp.int32, (1, wl), 1)
    bias_w = jnp.where((pos_w <= t_col) & (pos_w > t_col - WINDOW), 0.0, NEG_BIG)
    rel_w = (pos_w - start).astype(F32)
    for j in range(J):
        rows = slice(j * qb, (j + 1) * qb)
        sj = sw[rows] + bias_w + slopes[j] * rel_w
        p_scr[rows, :wl] = jnp.exp(sj - jnp.max(sj, axis=-1, keepdims=True)).astype(BF16)
    ow = _dot_row_halves(p_scr, wl, vw)
    o_win = ow[:, :dh] * (1.0 / jnp.maximum(ow[:, dh:], 1e-30))

    score_t = score.T
    n_idx = lax.broadcasted_iota(jnp.int32, (ns, qb), 0)
    cur = (start + lax.broadcasted_iota(jnp.int32, (1, qb), 1)) // SEL_BLOCK
    forced = (n_idx == 0) | (n_idx == cur) | (n_idx == cur - 1)
    score_t = jnp.where(n_idx > cur, -1.0, jnp.where(forced, FORCED_SCORE, score_t))
    sel_t = jnp.zeros((ns, qb), F32)
    for _ in range(min(N_SELECT, ns)):
        best = jnp.max(score_t, axis=0, keepdims=True)
        first = jnp.min(jnp.where(score_t == best, n_idx, ns), axis=0, keepdims=True)
        hit = n_idx == first
        sel_t = jnp.where(hit, 1.0, sel_t)
        score_t = jnp.where(hit, -2.0, score_t)
    block_bias = jnp.where(sel_t.T > 0.5, 0.0, NEG_BIG).astype(BF16)
    for j in range(J):
        qa_scr[j * qb:(j + 1) * qb, dh:] = block_bias
    bpt = kt // SEL_BLOCK
    for i in range(ns // bpt):
        used_ref[i] = (jnp.max(sel_t[i * bpt:(i + 1) * bpt, :]) > 0.5).astype(jnp.int32)

    pos_d = (start // kt) * kt + lax.broadcasted_iota(jnp.int32, (1, kt), 1)
    causal_bias = jnp.where(pos_d <= t_col, 0.0, NEG_BIG)

    def scores(i, j, s, diagonal):
        pos = i * kt + lax.broadcasted_iota(jnp.int32, (1, kt), 1)
        sj = s[j * qb:(j + 1) * qb] + slopes[j] * (pos - start).astype(F32)
        if diagonal:
            sj = sj + causal_bias
        return sj

    def max_tile(i, diagonal):
        k0 = pl.multiple_of(i * kt, kt)
        s = lax.dot_general(qa_scr[...], ka_scr[pl.ds(k0, kt), :], _NT, preferred_element_type=F32)
        for j in range(J):
            rows = slice(j * qb, (j + 1) * qb)
            sj = scores(i, j, s, diagonal)
            mx = mx_scr[rows]
            for c in range(kt // lanes):
                mx = jnp.maximum(mx, sj[:, c * lanes:(c + 1) * lanes])
            mx_scr[rows] = mx

    def pv_tile(i, diagonal):
        k0 = pl.multiple_of(i * kt, kt)
        s = lax.dot_general(qa_scr[...], ka_scr[pl.ds(k0, kt), :], _NT, preferred_element_type=F32)
        for j in range(J):
            rows = slice(j * qb, (j + 1) * qb)
            sj = scores(i, j, s, diagonal)
            m = mx_scr[rows]
            for c in range(kt // lanes):
                p_scr[rows, c * lanes:(c + 1) * lanes] = jnp.exp(sj[:, c * lanes:(c + 1) * lanes] - m).astype(BF16)
        acc_scr[...] += _dot_row_halves(p_scr, kt, va_scr[pl.ds(k0, kt), :])

    def run_pass(tile_fn):
        def full(i, carry):
            @pl.when(used_ref[i] > 0)
            def _():
                tile_fn(i, False)

            return carry

        n_full = start // kt
        lax.fori_loop(0, n_full, full, 0)
        tile_fn(n_full, True)

    mx_scr[...] = jnp.full(mx_scr.shape, NEG_BIG, F32)
    run_pass(max_tile)
    for j in range(J):
        rows = slice(j * qb, (j + 1) * qb)
        mx_scr[rows] = jnp.broadcast_to(jnp.max(mx_scr[rows], axis=-1, keepdims=True), (qb, lanes))
    acc_scr[...] = jnp.zeros(acc_scr.shape, F32)
    run_pass(pv_tile)

    gate = gate_ref[0]
    for j in range(J):
        rows = slice(j * qb, (j + 1) * qb)
        o_sel = acc_scr[rows, :dh] * (1.0 / jnp.maximum(acc_scr[rows, dh:], 1e-30))
        o = (gate[:, j:j + 1] * oc_scr[rows] + gate[:, J + j:J + j + 1] * o_sel
             + gate[:, 2 * J + j:2 * J + j + 1] * o_win[rows])
        o_ref[:, j * dh:(j + 1) * dh] = o.astype(o_ref.dtype)


def nsa_attention(q, gates, kv_cmp, kv_slabs, slopes, *, batch, seq_len, qb=256, kt=512):
    G, J, dh = NSA_KV_GROUPS, NSA_HPG, HEAD_DIM
    M = q.shape[0]
    nq = seq_len // qb
    ncp = kv_cmp.shape[3]
    ns = seq_len // SEL_BLOCK
    assert seq_len % kt == 0 and kt % qb == 0 and seq_len >= WINDOW + qb
    slabs = kv_slabs.reshape(kv_slabs.shape[0], batch, seq_len, dh)

    def slab(branch, is_v):
        base = (branch * 2 + is_v) * G
        return pl.BlockSpec((1, 1, seq_len, dh), lambda b, g, i, sl: (base + g, b, 0, 0))

    def cmp_spec(is_v):
        return pl.BlockSpec((1, 1, 1, ncp, dh), lambda b, g, i, sl: (is_v, b, g, 0, 0))

    grid_spec = pltpu.PrefetchScalarGridSpec(
        num_scalar_prefetch=1,
        grid=(batch, G, nq),
        in_specs=[
            pl.BlockSpec((qb, J * dh), lambda b, g, i, sl: (b * nq + i, g)),
            pl.BlockSpec((1, qb, NSA_BRANCHES * J), lambda b, g, i, sl: (g, b * nq + i, 0)),
            cmp_spec(0), cmp_spec(1),
            slab(1, 0), slab(1, 1), slab(2, 0), slab(2, 1),
        ],
        out_specs=pl.BlockSpec((qb, J * dh), lambda b, g, i, sl: (b * nq + i, g)),
        scratch_shapes=[
            pltpu.VMEM((J * qb, dh + ns), BF16),
            pltpu.VMEM((seq_len, dh + ns), BF16),
            pltpu.VMEM((seq_len, 2 * dh), BF16),
            pltpu.VMEM((J * qb, max(ncp, WINDOW + qb, kt)), BF16),
            pltpu.VMEM((J * qb, dh), F32),
            pltpu.VMEM((J * qb, LANES), F32),
            pltpu.VMEM((J * qb, 2 * dh), F32),
            pltpu.SMEM((seq_len // kt,), jnp.int32),
        ],
    )
    return pl.pallas_call(
        functools.partial(_nsa_kernel, kt=kt),
        grid_spec=grid_spec,
        out_shape=jax.ShapeDtypeStruct((M, G * J * dh), BF16),
        compiler_params=_params("parallel", "parallel", "arbitrary"),
        name="nsa_attention",
    )(slopes, q, gates, kv_cmp, kv_cmp, slabs, slabs, slabs, slabs)


def _ffn_layer(h, g0, g1, w_gu, w_down, layer, conv_w, conv_b, *, seq_len):
    act = ffn_up(h, g0, w_gu, layer, conv_w, conv_b, seq_len=seq_len)
    f = matmul(act, w_down, layer, tm=512, tn=512)
    return residual_norm(h, f, g1)


def _xattn(h, mem_rows, mem_norm, g0, g1, wq, wkv, wo, layer, *, batch, seq_len, pending=None):
    mem_len = mem_rows.shape[0] // batch
    kv = norm_matmul(mem_rows, mem_norm, wkv, layer, tm=mem_rows.shape[0], tn=wkv.shape[2] // 2, out_dtype=BF16)
    kv = kv.reshape(batch, mem_len, wkv.shape[2])
    return xattn_layer(h, g0, wq[layer], kv, wo[layer], g1, seq_len=seq_len, pending=pending)


def _nsa_mixer(h, g0, w_in, w_out, layer, cmp_pos, cmp_w1, cmp_b1, cmp_w2, *, batch, seq_len):
    G, J, dh = NSA_KV_GROUPS, NSA_HPG, HEAD_DIM
    qw = G * J * dh
    kvw = NSA_BRANCHES * 2 * G * dh
    w_gate = w_in[layer, :, qw + kvw:].reshape(-1, NSA_BRANCHES, G, J).transpose(0, 2, 1, 3)
    w_gate = w_gate.reshape(-1, G * NSA_BRANCHES * J)
    q, kv_slabs, gates = nsa_in_proj(h, g0, w_in, layer, w_gate, q_width=qw, kv_width=kvw)
    kv_cmp = compress(kv_slabs, cmp_pos, cmp_w1.astype(BF16), cmp_b1, cmp_w2.astype(BF16),
                      batch=batch, seq_len=seq_len)
    n_heads = G * J
    slopes = 2.0 ** (-8.0 * jnp.arange(1, n_heads + 1, dtype=F32) / n_heads)
    o = nsa_attention(q, gates, kv_cmp, kv_slabs, slopes, batch=batch, seq_len=seq_len)
    return matmul(o, w_out, layer, tm=1024, tn=512)


def kernel(x, mem, ln_mix, ln_xa, ln_ffn, mem_norm, pool_w, pool_scale, nsa_w_in, nsa_w_out, nsa_cmp_pos, nsa_cmp_w1, nsa_cmp_b1, nsa_cmp_w2, xa_wq, xa_wkv, xa_wo, ffn_w_gu, ffn_conv_w, ffn_conv_b, ffn_w_down):
    B, S, D = x.shape
    depth = ln_mix.shape[0]
    h = x.reshape(B * S, D)
    mem_rows = mem.reshape(B * mem.shape[1], D)
    nsa_w_in, nsa_w_out = nsa_w_in.astype(BF16), nsa_w_out.astype(BF16)
    xa_wq, xa_wkv, xa_wo = xa_wq.astype(BF16), xa_wkv.astype(BF16), xa_wo.astype(BF16)
    ffn_w_down = ffn_w_down.astype(BF16)
    for i in range(depth):
        j = i // 2
        pending = None
        if i % 2 == 0:
            h = pool_layer(h, ln_mix[i, 0], pool_w[j].astype(BF16), pool_scale[j], ln_mix[i, 1], seq_len=S)
        else:
            a = _nsa_mixer(h, ln_mix[i, 0], nsa_w_in, nsa_w_out, j, nsa_cmp_pos[j],
                           nsa_cmp_w1[j], nsa_cmp_b1[j], nsa_cmp_w2[j], batch=B, seq_len=S)
            pending = (a, ln_mix[i, 1])
        h = _xattn(h, mem_rows, mem_norm, ln_xa[i, 0], ln_xa[i, 1], xa_wq, xa_wkv, xa_wo, i, batch=B, seq_len=S,
                   pending=pending)
        h = _ffn_layer(h, ln_ffn[i, 0], ln_ffn[i, 1], ffn_w_gu, ffn_w_down, i, ffn_conv_w[i], ffn_conv_b[i],
                       seq_len=S)
    return h.reshape(B, S, D)
```

```python
import functools

import jax
import jax.numpy as jnp
from jax import lax
from jax.experimental import pallas as pl
from jax.experimental.pallas import tpu as pltpu

F32 = jnp.float32
BF16 = jnp.bfloat16

RMS_EPS = 1e-6
NEG_BIG = -1e30
HEAD_DIM = 128
POOL_WINDOWS = (2, 4, 8, 16)
POOL_HALO = 16
NSA_KV_GROUPS = 4
NSA_HPG = 8
NSA_BRANCHES = 3
CMP_LEN = 32
CMP_STRIDE = 16
SEL_BLOCK = 64
N_SELECT = 16
WINDOW = 512
FORCED_SCORE = 1e6
XA_HEADS = 4
CONV_HALO = 16
VMEM_LIMIT = 56 * 1024 * 1024
NORM_CHUNK = 128
LANES = 128

_NT = (((1,), (1,)), ((), ()))


def _params(*sem):
    return pltpu.CompilerParams(dimension_semantics=sem, vmem_limit_bytes=VMEM_LIMIT)


def _rms(x, g):
    ms = jnp.mean(x * x, axis=-1, keepdims=True)
    return x * lax.rsqrt(ms + RMS_EPS) * g


def _norm_rows(h_ref, g_ref, a_scr):
    tm = h_ref.shape[0]
    chunk = min(NORM_CHUNK, tm)

    def body(c, carry):
        r = pl.multiple_of(c * chunk, chunk)
        a_scr[pl.ds(r, chunk), :] = _rms(h_ref[pl.ds(r, chunk), :], g_ref[...]).astype(BF16)
        return carry

    lax.fori_loop(0, tm // chunk, body, 0)


def _once(shape, index_map):
    return pl.BlockSpec(shape, index_map, pipeline_mode=pl.Buffered(1))


def _norm_mm_kernel(h_ref, g_ref, w_ref, o_ref, a_scr):
    @pl.when(pl.program_id(1) == 0)
    def _():
        _norm_rows(h_ref, g_ref, a_scr)

    o_ref[...] = jnp.dot(a_scr[...], w_ref[...], preferred_element_type=F32).astype(o_ref.dtype)


def norm_matmul(h, g, w, layer, *, tm, tn, out_dtype):
    M, D = h.shape
    N = w.shape[2]
    assert M % tm == 0 and N % tn == 0
    return pl.pallas_call(
        _norm_mm_kernel,
        grid=(M // tm, N // tn),
        in_specs=[
            _once((tm, D), lambda i, j: (i, 0)),
            pl.BlockSpec((1, D), lambda i, j: (0, 0)),
            pl.BlockSpec((None, D, tn), lambda i, j: (layer, 0, j)),
        ],
        out_specs=pl.BlockSpec((tm, tn), lambda i, j: (i, j)),
        out_shape=jax.ShapeDtypeStruct((M, N), out_dtype),
        scratch_shapes=[pltpu.VMEM((tm, D), BF16)],
        compiler_params=_params("parallel", "arbitrary"),
        name="norm_matmul",
    )(h, g.reshape(1, D), w)


def _nsa_proj_kernel(h_ref, g_ref, w_ref, wg_ref, q_ref, kv_ref, gate_ref, a_scr, *, nq, nkv, q_scale):
    j = pl.program_id(1)

    @pl.when(j == 0)
    def _():
        _norm_rows(h_ref, g_ref, a_scr)

    @pl.when(j < nq)
    def _():
        acc = jnp.dot(a_scr[...], w_ref[...], preferred_element_type=F32)
        q_ref[...] = (acc * q_scale).astype(q_ref.dtype)

    @pl.when((j >= nq) & (j < nq + nkv))
    def _():
        acc = jnp.dot(a_scr[...], w_ref[...], preferred_element_type=F32)
        for s in range(kv_ref.shape[0]):
            kv_ref[s] = acc[:, s * HEAD_DIM:(s + 1) * HEAD_DIM].astype(kv_ref.dtype)

    @pl.when(j == nq + nkv)
    def _():
        sig = jax.nn.sigmoid(jnp.dot(a_scr[...], wg_ref[...], preferred_element_type=F32))
        per_group = gate_ref.shape[2]
        for grp in range(gate_ref.shape[0]):
            gate_ref[grp] = sig[:, grp * per_group:(grp + 1) * per_group]


def nsa_in_proj(h, g, w_in, layer, w_gate, *, q_width, kv_width, tm=1024, tn=512):
    M, D = h.shape
    G = NSA_KV_GROUPS
    n_gate = w_gate.shape[1]
    assert M % tm == 0 and q_width % tn == 0 and kv_width % tn == 0
    nq, nkv = q_width // tn, kv_width // tn
    spt = tn // HEAD_DIM
    return pl.pallas_call(
        functools.partial(_nsa_proj_kernel, nq=nq, nkv=nkv, q_scale=HEAD_DIM ** -0.5),
        grid=(M // tm, nq + nkv + 1),
        in_specs=[
            _once((tm, D), lambda i, j: (i, 0)),
            pl.BlockSpec((1, D), lambda i, j: (0, 0)),
            pl.BlockSpec((None, D, tn), lambda i, j: (layer, 0, jnp.minimum(j, nq + nkv - 1))),
            pl.BlockSpec((D, n_gate), lambda i, j: (0, 0)),
        ],
        out_specs=[
            pl.BlockSpec((tm, tn), lambda i, j: (i, jnp.minimum(j, nq - 1))),
            pl.BlockSpec((spt, tm, HEAD_DIM), lambda i, j: (jnp.clip(j - nq, 0, nkv - 1), i, 0)),
            pl.BlockSpec((G, tm, n_gate // G), lambda i, j: (0, i, 0)),
        ],
        out_shape=[
            jax.ShapeDtypeStruct((M, q_width), BF16),
            jax.ShapeDtypeStruct((kv_width // HEAD_DIM, M, HEAD_DIM), BF16),
            jax.ShapeDtypeStruct((G, M, n_gate // G), F32),
        ],
        scratch_shapes=[pltpu.VMEM((tm, D), BF16)],
        compiler_params=_params("parallel", "arbitrary"),
        name="nsa_in_proj",
    )(h, g.reshape(1, D), w_in, w_gate)


def _mm_kernel(a_ref, w_ref, o_ref):
    o_ref[...] = jnp.dot(a_ref[...], w_ref[...], preferred_element_type=F32).astype(o_ref.dtype)


def matmul(a, w, layer, *, tm, tn, out_dtype=F32):
    M, K = a.shape
    N = w.shape[2]
    assert M % tm == 0 and N % tn == 0
    return pl.pallas_call(
        _mm_kernel,
        grid=(M // tm, N // tn),
        in_specs=[
            _once((tm, K), lambda i, j: (i, 0)),
            pl.BlockSpec((None, K, tn), lambda i, j: (layer, 0, j)),
        ],
        out_specs=pl.BlockSpec((tm, tn), lambda i, j: (i, j)),
        out_shape=jax.ShapeDtypeStruct((M, N), out_dtype),
        compiler_params=_params("parallel", "arbitrary"),
        name="matmul",
    )(a, w)


def _resid_kernel(h_ref, f_ref, g_ref, o_ref):
    o_ref[...] = h_ref[...] + _rms(f_ref[...], g_ref[...])


def residual_norm(h, f, g, *, tm=256):
    M, D = h.shape
    row = pl.BlockSpec((tm, D), lambda i: (i, 0))
    return pl.pallas_call(
        _resid_kernel,
        grid=(M // tm,),
        in_specs=[row, row, pl.BlockSpec((1, D), lambda i: (0, 0))],
        out_specs=row,
        out_shape=jax.ShapeDtypeStruct((M, D), F32),
        compiler_params=_params("parallel"),
        name="residual_norm",
    )(h, f, g.reshape(1, D))


def _pool_kernel(x_ref, halo_ref, g0_ref, w_ref, sc_ref, g1_ref, o_ref, *, seq_len):
    tm, D = x_ref.shape
    gc = D // len(POOL_WINDOWS)
    t0 = (pl.program_id(0) * tm) % seq_len
    x = x_ref[...]
    a = _rms(x, g0_ref[...])
    ha = _rms(halo_ref[...], g0_ref[...])
    ha = jnp.where(t0 == 0, 0.0, ha)
    full = jnp.concatenate([ha, a], axis=0)
    t1 = (t0 + 1 + lax.broadcasted_iota(jnp.int32, (tm, 1), 0)).astype(F32)
    ys = []
    for gi, win in enumerate(POOL_WINDOWS):
        s = full[:, gi * gc:(gi + 1) * gc]
        k = 1
        while k < win:
            s = s + pltpu.roll(s, k, 0)
            k *= 2
        inv_cnt = 1.0 / jnp.minimum(t1, float(win))
        pooled = s[POOL_HALO:] * inv_cnt
        d = (pooled - a[:, gi * gc:(gi + 1) * gc]).astype(BF16)
        ys.append(jnp.dot(d, w_ref[gi], preferred_element_type=F32))
    y = jnp.concatenate(ys, axis=-1) * sc_ref[...]
    o_ref[...] = x + _rms(y, g1_ref[...])


def pool_layer(x, g0, w, scale, g1, *, seq_len, tm=256):
    M, D = x.shape
    G, gc, _ = w.shape
    vec = pl.BlockSpec((1, D), lambda i: (0, 0))
    hb = tm // POOL_HALO
    return pl.pallas_call(
        functools.partial(_pool_kernel, seq_len=seq_len),
        grid=(M // tm,),
        in_specs=[
            pl.BlockSpec((tm, D), lambda i: (i, 0)),
            pl.BlockSpec((POOL_HALO, D), lambda i: (jnp.maximum(i * hb - 1, 0), 0)),
            vec,
            _once((G, gc, gc), lambda i: (0, 0, 0)),
            vec,
            vec,
        ],
        out_specs=pl.BlockSpec((tm, D), lambda i: (i, 0)),
        out_shape=jax.ShapeDtypeStruct((M, D), F32),
        compiler_params=_params("parallel"),
        name="pool_layer",
    )(x, x, g0.reshape(1, D), w, scale.reshape(1, D), g1.reshape(1, D))


def _xattn_kernel(h_ref, *refs, pending):
    h = h_ref[...]
    if pending:
        a_ref, ga_ref, *refs = refs
        h = h + _rms(a_ref[...], ga_ref[...])
    g0_ref, wq_ref, k_ref, v_ref, wo_ref, g1_ref, gn_ref, o_ref, n_ref = refs
    n = _rms(h, g0_ref[...]).astype(BF16)
    q = jnp.dot(n, wq_ref[...], preferred_element_type=F32).astype(BF16)
    k = k_ref[0]
    v = v_ref[0]
    scale = HEAD_DIM ** -0.5
    outs = []
    for hh in range(XA_HEADS):
        sl = slice(hh * HEAD_DIM, (hh + 1) * HEAD_DIM)
        s = lax.dot_general(q[:, sl], k[:, sl], _NT, preferred_element_type=F32) * scale
        m = jnp.max(s, axis=-1, keepdims=True)
        e = jnp.exp(s - m)
        p = e * (1.0 / jnp.sum(e, axis=-1, keepdims=True))
        outs.append(jnp.dot(p.astype(BF16), v[:, sl], preferred_element_type=F32))
    o = jnp.concatenate(outs, axis=-1).astype(BF16)
    c = jnp.dot(o, wo_ref[...], preferred_element_type=F32)
    h_new = h + _rms(c, g1_ref[...])
    o_ref[...] = h_new
    n_ref[...] = _rms(h_new, gn_ref[...]).astype(n_ref.dtype)


def xattn_layer(h, g0, wq, kv, wo, g1, g_next, *, seq_len, pending=None, tm=256):
    M, D = h.shape
    xw = wq.shape[1]
    mem_len = kv.shape[1]
    per_seq = seq_len // tm
    vec = pl.BlockSpec((1, D), lambda i: (0, 0))
    row = pl.BlockSpec((tm, D), lambda i: (i, 0))
    pend_specs, pend_args = ([row, vec], [pending[0], pending[1].reshape(1, D)]) if pending is not None else ([], [])
    return pl.pallas_call(
        functools.partial(_xattn_kernel, pending=pending is not None),
        grid=(M // tm,),
        in_specs=[
            row,
            *pend_specs,
            vec,
            _once((D, xw), lambda i: (0, 0)),
            pl.BlockSpec((1, mem_len, xw), lambda i: (i // per_seq, 0, 0)),
            pl.BlockSpec((1, mem_len, xw), lambda i: (i // per_seq, 0, 1)),
            _once((xw, D), lambda i: (0, 0)),
            vec,
            vec,
        ],
        out_specs=[row, row],
        out_shape=[jax.ShapeDtypeStruct((M, D), F32), jax.ShapeDtypeStruct((M, D), BF16)],
        compiler_params=_params("parallel"),
        name="xattn_layer",
    )(h, *pend_args, g0.reshape(1, D), wq, kv, kv, wo, g1.reshape(1, D), g_next.reshape(1, D))


def _ffn_up_kernel(n_ref, halo_ref, wg_ref, wu_ref, cw_ref, cb_ref, o_ref, *, seq_len):
    tm = n_ref.shape[0]
    t0 = (pl.program_id(0) * tm) % seq_len
    a = n_ref[...]
    halo = jnp.where(t0 == 0, 0.0, halo_ref[...]).astype(BF16)
    wg = wg_ref[...].astype(BF16)
    gate = jnp.dot(a, wg, preferred_element_type=F32)
    up = jnp.dot(a, wu_ref[...].astype(BF16), preferred_element_type=F32)
    hg = jnp.dot(halo, wg, preferred_element_type=F32)
    row = lax.broadcasted_iota(jnp.int32, (tm, 1), 0)
    prev1 = hg[CONV_HALO - 1:CONV_HALO]
    prev2 = hg[CONV_HALO - 2:CONV_HALO - 1]
    g1 = jnp.where(row == 0, prev1, pltpu.roll(gate, 1, 0))
    g2 = jnp.where(row == 0, prev2, jnp.where(row == 1, prev1, pltpu.roll(gate, 2, 0)))
    cw = cw_ref[...]
    gc = cw[0:1] * g2 + cw[1:2] * g1 + cw[2:3] * gate + cb_ref[...]
    o_ref[...] = (gc * jax.nn.sigmoid(gc) * up).astype(o_ref.dtype)


def ffn_up(n, w_gu, layer, conv_w, conv_b, *, seq_len, tm=2048, tn=256):
    M, D = n.shape
    F = w_gu.shape[2] // 2
    assert F % tn == 0 and M % tm == 0 and seq_len % tm == 0
    nj = F // tn
    hb = tm // CONV_HALO
    return pl.pallas_call(
        functools.partial(_ffn_up_kernel, seq_len=seq_len),
        grid=(M // tm, nj),
        in_specs=[
            _once((tm, D), lambda i, j: (i, 0)),
            _once((CONV_HALO, D), lambda i, j: (jnp.maximum(i * hb - 1, 0), 0)),
            pl.BlockSpec((None, D, tn), lambda i, j: (layer, 0, j)),
            pl.BlockSpec((None, D, tn), lambda i, j: (layer, 0, j + nj)),
            pl.BlockSpec((conv_w.shape[0], tn), lambda i, j: (0, j)),
            pl.BlockSpec((1, tn), lambda i, j: (0, j)),
        ],
        out_specs=pl.BlockSpec((tm, tn), lambda i, j: (i, j)),
        out_shape=jax.ShapeDtypeStruct((M, F), BF16),
        compiler_params=_params("parallel", "arbitrary"),
        name="ffn_up",
    )(n, n, w_gu, w_gu, conv_w, conv_b.reshape(1, F))


def _gelu_tanh(x):
    return 0.5 * x * (1.0 + jnp.tanh(0.7978845608028654 * (x + 0.044715 * x * x * x)))


def _compress_kernel(r_ref, pos_ref, w1_ref, b1_ref, w2_ref, o_ref):
    r = r_ref[0, 0].astype(F32)
    nch, half = r.shape
    pos = pos_ref[0]
    xa = (r + pos[0:1]).astype(BF16)
    xb = (r + pos[1:2]).astype(BF16)
    h1 = jnp.dot(xa, w1_ref[0, :half, :], preferred_element_type=F32)
    h2 = jnp.dot(xb, w1_ref[0, half:, :], preferred_element_type=F32)
    hid = h1 + pltpu.roll(h2, nch - 1, 0) + b1_ref[0]
    out = jnp.dot(_gelu_tanh(hid).astype(BF16), w2_ref[0], preferred_element_type=F32)
    row = lax.broadcasted_iota(jnp.int32, (nch, 1), 0)
    o_ref[0, 0, 0] = jnp.where(row < nch - 1, out, 0.0).astype(o_ref.dtype)


def compress(kv_slabs, pos, w1, b1, w2, *, batch, seq_len):
    G = NSA_KV_GROUPS
    nch = seq_len // CMP_STRIDE
    half = CMP_STRIDE * HEAD_DIM
    r = kv_slabs[:2 * G].reshape(2 * G, batch, nch, half)
    hidden = w1.shape[-1]
    return pl.pallas_call(
        _compress_kernel,
        grid=(2, batch, G),
        in_specs=[
            pl.BlockSpec((1, 1, nch, half), lambda kv, b, g: (kv * G + g, b, 0, 0)),
            pl.BlockSpec((1, 2, half), lambda kv, b, g: (kv, 0, 0)),
            pl.BlockSpec((1, 2 * half, hidden), lambda kv, b, g: (kv, 0, 0)),
            pl.BlockSpec((1, 1, hidden), lambda kv, b, g: (kv, 0, 0)),
            pl.BlockSpec((1, hidden, HEAD_DIM), lambda kv, b, g: (kv, 0, 0)),
        ],
        out_specs=pl.BlockSpec((1, 1, 1, nch, HEAD_DIM), lambda kv, b, g: (kv, b, g, 0, 0)),
        out_shape=jax.ShapeDtypeStruct((2, batch, G, nch, HEAD_DIM), BF16),
        compiler_params=_params("parallel", "parallel", "parallel"),
        name="nsa_compress",
    )(r, pos.reshape(2, 2, half), w1, b1.reshape(2, 1, hidden), w2)


def _dot_row_halves(p_ref, width, rhs):
    half = p_ref.shape[0] // 2
    return jnp.concatenate([jnp.dot(p_ref[:half, :width], rhs, preferred_element_type=F32),
                            jnp.dot(p_ref[half:, :width], rhs, preferred_element_type=F32)], axis=0)


def _nsa_kernel(slope_ref, q_ref, gate_ref, kc_ref, vc_ref, ks_ref, vs_ref, kw_ref, vw_ref, o_ref,
                qa_scr, ka_scr, va_scr, p_scr, oc_scr, mx_scr, acc_scr, used_ref, *, kt):
    J, dh = NSA_HPG, HEAD_DIM
    qb = q_ref.shape[0]
    seq_len = ks_ref.shape[2]
    ncp = kc_ref.shape[3]
    ns = seq_len // SEL_BLOCK
    g = pl.program_id(1)
    qi = pl.program_id(2)
    start = qi * qb

    @pl.when(qi == 0)
    def _():
        chunk = min(512, seq_len)

        def body(c, carry):
            r = pl.multiple_of(c * chunk, chunk)
            ka_scr[pl.ds(r, chunk), :dh] = ks_ref[0, 0, pl.ds(r, chunk), :]
            blk = (r + lax.broadcasted_iota(jnp.int32, (chunk, ns), 0)) // SEL_BLOCK
            ka_scr[pl.ds(r, chunk), dh:] = (blk == lax.broadcasted_iota(jnp.int32, (chunk, ns), 1)).astype(BF16)
            va_scr[pl.ds(r, chunk), :dh] = vs_ref[0, 0, pl.ds(r, chunk), :]
            va_scr[pl.ds(r, chunk), dh:] = jnp.ones((chunk, dh), BF16)
            return carry

        lax.fori_loop(0, seq_len // chunk, body, 0)

    q = q_ref[...]
    for j in range(J):
        qa_scr[j * qb:(j + 1) * qb, :dh] = q[:, j * dh:(j + 1) * dh]
    qr = qa_scr[:, :dh]
    slopes = [slope_ref[g * J + j] for j in range(J)]
    t_col = start + lax.broadcasted_iota(jnp.int32, (qb, 1), 0)

    lanes = mx_scr.shape[1]
    kc = kc_ref[0, 0, 0]
    sc = lax.dot_general(qr, kc, _NT, preferred_element_type=F32)
    c_end = lax.broadcasted_iota(jnp.int32, (1, ncp), 1) * CMP_STRIDE + (CMP_LEN - 1)
    bias_c = jnp.where(t_col >= c_end, 0.0, NEG_BIG)
    rel_c = (c_end - start).astype(F32)
    for j in range(J):
        rows = slice(j * qb, (j + 1) * qb)
        sj = sc[rows] + bias_c + slopes[j] * rel_c
        p_scr[rows, :ncp] = jnp.exp(sj - jnp.max(sj, axis=-1, keepdims=True)).astype(BF16)
    c_row = lax.broadcasted_iota(jnp.int32, (ncp, ns), 0) * CMP_STRIDE
    n_col = lax.broadcasted_iota(jnp.int32, (ncp, ns), 1) * SEL_BLOCK
    overlap = ((c_row < n_col + SEL_BLOCK) & (c_row + CMP_LEN > n_col)
               & (c_row < (ncp - 1) * CMP_STRIDE)).astype(BF16)
    rhs_c = jnp.concatenate([vc_ref[0, 0, 0], jnp.ones((ncp, lanes), BF16), overlap], axis=1)
    oc = _dot_row_halves(p_scr, ncp, rhs_c)
    has_key = t_col >= CMP_LEN - 1
    score = jnp.zeros((qb, ns), F32)
    for j in range(J):
        rows = slice(j * qb, (j + 1) * qb)
        inv = jnp.where(has_key, 1.0 / jnp.maximum(oc[rows, dh:dh + lanes], 1e-30), 0.0)
        oc_scr[rows] = oc[rows, :dh] * inv
        score = score + oc[rows, dh + lanes:] * inv[:, :ns]

    wl = WINDOW + qb
    w0 = pl.multiple_of(jnp.maximum(start - WINDOW, 0), qb)
    kw = kw_ref[0, 0, pl.ds(w0, wl), :]
    vw = jnp.concatenate([vw_ref[0, 0, pl.ds(w0, wl), :], jnp.ones((wl, dh), BF16)], axis=1)
    sw = lax.dot_general(qr, kw, _NT, preferred_element_type=F32)
    pos_w = w0 + lax.broadcasted_iota(jnp.int32, (1, wl), 1)
    bias_w = jnp.where((pos_w <= t_col) & (pos_w > t_col - WINDOW), 0.0, NEG_BIG)
    rel_w = (pos_w - start).astype(F32)
    for j in range(J):
        rows = slice(j * qb, (j + 1) * qb)
        sj = sw[rows] + bias_w + slopes[j] * rel_w
        p_scr[rows, :wl] = jnp.exp(sj - jnp.max(sj, axis=-1, keepdims=True)).astype(BF16)
    ow = _dot_row_halves(p_scr, wl, vw)
    o_win = ow[:, :dh] * (1.0 / jnp.maximum(ow[:, dh:], 1e-30))

    score_t = score.T
    n_idx = lax.broadcasted_iota(jnp.int32, (ns, qb), 0)
    cur = (start + lax.broadcasted_iota(jnp.int32, (1, qb), 1)) // SEL_BLOCK
    forced = (n_idx == 0) | (n_idx == cur) | (n_idx == cur - 1)
    score_t = jnp.where(n_idx > cur, -1.0, jnp.where(forced, FORCED_SCORE, score_t))
    sel_t = jnp.zeros((ns, qb), F32)
    for _ in range(min(N_SELECT, ns)):
        best = jnp.max(score_t, axis=0, keepdims=True)
        first = jnp.min(jnp.where(score_t == best, n_idx, ns), axis=0, keepdims=True)
        hit = n_idx == first
        sel_t = jnp.where(hit, 1.0, sel_t)
        score_t = jnp.where(hit, -2.0, score_t)
    block_bias = jnp.where(sel_t.T > 0.5, 0.0, NEG_BIG).astype(BF16)
    for j in range(J):
        qa_scr[j * qb:(j + 1) * qb, dh:] = block_bias
    bpt = kt // SEL_BLOCK
    for i in range(ns // bpt):
        used_ref[i] = (jnp.max(sel_t[i * bpt:(i + 1) * bpt, :]) > 0.5).astype(jnp.int32)

    pos_d = (start // kt) * kt + lax.broadcasted_iota(jnp.int32, (1, kt), 1)
    causal_bias = jnp.where(pos_d <= t_col, 0.0, NEG_BIG)

    def scores(i, j, s, diagonal):
        pos = i * kt + lax.broadcasted_iota(jnp.int32, (1, kt), 1)
        sj = s[j * qb:(j + 1) * qb] + slopes[j] * (pos - start).astype(F32)
        if diagonal:
            sj = sj + causal_bias
        return sj

    def max_tile(i, diagonal):
        k0 = pl.multiple_of(i * kt, kt)
        s = lax.dot_general(qa_scr[...], ka_scr[pl.ds(k0, kt), :], _NT, preferred_element_type=F32)
        for j in range(J):
            rows = slice(j * qb, (j + 1) * qb)
            sj = scores(i, j, s, diagonal)
            mx = mx_scr[rows]
            for c in range(kt // lanes):
                mx = jnp.maximum(mx, sj[:, c * lanes:(c + 1) * lanes])
            mx_scr[rows] = mx

    def pv_tile(i, diagonal):
        k0 = pl.multiple_of(i * kt, kt)
        s = lax.dot_general(qa_scr[...], ka_scr[pl.ds(k0, kt), :], _NT, preferred_element_type=F32)
        for j in range(J):
            rows = slice(j * qb, (j + 1) * qb)
            sj = scores(i, j, s, diagonal)
            m = mx_scr[rows]
            for c in range(kt // lanes):
                p_scr[rows, c * lanes:(c + 1) * lanes] = jnp.exp(sj[:, c * lanes:(c + 1) * lanes] - m).astype(BF16)
        acc_scr[...] += _dot_row_halves(p_scr, kt, va_scr[pl.ds(k0, kt), :])

    def run_pass(tile_fn):
        def full(i, carry):
            @pl.when(used_ref[i] > 0)
            def _():
                tile_fn(i, False)

            return carry

        n_full = start // kt
        lax.fori_loop(0, n_full, full, 0)
        tile_fn(n_full, True)

    mx_scr[...] = jnp.full(mx_scr.shape, NEG_BIG, F32)
    run_pass(max_tile)
    for j in range(J):
        rows = slice(j * qb, (j + 1) * qb)
        mx_scr[rows] = jnp.broadcast_to(jnp.max(mx_scr[rows], axis=-1, keepdims=True), (qb, lanes))
    acc_scr[...] = jnp.zeros(acc_scr.shape, F32)
    run_pass(pv_tile)

    gate = gate_ref[0]
    for j in range(J):
        rows = slice(j * qb, (j + 1) * qb)
        o_sel = acc_scr[rows, :dh] * (1.0 / jnp.maximum(acc_scr[rows, dh:], 1e-30))
        o = (gate[:, j:j + 1] * oc_scr[rows] + gate[:, J + j:J + j + 1] * o_sel
             + gate[:, 2 * J + j:2 * J + j + 1] * o_win[rows])
        o_ref[:, j * dh:(j + 1) * dh] = o.astype(o_ref.dtype)


def nsa_attention(q, gates, kv_cmp, kv_slabs, slopes, *, batch, seq_len, qb=256, kt=512):
    G, J, dh = NSA_KV_GROUPS, NSA_HPG, HEAD_DIM
    M = q.shape[0]
    nq = seq_len // qb
    ncp = kv_cmp.shape[3]
    ns = seq_len // SEL_BLOCK
    assert seq_len % kt == 0 and kt % qb == 0 and seq_len >= WINDOW + qb
    slabs = kv_slabs.reshape(kv_slabs.shape[0], batch, seq_len, dh)

    def slab(branch, is_v):
        base = (branch * 2 + is_v) * G
        return pl.BlockSpec((1, 1, seq_len, dh), lambda b, g, i, sl: (base + g, b, 0, 0))

    def cmp_spec(is_v):
        return pl.BlockSpec((1, 1, 1, ncp, dh), lambda b, g, i, sl: (is_v, b, g, 0, 0))

    grid_spec = pltpu.PrefetchScalarGridSpec(
        num_scalar_prefetch=1,
        grid=(batch, G, nq),
        in_specs=[
            pl.BlockSpec((qb, J * dh), lambda b, g, i, sl: (b * nq + i, g)),
            pl.BlockSpec((1, qb, NSA_BRANCHES * J), lambda b, g, i, sl: (g, b * nq + i, 0)),
            cmp_spec(0), cmp_spec(1),
            slab(1, 0), slab(1, 1), slab(2, 0), slab(2, 1),
        ],
        out_specs=pl.BlockSpec((qb, J * dh), lambda b, g, i, sl: (b * nq + i, g)),
        scratch_shapes=[
            pltpu.VMEM((J * qb, dh + ns), BF16),
            pltpu.VMEM((seq_len, dh + ns), BF16),
            pltpu.VMEM((seq_len, 2 * dh), BF16),
            pltpu.VMEM((J * qb, max(ncp, WINDOW + qb, kt)), BF16),
            pltpu.VMEM((J * qb, dh), F32),
            pltpu.VMEM((J * qb, LANES), F32),
            pltpu.VMEM((J * qb, 2 * dh), F32),
            pltpu.SMEM((seq_len // kt,), jnp.int32),
        ],
    )
    return pl.pallas_call(
        functools.partial(_nsa_kernel, kt=kt),
        grid_spec=grid_spec,
        out_shape=jax.ShapeDtypeStruct((M, G * J * dh), BF16),
        compiler_params=_params("parallel", "parallel", "arbitrary"),
        name="nsa_attention",
    )(slopes, q, gates, kv_cmp, kv_cmp, slabs, slabs, slabs, slabs)


def _ffn_layer(h, n, g1, w_gu, w_down, layer, conv_w, conv_b, *, seq_len):
    act = ffn_up(n, w_gu, layer, conv_w, conv_b, seq_len=seq_len)
    f = matmul(act, w_down, layer, tm=512, tn=512)
    return residual_norm(h, f, g1)


def _xattn(h, mem_rows, mem_norm, g0, g1, g_next, wq, wkv, wo, layer, *, batch, seq_len, pending=None):
    mem_len = mem_rows.shape[0] // batch
    kv = norm_matmul(mem_rows, mem_norm, wkv, layer, tm=mem_rows.shape[0], tn=wkv.shape[2] // 2, out_dtype=BF16)
    kv = kv.reshape(batch, mem_len, wkv.shape[2])
    return xattn_layer(h, g0, wq[layer], kv, wo[layer], g1, g_next, seq_len=seq_len, pending=pending)


def _nsa_mixer(h, g0, w_in, w_out, layer, cmp_pos, cmp_w1, cmp_b1, cmp_w2, *, batch, seq_len):
    G, J, dh = NSA_KV_GROUPS, NSA_HPG, HEAD_DIM
    qw = G * J * dh
    kvw = NSA_BRANCHES * 2 * G * dh
    w_gate = w_in[layer, :, qw + kvw:].reshape(-1, NSA_BRANCHES, G, J).transpose(0, 2, 1, 3)
    w_gate = w_gate.reshape(-1, G * NSA_BRANCHES * J)
    q, kv_slabs, gates = nsa_in_proj(h, g0, w_in, layer, w_gate, q_width=qw, kv_width=kvw)
    kv_cmp = compress(kv_slabs, cmp_pos, cmp_w1.astype(BF16), cmp_b1, cmp_w2.astype(BF16),
                      batch=batch, seq_len=seq_len)
    n_heads = G * J
    slopes = 2.0 ** (-8.0 * jnp.arange(1, n_heads + 1, dtype=F32) / n_heads)
    o = nsa_attention(q, gates, kv_cmp, kv_slabs, slopes, batch=batch, seq_len=seq_len)
    return matmul(o, w_out, layer, tm=1024, tn=512)


def kernel(x, mem, ln_mix, ln_xa, ln_ffn, mem_norm, pool_w, pool_scale, nsa_w_in, nsa_w_out, nsa_cmp_pos, nsa_cmp_w1, nsa_cmp_b1, nsa_cmp_w2, xa_wq, xa_wkv, xa_wo, ffn_w_gu, ffn_conv_w, ffn_conv_b, ffn_w_down):
    B, S, D = x.shape
    depth = ln_mix.shape[0]
    h = x.reshape(B * S, D)
    mem_rows = mem.reshape(B * mem.shape[1], D)
    nsa_w_in, nsa_w_out = nsa_w_in.astype(BF16), nsa_w_out.astype(BF16)
    xa_wq, xa_wkv, xa_wo = xa_wq.astype(BF16), xa_wkv.astype(BF16), xa_wo.astype(BF16)
    ffn_w_down = ffn_w_down.astype(BF16)
    for i in range(depth):
        j = i // 2
        pending = None
        if i % 2 == 0:
            h = pool_layer(h, ln_mix[i, 0], pool_w[j].astype(BF16), pool_scale[j], ln_mix[i, 1], seq_len=S)
        else:
            a = _nsa_mixer(h, ln_mix[i, 0], nsa_w_in, nsa_w_out, j, nsa_cmp_pos[j],
                           nsa_cmp_w1[j], nsa_cmp_b1[j], nsa_cmp_w2[j], batch=B, seq_len=S)
            pending = (a, ln_mix[i, 1])
        h, n = _xattn(h, mem_rows, mem_norm, ln_xa[i, 0], ln_xa[i, 1], ln_ffn[i, 0], xa_wq, xa_wkv, xa_wo, i,
                      batch=B, seq_len=S, pending=pending)
        h = _ffn_layer(h, n, ln_ffn[i, 1], ffn_w_gu, ffn_w_down, i, ffn_conv_w[i], ffn_conv_b[i], seq_len=S)
    return h.reshape(B, S, D)
```

```python
import functools

import jax
import jax.numpy as jnp
from jax import lax
from jax.experimental import pallas as pl
from jax.experimental.pallas import tpu as pltpu

F32 = jnp.float32
BF16 = jnp.bfloat16

RMS_EPS = 1e-6
NEG_BIG = -1e30
HEAD_DIM = 128
POOL_WINDOWS = (2, 4, 8, 16)
POOL_HALO = 16
NSA_KV_GROUPS = 4
NSA_HPG = 8
NSA_BRANCHES = 3
CMP_LEN = 32
CMP_STRIDE = 16
SEL_BLOCK = 64
N_SELECT = 16
WINDOW = 512
FORCED_SCORE = 1e6
SEL_EXCESS_LIMIT = 30.0
XA_HEADS = 4
CONV_HALO = 16
VMEM_LIMIT = 56 * 1024 * 1024
NORM_CHUNK = 128
LANES = 128

_NT = (((1,), (1,)), ((), ()))


def _params(*sem):
    return pltpu.CompilerParams(dimension_semantics=sem, vmem_limit_bytes=VMEM_LIMIT)


def _rms(x, g):
    ms = jnp.mean(x * x, axis=-1, keepdims=True)
    return x * lax.rsqrt(ms + RMS_EPS) * g


def _norm_rows(h_ref, g_ref, a_scr):
    tm = h_ref.shape[0]
    chunk = min(NORM_CHUNK, tm)

    def body(c, carry):
        r = pl.multiple_of(c * chunk, chunk)
        a_scr[pl.ds(r, chunk), :] = _rms(h_ref[pl.ds(r, chunk), :], g_ref[...]).astype(BF16)
        return carry

    lax.fori_loop(0, tm // chunk, body, 0)


def _once(shape, index_map):
    return pl.BlockSpec(shape, index_map, pipeline_mode=pl.Buffered(1))


def _norm_mm_kernel(h_ref, g_ref, w_ref, o_ref, a_scr):
    @pl.when(pl.program_id(1) == 0)
    def _():
        _norm_rows(h_ref, g_ref, a_scr)

    o_ref[...] = jnp.dot(a_scr[...], w_ref[...], preferred_element_type=F32).astype(o_ref.dtype)


def norm_matmul(h, g, w, layer, *, tm, tn, out_dtype):
    M, D = h.shape
    N = w.shape[2]
    assert M % tm == 0 and N % tn == 0
    return pl.pallas_call(
        _norm_mm_kernel,
        grid=(M // tm, N // tn),
        in_specs=[
            _once((tm, D), lambda i, j: (i, 0)),
            pl.BlockSpec((1, D), lambda i, j: (0, 0)),
            pl.BlockSpec((None, D, tn), lambda i, j: (layer, 0, j)),
        ],
        out_specs=pl.BlockSpec((tm, tn), lambda i, j: (i, j)),
        out_shape=jax.ShapeDtypeStruct((M, N), out_dtype),
        scratch_shapes=[pltpu.VMEM((tm, D), BF16)],
        compiler_params=_params("parallel", "arbitrary"),
        name="norm_matmul",
    )(h, g.reshape(1, D), w)


def _nsa_proj_kernel(h_ref, g_ref, w_ref, wg_ref, q_ref, kv_ref, gate_ref, a_scr, *, nq, nkv, q_scale):
    j = pl.program_id(1)

    @pl.when(j == 0)
    def _():
        _norm_rows(h_ref, g_ref, a_scr)

    @pl.when(j < nq)
    def _():
        acc = jnp.dot(a_scr[...], w_ref[...], preferred_element_type=F32)
        q_ref[...] = (acc * q_scale).astype(q_ref.dtype)

    @pl.when((j >= nq) & (j < nq + nkv))
    def _():
        acc = jnp.dot(a_scr[...], w_ref[...], preferred_element_type=F32)
        for s in range(kv_ref.shape[0]):
            kv_ref[s] = acc[:, s * HEAD_DIM:(s + 1) * HEAD_DIM].astype(kv_ref.dtype)

    @pl.when(j == nq + nkv)
    def _():
        sig = jax.nn.sigmoid(jnp.dot(a_scr[...], wg_ref[...], preferred_element_type=F32))
        per_group = gate_ref.shape[2]
        for grp in range(gate_ref.shape[0]):
            gate_ref[grp] = sig[:, grp * per_group:(grp + 1) * per_group]


def nsa_in_proj(h, g, w_in, layer, w_gate, *, q_width, kv_width, tm=1024, tn=512):
    M, D = h.shape
    G = NSA_KV_GROUPS
    n_gate = w_gate.shape[1]
    assert M % tm == 0 and q_width % tn == 0 and kv_width % tn == 0
    nq, nkv = q_width // tn, kv_width // tn
    spt = tn // HEAD_DIM
    return pl.pallas_call(
        functools.partial(_nsa_proj_kernel, nq=nq, nkv=nkv, q_scale=HEAD_DIM ** -0.5),
        grid=(M // tm, nq + nkv + 1),
        in_specs=[
            _once((tm, D), lambda i, j: (i, 0)),
            pl.BlockSpec((1, D), lambda i, j: (0, 0)),
            pl.BlockSpec((None, D, tn), lambda i, j: (layer, 0, jnp.minimum(j, nq + nkv - 1))),
            pl.BlockSpec((D, n_gate), lambda i, j: (0, 0)),
        ],
        out_specs=[
            pl.BlockSpec((tm, tn), lambda i, j: (i, jnp.minimum(j, nq - 1))),
            pl.BlockSpec((spt, tm, HEAD_DIM), lambda i, j: (jnp.clip(j - nq, 0, nkv - 1), i, 0)),
            pl.BlockSpec((G, tm, n_gate // G), lambda i, j: (0, i, 0)),
        ],
        out_shape=[
            jax.ShapeDtypeStruct((M, q_width), BF16),
            jax.ShapeDtypeStruct((kv_width // HEAD_DIM, M, HEAD_DIM), BF16),
            jax.ShapeDtypeStruct((G, M, n_gate // G), F32),
        ],
        scratch_shapes=[pltpu.VMEM((tm, D), BF16)],
        compiler_params=_params("parallel", "arbitrary"),
        name="nsa_in_proj",
    )(h, g.reshape(1, D), w_in, w_gate)


def _mm_kernel(a_ref, w_ref, o_ref):
    o_ref[...] = jnp.dot(a_ref[...], w_ref[...], preferred_element_type=F32).astype(o_ref.dtype)


def matmul(a, w, layer, *, tm, tn, out_dtype=F32):
    M, K = a.shape
    N = w.shape[2]
    assert M % tm == 0 and N % tn == 0
    return pl.pallas_call(
        _mm_kernel,
        grid=(M // tm, N // tn),
        in_specs=[
            _once((tm, K), lambda i, j: (i, 0)),
            pl.BlockSpec((None, K, tn), lambda i, j: (layer, 0, j)),
        ],
        out_specs=pl.BlockSpec((tm, tn), lambda i, j: (i, j)),
        out_shape=jax.ShapeDtypeStruct((M, N), out_dtype),
        compiler_params=_params("parallel", "arbitrary"),
        name="matmul",
    )(a, w)


def _resid_kernel(h_ref, f_ref, g_ref, o_ref):
    o_ref[...] = h_ref[...] + _rms(f_ref[...], g_ref[...])


def residual_norm(h, f, g, *, tm=256):
    M, D = h.shape
    row = pl.BlockSpec((tm, D), lambda i: (i, 0))
    return pl.pallas_call(
        _resid_kernel,
        grid=(M // tm,),
        in_specs=[row, row, pl.BlockSpec((1, D), lambda i: (0, 0))],
        out_specs=row,
        out_shape=jax.ShapeDtypeStruct((M, D), F32),
        compiler_params=_params("parallel"),
        name="residual_norm",
    )(h, f, g.reshape(1, D))


def _pool_kernel(x_ref, halo_ref, g0_ref, w_ref, sc_ref, g1_ref, o_ref, *, seq_len):
    tm, D = x_ref.shape
    gc = D // len(POOL_WINDOWS)
    t0 = (pl.program_id(0) * tm) % seq_len
    x = x_ref[...]
    a = _rms(x, g0_ref[...])
    ha = _rms(halo_ref[...], g0_ref[...])
    ha = jnp.where(t0 == 0, 0.0, ha)
    full = jnp.concatenate([ha, a], axis=0)
    t1 = (t0 + 1 + lax.broadcasted_iota(jnp.int32, (tm, 1), 0)).astype(F32)
    ys = []
    for gi, win in enumerate(POOL_WINDOWS):
        s = full[:, gi * gc:(gi + 1) * gc]
        k = 1
        while k < win:
            s = s + pltpu.roll(s, k, 0)
            k *= 2
        inv_cnt = 1.0 / jnp.minimum(t1, float(win))
        pooled = s[POOL_HALO:] * inv_cnt
        d = (pooled - a[:, gi * gc:(gi + 1) * gc]).astype(BF16)
        ys.append(jnp.dot(d, w_ref[gi], preferred_element_type=F32))
    y = jnp.concatenate(ys, axis=-1) * sc_ref[...]
    o_ref[...] = x + _rms(y, g1_ref[...])


def pool_layer(x, g0, w, scale, g1, *, seq_len, tm=256):
    M, D = x.shape
    G, gc, _ = w.shape
    vec = pl.BlockSpec((1, D), lambda i: (0, 0))
    hb = tm // POOL_HALO
    return pl.pallas_call(
        functools.partial(_pool_kernel, seq_len=seq_len),
        grid=(M // tm,),
        in_specs=[
            pl.BlockSpec((tm, D), lambda i: (i, 0)),
            pl.BlockSpec((POOL_HALO, D), lambda i: (jnp.maximum(i * hb - 1, 0), 0)),
            vec,
            _once((G, gc, gc), lambda i: (0, 0, 0)),
            vec,
            vec,
        ],
        out_specs=pl.BlockSpec((tm, D), lambda i: (i, 0)),
        out_shape=jax.ShapeDtypeStruct((M, D), F32),
        compiler_params=_params("parallel"),
        name="pool_layer",
    )(x, x, g0.reshape(1, D), w, scale.reshape(1, D), g1.reshape(1, D))


def _xattn_kernel(h_ref, *refs, pending):
    h = h_ref[...]
    if pending:
        a_ref, ga_ref, *refs = refs
        h = h + _rms(a_ref[...], ga_ref[...])
    g0_ref, wq_ref, k_ref, v_ref, wo_ref, g1_ref, gn_ref, o_ref, n_ref = refs
    n = _rms(h, g0_ref[...]).astype(BF16)
    q = jnp.dot(n, wq_ref[...], preferred_element_type=F32).astype(BF16)
    k = k_ref[0]
    v = v_ref[0]
    scale = HEAD_DIM ** -0.5
    outs = []
    for hh in range(XA_HEADS):
        sl = slice(hh * HEAD_DIM, (hh + 1) * HEAD_DIM)
        s = lax.dot_general(q[:, sl], k[:, sl], _NT, preferred_element_type=F32) * scale
        m = jnp.max(s, axis=-1, keepdims=True)
        e = jnp.exp(s - m)
        p = e * (1.0 / jnp.sum(e, axis=-1, keepdims=True))
        outs.append(jnp.dot(p.astype(BF16), v[:, sl], preferred_element_type=F32))
    o = jnp.concatenate(outs, axis=-1).astype(BF16)
    c = jnp.dot(o, wo_ref[...], preferred_element_type=F32)
    h_new = h + _rms(c, g1_ref[...])
    o_ref[...] = h_new
    n_ref[...] = _rms(h_new, gn_ref[...]).astype(n_ref.dtype)


def xattn_layer(h, g0, wq, kv, wo, g1, g_next, *, seq_len, pending=None, tm=256):
    M, D = h.shape
    xw = wq.shape[1]
    mem_len = kv.shape[1]
    per_seq = seq_len // tm
    vec = pl.BlockSpec((1, D), lambda i: (0, 0))
    row = pl.BlockSpec((tm, D), lambda i: (i, 0))
    pend_specs, pend_args = ([row, vec], [pending[0], pending[1].reshape(1, D)]) if pending is not None else ([], [])
    return pl.pallas_call(
        functools.partial(_xattn_kernel, pending=pending is not None),
        grid=(M // tm,),
        in_specs=[
            row,
            *pend_specs,
            vec,
            _once((D, xw), lambda i: (0, 0)),
            pl.BlockSpec((1, mem_len, xw), lambda i: (i // per_seq, 0, 0)),
            pl.BlockSpec((1, mem_len, xw), lambda i: (i // per_seq, 0, 1)),
            _once((xw, D), lambda i: (0, 0)),
            vec,
            vec,
        ],
        out_specs=[row, row],
        out_shape=[jax.ShapeDtypeStruct((M, D), F32), jax.ShapeDtypeStruct((M, D), BF16)],
        compiler_params=_params("parallel"),
        name="xattn_layer",
    )(h, *pend_args, g0.reshape(1, D), wq, kv, kv, wo, g1.reshape(1, D), g_next.reshape(1, D))


def _ffn_up_kernel(n_ref, halo_ref, wg_ref, wu_ref, cw_ref, cb_ref, o_ref, *, seq_len):
    tm = n_ref.shape[0]
    t0 = (pl.program_id(0) * tm) % seq_len
    a = n_ref[...]
    halo = jnp.where(t0 == 0, 0.0, halo_ref[...]).astype(BF16)
    wg = wg_ref[...].astype(BF16)
    gate = jnp.dot(a, wg, preferred_element_type=F32)
    up = jnp.dot(a, wu_ref[...].astype(BF16), preferred_element_type=F32)
    hg = jnp.dot(halo, wg, preferred_element_type=F32)
    row = lax.broadcasted_iota(jnp.int32, (tm, 1), 0)
    prev1 = hg[CONV_HALO - 1:CONV_HALO]
    prev2 = hg[CONV_HALO - 2:CONV_HALO - 1]
    g1 = jnp.where(row == 0, prev1, pltpu.roll(gate, 1, 0))
    g2 = jnp.where(row == 0, prev2, jnp.where(row == 1, prev1, pltpu.roll(gate, 2, 0)))
    cw = cw_ref[...]
    gc = cw[0:1] * g2 + cw[1:2] * g1 + cw[2:3] * gate + cb_ref[...]
    o_ref[...] = (gc * jax.nn.sigmoid(gc) * up).astype(o_ref.dtype)


def ffn_up(n, w_gu, layer, conv_w, conv_b, *, seq_len, tm=2048, tn=256):
    M, D = n.shape
    F = w_gu.shape[2] // 2
    assert F % tn == 0 and M % tm == 0 and seq_len % tm == 0
    nj = F // tn
    hb = tm // CONV_HALO
    return pl.pallas_call(
        functools.partial(_ffn_up_kernel, seq_len=seq_len),
        grid=(M // tm, nj),
        in_specs=[
            _once((tm, D), lambda i, j: (i, 0)),
            _once((CONV_HALO, D), lambda i, j: (jnp.maximum(i * hb - 1, 0), 0)),
            pl.BlockSpec((None, D, tn), lambda i, j: (layer, 0, j)),
            pl.BlockSpec((None, D, tn), lambda i, j: (layer, 0, j + nj)),
            pl.BlockSpec((conv_w.shape[0], tn), lambda i, j: (0, j)),
            pl.BlockSpec((1, tn), lambda i, j: (0, j)),
        ],
        out_specs=pl.BlockSpec((tm, tn), lambda i, j: (i, j)),
        out_shape=jax.ShapeDtypeStruct((M, F), BF16),
        compiler_params=_params("parallel", "arbitrary"),
        name="ffn_up",
    )(n, n, w_gu, w_gu, conv_w, conv_b.reshape(1, F))


def _gelu_tanh(x):
    return 0.5 * x * (1.0 + jnp.tanh(0.7978845608028654 * (x + 0.044715 * x * x * x)))


def _compress_kernel(r_ref, pos_ref, w1_ref, b1_ref, w2_ref, o_ref):
    r = r_ref[0, 0].astype(F32)
    nch, half = r.shape
    pos = pos_ref[0]
    xa = (r + pos[0:1]).astype(BF16)
    xb = (r + pos[1:2]).astype(BF16)
    h1 = jnp.dot(xa, w1_ref[0, :half, :], preferred_element_type=F32)
    h2 = jnp.dot(xb, w1_ref[0, half:, :], preferred_element_type=F32)
    hid = h1 + pltpu.roll(h2, nch - 1, 0) + b1_ref[0]
    out = jnp.dot(_gelu_tanh(hid).astype(BF16), w2_ref[0], preferred_element_type=F32)
    row = lax.broadcasted_iota(jnp.int32, (nch, 1), 0)
    o_ref[0, 0, 0] = jnp.where(row < nch - 1, out, 0.0).astype(o_ref.dtype)


def compress(kv_slabs, pos, w1, b1, w2, *, batch, seq_len):
    G = NSA_KV_GROUPS
    nch = seq_len // CMP_STRIDE
    half = CMP_STRIDE * HEAD_DIM
    r = kv_slabs[:2 * G].reshape(2 * G, batch, nch, half)
    hidden = w1.shape[-1]
    return pl.pallas_call(
        _compress_kernel,
        grid=(2, batch, G),
        in_specs=[
            pl.BlockSpec((1, 1, nch, half), lambda kv, b, g: (kv * G + g, b, 0, 0)),
            pl.BlockSpec((1, 2, half), lambda kv, b, g: (kv, 0, 0)),
            pl.BlockSpec((1, 2 * half, hidden), lambda kv, b, g: (kv, 0, 0)),
            pl.BlockSpec((1, 1, hidden), lambda kv, b, g: (kv, 0, 0)),
            pl.BlockSpec((1, hidden, HEAD_DIM), lambda kv, b, g: (kv, 0, 0)),
        ],
        out_specs=pl.BlockSpec((1, 1, 1, nch, HEAD_DIM), lambda kv, b, g: (kv, b, g, 0, 0)),
        out_shape=jax.ShapeDtypeStruct((2, batch, G, nch, HEAD_DIM), BF16),
        compiler_params=_params("parallel", "parallel", "parallel"),
        name="nsa_compress",
    )(r, pos.reshape(2, 2, half), w1, b1.reshape(2, 1, hidden), w2)


def _dot_row_halves(p_ref, width, rhs):
    half = p_ref.shape[0] // 2
    return jnp.concatenate([jnp.dot(p_ref[:half, :width], rhs, preferred_element_type=F32),
                            jnp.dot(p_ref[half:, :width], rhs, preferred_element_type=F32)], axis=0)


def _nsa_kernel(slope_ref, q_ref, gate_ref, kc_ref, vc_ref, ks_ref, vs_ref, kw_ref, vw_ref, o_ref,
                qa_scr, ka_scr, va_scr, p_scr, oc_scr, mx_scr, trk_scr, acc_scr, used_ref, *, kt):
    J, dh = NSA_HPG, HEAD_DIM
    qb = q_ref.shape[0]
    seq_len = ks_ref.shape[2]
    ncp = kc_ref.shape[3]
    ns = seq_len // SEL_BLOCK
    g = pl.program_id(1)
    qi = pl.program_id(2)
    start = qi * qb

    @pl.when(qi == 0)
    def _():
        chunk = min(512, seq_len)

        def body(c, carry):
            r = pl.multiple_of(c * chunk, chunk)
            ka_scr[pl.ds(r, chunk), :dh] = ks_ref[0, 0, pl.ds(r, chunk), :]
            blk = (r + lax.broadcasted_iota(jnp.int32, (chunk, ns), 0)) // SEL_BLOCK
            ka_scr[pl.ds(r, chunk), dh:] = (blk == lax.broadcasted_iota(jnp.int32, (chunk, ns), 1)).astype(BF16)
            va_scr[pl.ds(r, chunk), :dh] = vs_ref[0, 0, pl.ds(r, chunk), :]
            va_scr[pl.ds(r, chunk), dh:] = jnp.ones((chunk, dh), BF16)
            return carry

        lax.fori_loop(0, seq_len // chunk, body, 0)

    q = q_ref[...]
    for j in range(J):
        qa_scr[j * qb:(j + 1) * qb, :dh] = q[:, j * dh:(j + 1) * dh]
    qr = qa_scr[:, :dh]
    slopes = [slope_ref[g * J + j] for j in range(J)]
    t_col = start + lax.broadcasted_iota(jnp.int32, (qb, 1), 0)

    lanes = mx_scr.shape[1]
    kc = kc_ref[0, 0, 0]
    sc = lax.dot_general(qr, kc, _NT, preferred_element_type=F32)
    c_end = lax.broadcasted_iota(jnp.int32, (1, ncp), 1) * CMP_STRIDE + (CMP_LEN - 1)
    bias_c = jnp.where(t_col >= c_end, 0.0, NEG_BIG)
    rel_c = (c_end - start).astype(F32)
    for j in range(J):
        rows = slice(j * qb, (j + 1) * qb)
        sj = sc[rows] + bias_c + slopes[j] * rel_c
        p_scr[rows, :ncp] = jnp.exp(sj - jnp.max(sj, axis=-1, keepdims=True)).astype(BF16)
    c_row = lax.broadcasted_iota(jnp.int32, (ncp, ns), 0) * CMP_STRIDE
    n_col = lax.broadcasted_iota(jnp.int32, (ncp, ns), 1) * SEL_BLOCK
    overlap = ((c_row < n_col + SEL_BLOCK) & (c_row + CMP_LEN > n_col)
               & (c_row < (ncp - 1) * CMP_STRIDE)).astype(BF16)
    rhs_c = jnp.concatenate([vc_ref[0, 0, 0], jnp.ones((ncp, lanes), BF16), overlap], axis=1)
    oc = _dot_row_halves(p_scr, ncp, rhs_c)
    has_key = t_col >= CMP_LEN - 1
    score = jnp.zeros((qb, ns), F32)
    for j in range(J):
        rows = slice(j * qb, (j + 1) * qb)
        inv = jnp.where(has_key, 1.0 / jnp.maximum(oc[rows, dh:dh + lanes], 1e-30), 0.0)
        oc_scr[rows] = oc[rows, :dh] * inv
        score = score + oc[rows, dh + lanes:] * inv[:, :ns]

    wl = WINDOW + qb
    w0 = pl.multiple_of(jnp.maximum(start - WINDOW, 0), qb)
    kw = kw_ref[0, 0, pl.ds(w0, wl), :]
    vw = jnp.concatenate([vw_ref[0, 0, pl.ds(w0, wl), :], jnp.ones((wl, dh), BF16)], axis=1)
    sw = lax.dot_general(qr, kw, _NT, preferred_element_type=F32)
    pos_w = w0 + lax.broadcasted_iota(jnp.int32, (1, wl), 1)
    bias_w = jnp.where((pos_w <= t_col) & (pos_w > t_col - WINDOW), 0.0, NEG_BIG)
    rel_w = (pos_w - start).astype(F32)
    for j in range(J):
        rows = slice(j * qb, (j + 1) * qb)
        sj = sw[rows] + bias_w + slopes[j] * rel_w
        p_scr[rows, :wl] = jnp.exp(sj - jnp.max(sj, axis=-1, keepdims=True)).astype(BF16)
    ow = _dot_row_halves(p_scr, wl, vw)
    o_win = ow[:, :dh] * (1.0 / jnp.maximum(ow[:, dh:], 1e-30))

    score_t = score.T
    n_idx = lax.broadcasted_iota(jnp.int32, (ns, qb), 0)
    cur = (start + lax.broadcasted_iota(jnp.int32, (1, qb), 1)) // SEL_BLOCK
    forced = (n_idx == 0) | (n_idx == cur) | (n_idx == cur - 1)
    score_t = jnp.where(n_idx > cur, -1.0, jnp.where(forced, FORCED_SCORE, score_t))
    sel_t = jnp.zeros((ns, qb), F32)
    for _ in range(min(N_SELECT, ns)):
        best = jnp.max(score_t, axis=0, keepdims=True)
        first = jnp.min(jnp.where(score_t == best, n_idx, ns), axis=0, keepdims=True)
        hit = n_idx == first
        sel_t = jnp.where(hit, 1.0, sel_t)
        score_t = jnp.where(hit, -2.0, score_t)
    block_bias = jnp.where(sel_t.T > 0.5, 0.0, NEG_BIG).astype(BF16)
    for j in range(J):
        qa_scr[j * qb:(j + 1) * qb, dh:] = block_bias
    bpt = kt // SEL_BLOCK
    for i in range(ns // bpt):
        used_ref[i] = (jnp.max(sel_t[i * bpt:(i + 1) * bpt, :]) > 0.5).astype(jnp.int32)

    pos_d = (start // kt) * kt + lax.broadcasted_iota(jnp.int32, (1, kt), 1)
    causal_bias = jnp.where(pos_d <= t_col, 0.0, NEG_BIG)

    def scores(i, j, s, diagonal):
        pos = i * kt + lax.broadcasted_iota(jnp.int32, (1, kt), 1)
        sj = s[j * qb:(j + 1) * qb] + slopes[j] * (pos - start).astype(F32)
        if diagonal:
            sj = sj + causal_bias
        return sj

    def max_tile(i, diagonal):
        k0 = pl.multiple_of(i * kt, kt)
        s = lax.dot_general(qa_scr[...], ka_scr[pl.ds(k0, kt), :], _NT, preferred_element_type=F32)
        for j in range(J):
            rows = slice(j * qb, (j + 1) * qb)
            sj = scores(i, j, s, diagonal)
            mx = mx_scr[rows]
            for c in range(kt // lanes):
                mx = jnp.maximum(mx, sj[:, c * lanes:(c + 1) * lanes])
            mx_scr[rows] = mx

    def pv_tile(i, diagonal, track=False):
        k0 = pl.multiple_of(i * kt, kt)
        s = lax.dot_general(qa_scr[...], ka_scr[pl.ds(k0, kt), :], _NT, preferred_element_type=F32)
        for j in range(J):
            rows = slice(j * qb, (j + 1) * qb)
            sj = scores(i, j, s, diagonal)
            m = mx_scr[rows]
            excess = trk_scr[rows] if track else None
            for c in range(kt // lanes):
                d = sj[:, c * lanes:(c + 1) * lanes] - m
                p_scr[rows, c * lanes:(c + 1) * lanes] = jnp.exp(d).astype(BF16)
                if track:
                    excess = jnp.maximum(excess, d)
            if track:
                trk_scr[rows] = excess
        acc_scr[...] += _dot_row_halves(p_scr, kt, va_scr[pl.ds(k0, kt), :])

    n_full = start // kt

    def over_used_tiles(tile_fn):
        def full(i, carry):
            @pl.when(used_ref[i] > 0)
            def _():
                tile_fn(i)

            return carry

        lax.fori_loop(0, n_full, full, 0)

    def reduce_row_max():
        for j in range(J):
            rows = slice(j * qb, (j + 1) * qb)
            mx_scr[rows] = jnp.broadcast_to(jnp.max(mx_scr[rows], axis=-1, keepdims=True), (qb, lanes))

    mx_scr[...] = jnp.full(mx_scr.shape, NEG_BIG, F32)
    max_tile(n_full, True)
    reduce_row_max()
    trk_scr[...] = jnp.full(trk_scr.shape, NEG_BIG, F32)
    acc_scr[...] = jnp.zeros(acc_scr.shape, F32)
    over_used_tiles(lambda i: pv_tile(i, False, track=True))
    pv_tile(n_full, True)

    @pl.when(jnp.max(trk_scr[...]) > SEL_EXCESS_LIMIT)
    def _():
        over_used_tiles(lambda i: max_tile(i, False))
        reduce_row_max()
        acc_scr[...] = jnp.zeros(acc_scr.shape, F32)
        over_used_tiles(lambda i: pv_tile(i, False))
        pv_tile(n_full, True)

    gate = gate_ref[0]
    for j in range(J):
        rows = slice(j * qb, (j + 1) * qb)
        o_sel = acc_scr[rows, :dh] * (1.0 / jnp.maximum(acc_scr[rows, dh:], 1e-30))
        o = (gate[:, j:j + 1] * oc_scr[rows] + gate[:, J + j:J + j + 1] * o_sel
             + gate[:, 2 * J + j:2 * J + j + 1] * o_win[rows])
        o_ref[:, j * dh:(j + 1) * dh] = o.astype(o_ref.dtype)


def nsa_attention(q, gates, kv_cmp, kv_slabs, slopes, *, batch, seq_len, qb=256, kt=512):
    G, J, dh = NSA_KV_GROUPS, NSA_HPG, HEAD_DIM
    M = q.shape[0]
    nq = seq_len // qb
    ncp = kv_cmp.shape[3]
    ns = seq_len // SEL_BLOCK
    assert seq_len % kt == 0 and kt % qb == 0 and seq_len >= WINDOW + qb
    slabs = kv_slabs.reshape(kv_slabs.shape[0], batch, seq_len, dh)

    def slab(branch, is_v):
        base = (branch * 2 + is_v) * G
        return pl.BlockSpec((1, 1, seq_len, dh), lambda b, g, i, sl: (base + g, b, 0, 0))

    def cmp_spec(is_v):
        return pl.BlockSpec((1, 1, 1, ncp, dh), lambda b, g, i, sl: (is_v, b, g, 0, 0))

    grid_spec = pltpu.PrefetchScalarGridSpec(
        num_scalar_prefetch=1,
        grid=(batch, G, nq),
        in_specs=[
            pl.BlockSpec((qb, J * dh), lambda b, g, i, sl: (b * nq + i, g)),
            pl.BlockSpec((1, qb, NSA_BRANCHES * J), lambda b, g, i, sl: (g, b * nq + i, 0)),
            cmp_spec(0), cmp_spec(1),
            slab(1, 0), slab(1, 1), slab(2, 0), slab(2, 1),
        ],
        out_specs=pl.BlockSpec((qb, J * dh), lambda b, g, i, sl: (b * nq + i, g)),
        scratch_shapes=[
            pltpu.VMEM((J * qb, dh + ns), BF16),
            pltpu.VMEM((seq_len, dh + ns), BF16),
            pltpu.VMEM((seq_len, 2 * dh), BF16),
            pltpu.VMEM((J * qb, max(ncp, WINDOW + qb, kt)), BF16),
            pltpu.VMEM((J * qb, dh), F32),
            pltpu.VMEM((J * qb, LANES), F32),
            pltpu.VMEM((J * qb, LANES), F32),
            pltpu.VMEM((J * qb, 2 * dh), F32),
            pltpu.SMEM((seq_len // kt,), jnp.int32),
        ],
    )
    return pl.pallas_call(
        functools.partial(_nsa_kernel, kt=kt),
        grid_spec=grid_spec,
        out_shape=jax.ShapeDtypeStruct((M, G * J * dh), BF16),
        compiler_params=_params("parallel", "parallel", "arbitrary"),
        name="nsa_attention",
    )(slopes, q, gates, kv_cmp, kv_cmp, slabs, slabs, slabs, slabs)


def _ffn_layer(h, n, g1, w_gu, w_down, layer, conv_w, conv_b, *, seq_len):
    act = ffn_up(n, w_gu, layer, conv_w, conv_b, seq_len=seq_len)
    f = matmul(act, w_down, layer, tm=512, tn=512)
    return residual_norm(h, f, g1)


def _xattn(h, mem_rows, mem_norm, g0, g1, g_next, wq, wkv, wo, layer, *, batch, seq_len, pending=None):
    mem_len = mem_rows.shape[0] // batch
    kv = norm_matmul(mem_rows, mem_norm, wkv, layer, tm=mem_rows.shape[0], tn=wkv.shape[2] // 2, out_dtype=BF16)
    kv = kv.reshape(batch, mem_len, wkv.shape[2])
    return xattn_layer(h, g0, wq[layer], kv, wo[layer], g1, g_next, seq_len=seq_len, pending=pending)


def _nsa_mixer(h, g0, w_in, w_out, layer, cmp_pos, cmp_w1, cmp_b1, cmp_w2, *, batch, seq_len):
    G, J, dh = NSA_KV_GROUPS, NSA_HPG, HEAD_DIM
    qw = G * J * dh
    kvw = NSA_BRANCHES * 2 * G * dh
    w_gate = w_in[layer, :, qw + kvw:].reshape(-1, NSA_BRANCHES, G, J).transpose(0, 2, 1, 3)
    w_gate = w_gate.reshape(-1, G * NSA_BRANCHES * J)
    q, kv_slabs, gates = nsa_in_proj(h, g0, w_in, layer, w_gate, q_width=qw, kv_width=kvw)
    kv_cmp = compress(kv_slabs, cmp_pos, cmp_w1.astype(BF16), cmp_b1, cmp_w2.astype(BF16),
                      batch=batch, seq_len=seq_len)
    n_heads = G * J
    slopes = 2.0 ** (-8.0 * jnp.arange(1, n_heads + 1, dtype=F32) / n_heads)
    o = nsa_attention(q, gates, kv_cmp, kv_slabs, slopes, batch=batch, seq_len=seq_len)
    return matmul(o, w_out, layer, tm=1024, tn=512)


def kernel(x, mem, ln_mix, ln_xa, ln_ffn, mem_norm, pool_w, pool_scale, nsa_w_in, nsa_w_out, nsa_cmp_pos, nsa_cmp_w1, nsa_cmp_b1, nsa_cmp_w2, xa_wq, xa_wkv, xa_wo, ffn_w_gu, ffn_conv_w, ffn_conv_b, ffn_w_down):
    B, S, D = x.shape
    depth = ln_mix.shape[0]
    h = x.reshape(B * S, D)
    mem_rows = mem.reshape(B * mem.shape[1], D)
    nsa_w_in, nsa_w_out = nsa_w_in.astype(BF16), nsa_w_out.astype(BF16)
    xa_wq, xa_wkv, xa_wo = xa_wq.astype(BF16), xa_wkv.astype(BF16), xa_wo.astype(BF16)
    ffn_w_down = ffn_w_down.astype(BF16)
    for i in range(depth):
        j = i // 2
        pending = None
        if i % 2 == 0:
            h = pool_layer(h, ln_mix[i, 0], pool_w[j].astype(BF16), pool_scale[j], ln_mix[i, 1], seq_len=S)
        else:
            a = _nsa_mixer(h, ln_mix[i, 0], nsa_w_in, nsa_w_out, j, nsa_cmp_pos[j],
                           nsa_cmp_w1[j], nsa_cmp_b1[j], nsa_cmp_w2[j], batch=B, seq_len=S)
            pending = (a, ln_mix[i, 1])
        h, n = _xattn(h, mem_rows, mem_norm, ln_xa[i, 0], ln_xa[i, 1], ln_ffn[i, 0], xa_wq, xa_wkv, xa_wo, i,
                      batch=B, seq_len=S, pending=pending)
        h = _ffn_layer(h, n, ln_ffn[i, 1], ffn_w_gu, ffn_w_down, i, ffn_conv_w[i], ffn_conv_b[i], seq_len=S)
    return h.reshape(B, S, D)
```

```python
import functools

import jax
import jax.numpy as jnp
from jax import lax
from jax.experimental import pallas as pl
from jax.experimental.pallas import tpu as pltpu

F32 = jnp.float32
BF16 = jnp.bfloat16

RMS_EPS = 1e-6
NEG_BIG = -1e30
HEAD_DIM = 128
POOL_WINDOWS = (2, 4, 8, 16)
POOL_HALO = 16
NSA_KV_GROUPS = 4
NSA_HPG = 8
NSA_BRANCHES = 3
CMP_LEN = 32
CMP_STRIDE = 16
SEL_BLOCK = 64
N_SELECT = 16
WINDOW = 512
FORCED_SCORE = 1e6
SEL_EXCESS_LIMIT = 30.0
XA_HEADS = 4
CONV_HALO = 16
VMEM_LIMIT = 56 * 1024 * 1024
NORM_CHUNK = 128
LANES = 128

_NT = (((1,), (1,)), ((), ()))


def _params(*sem):
    return pltpu.CompilerParams(dimension_semantics=sem, vmem_limit_bytes=VMEM_LIMIT)


def _rms(x, g):
    ms = jnp.mean(x * x, axis=-1, keepdims=True)
    return x * lax.rsqrt(ms + RMS_EPS) * g


def _norm_rows(h_ref, g_ref, a_scr):
    tm = h_ref.shape[0]
    chunk = min(NORM_CHUNK, tm)

    def body(c, carry):
        r = pl.multiple_of(c * chunk, chunk)
        a_scr[pl.ds(r, chunk), :] = _rms(h_ref[pl.ds(r, chunk), :], g_ref[...]).astype(BF16)
        return carry

    lax.fori_loop(0, tm // chunk, body, 0)


def _once(shape, index_map):
    return pl.BlockSpec(shape, index_map, pipeline_mode=pl.Buffered(1))


def _norm_mm_kernel(h_ref, g_ref, w_ref, o_ref, a_scr):
    @pl.when(pl.program_id(1) == 0)
    def _():
        _norm_rows(h_ref, g_ref, a_scr)

    o_ref[...] = jnp.dot(a_scr[...], w_ref[...], preferred_element_type=F32).astype(o_ref.dtype)


def norm_matmul(h, g, w, layer, *, tm, tn, out_dtype):
    M, D = h.shape
    N = w.shape[2]
    assert M % tm == 0 and N % tn == 0
    return pl.pallas_call(
        _norm_mm_kernel,
        grid=(M // tm, N // tn),
        in_specs=[
            _once((tm, D), lambda i, j: (i, 0)),
            pl.BlockSpec((1, D), lambda i, j: (0, 0)),
            pl.BlockSpec((None, D, tn), lambda i, j: (layer, 0, j)),
        ],
        out_specs=pl.BlockSpec((tm, tn), lambda i, j: (i, j)),
        out_shape=jax.ShapeDtypeStruct((M, N), out_dtype),
        scratch_shapes=[pltpu.VMEM((tm, D), BF16)],
        compiler_params=_params("parallel", "arbitrary"),
        name="norm_matmul",
    )(h, g.reshape(1, D), w)


def _nsa_proj_kernel(h_ref, g_ref, w_ref, wg_ref, q_ref, kv_ref, gate_ref, a_scr, *, nq, nkv, q_scale):
    j = pl.program_id(1)

    @pl.when(j == 0)
    def _():
        _norm_rows(h_ref, g_ref, a_scr)

    @pl.when(j < nq)
    def _():
        acc = jnp.dot(a_scr[...], w_ref[...], preferred_element_type=F32)
        q_ref[...] = (acc * q_scale).astype(q_ref.dtype)

    @pl.when((j >= nq) & (j < nq + nkv))
    def _():
        acc = jnp.dot(a_scr[...], w_ref[...], preferred_element_type=F32)
        for s in range(kv_ref.shape[0]):
            kv_ref[s] = acc[:, s * HEAD_DIM:(s + 1) * HEAD_DIM].astype(kv_ref.dtype)

    @pl.when(j == nq + nkv)
    def _():
        sig = jax.nn.sigmoid(jnp.dot(a_scr[...], wg_ref[...], preferred_element_type=F32))
        per_group = gate_ref.shape[2]
        for grp in range(gate_ref.shape[0]):
            gate_ref[grp] = sig[:, grp * per_group:(grp + 1) * per_group]


def nsa_in_proj(h, g, w_in, layer, w_gate, *, q_width, kv_width, tm=1024, tn=512):
    M, D = h.shape
    G = NSA_KV_GROUPS
    n_gate = w_gate.shape[1]
    assert M % tm == 0 and q_width % tn == 0 and kv_width % tn == 0
    nq, nkv = q_width // tn, kv_width // tn
    spt = tn // HEAD_DIM
    return pl.pallas_call(
        functools.partial(_nsa_proj_kernel, nq=nq, nkv=nkv, q_scale=HEAD_DIM ** -0.5),
        grid=(M // tm, nq + nkv + 1),
        in_specs=[
            _once((tm, D), lambda i, j: (i, 0)),
            pl.BlockSpec((1, D), lambda i, j: (0, 0)),
            pl.BlockSpec((None, D, tn), lambda i, j: (layer, 0, jnp.minimum(j, nq + nkv - 1))),
            pl.BlockSpec((D, n_gate), lambda i, j: (0, 0)),
        ],
        out_specs=[
            pl.BlockSpec((tm, tn), lambda i, j: (i, jnp.minimum(j, nq - 1))),
            pl.BlockSpec((spt, tm, HEAD_DIM), lambda i, j: (jnp.clip(j - nq, 0, nkv - 1), i, 0)),
            pl.BlockSpec((G, tm, n_gate // G), lambda i, j: (0, i, 0)),
        ],
        out_shape=[
            jax.ShapeDtypeStruct((M, q_width), BF16),
            jax.ShapeDtypeStruct((kv_width // HEAD_DIM, M, HEAD_DIM), BF16),
            jax.ShapeDtypeStruct((G, M, n_gate // G), F32),
        ],
        scratch_shapes=[pltpu.VMEM((tm, D), BF16)],
        compiler_params=_params("parallel", "arbitrary"),
        name="nsa_in_proj",
    )(h, g.reshape(1, D), w_in, w_gate)


def _mm_kernel(a_ref, w_ref, o_ref):
    o_ref[...] = jnp.dot(a_ref[...], w_ref[...], preferred_element_type=F32).astype(o_ref.dtype)


def matmul(a, w, layer, *, tm, tn, out_dtype=F32):
    M, K = a.shape
    N = w.shape[2]
    assert M % tm == 0 and N % tn == 0
    return pl.pallas_call(
        _mm_kernel,
        grid=(M // tm, N // tn),
        in_specs=[
            _once((tm, K), lambda i, j: (i, 0)),
            pl.BlockSpec((None, K, tn), lambda i, j: (layer, 0, j)),
        ],
        out_specs=pl.BlockSpec((tm, tn), lambda i, j: (i, j)),
        out_shape=jax.ShapeDtypeStruct((M, N), out_dtype),
        compiler_params=_params("parallel", "arbitrary"),
        name="matmul",
    )(a, w)


def _resid_kernel(h_ref, f_ref, g_ref, o_ref):
    o_ref[...] = h_ref[...] + _rms(f_ref[...], g_ref[...])


def residual_norm(h, f, g, *, tm=256):
    M, D = h.shape
    row = pl.BlockSpec((tm, D), lambda i: (i, 0))
    return pl.pallas_call(
        _resid_kernel,
        grid=(M // tm,),
        in_specs=[row, row, pl.BlockSpec((1, D), lambda i: (0, 0))],
        out_specs=row,
        out_shape=jax.ShapeDtypeStruct((M, D), F32),
        compiler_params=_params("parallel"),
        name="residual_norm",
    )(h, f, g.reshape(1, D))


def _pool_kernel(x_ref, halo_ref, g0_ref, w_ref, sc_ref, g1_ref, o_ref, *, seq_len):
    tm, D = x_ref.shape
    gc = D // len(POOL_WINDOWS)
    t0 = (pl.program_id(0) * tm) % seq_len
    x = x_ref[...]
    a = _rms(x, g0_ref[...])
    ha = _rms(halo_ref[...], g0_ref[...])
    ha = jnp.where(t0 == 0, 0.0, ha)
    full = jnp.concatenate([ha, a], axis=0)
    t1 = (t0 + 1 + lax.broadcasted_iota(jnp.int32, (tm, 1), 0)).astype(F32)
    ys = []
    for gi, win in enumerate(POOL_WINDOWS):
        s = full[:, gi * gc:(gi + 1) * gc]
        k = 1
        while k < win:
            s = s + pltpu.roll(s, k, 0)
            k *= 2
        inv_cnt = 1.0 / jnp.minimum(t1, float(win))
        pooled = s[POOL_HALO:] * inv_cnt
        d = (pooled - a[:, gi * gc:(gi + 1) * gc]).astype(BF16)
        ys.append(jnp.dot(d, w_ref[gi], preferred_element_type=F32))
    y = jnp.concatenate(ys, axis=-1) * sc_ref[...]
    o_ref[...] = x + _rms(y, g1_ref[...])


def pool_layer(x, g0, w, scale, g1, *, seq_len, tm=256):
    M, D = x.shape
    G, gc, _ = w.shape
    vec = pl.BlockSpec((1, D), lambda i: (0, 0))
    hb = tm // POOL_HALO
    return pl.pallas_call(
        functools.partial(_pool_kernel, seq_len=seq_len),
        grid=(M // tm,),
        in_specs=[
            pl.BlockSpec((tm, D), lambda i: (i, 0)),
            pl.BlockSpec((POOL_HALO, D), lambda i: (jnp.maximum(i * hb - 1, 0), 0)),
            vec,
            _once((G, gc, gc), lambda i: (0, 0, 0)),
            vec,
            vec,
        ],
        out_specs=pl.BlockSpec((tm, D), lambda i: (i, 0)),
        out_shape=jax.ShapeDtypeStruct((M, D), F32),
        compiler_params=_params("parallel"),
        name="pool_layer",
    )(x, x, g0.reshape(1, D), w, scale.reshape(1, D), g1.reshape(1, D))


def _xattn_kernel(h_ref, *refs, pending):
    h = h_ref[...]
    if pending:
        a_ref, ga_ref, *refs = refs
        h = h + _rms(a_ref[...], ga_ref[...])
    g0_ref, wq_ref, k_ref, v_ref, wo_ref, g1_ref, gn_ref, o_ref, n_ref = refs
    n = _rms(h, g0_ref[...]).astype(BF16)
    q = jnp.dot(n, wq_ref[...], preferred_element_type=F32).astype(BF16)
    k = k_ref[0]
    v = v_ref[0]
    scale = HEAD_DIM ** -0.5
    outs = []
    for hh in range(XA_HEADS):
        sl = slice(hh * HEAD_DIM, (hh + 1) * HEAD_DIM)
        s = lax.dot_general(q[:, sl], k[:, sl], _NT, preferred_element_type=F32) * scale
        m = jnp.max(s, axis=-1, keepdims=True)
        e = jnp.exp(s - m)
        p = e * (1.0 / jnp.sum(e, axis=-1, keepdims=True))
        outs.append(jnp.dot(p.astype(BF16), v[:, sl], preferred_element_type=F32))
    o = jnp.concatenate(outs, axis=-1).astype(BF16)
    c = jnp.dot(o, wo_ref[...], preferred_element_type=F32)
    h_new = h + _rms(c, g1_ref[...])
    o_ref[...] = h_new
    n_ref[...] = _rms(h_new, gn_ref[...]).astype(n_ref.dtype)


def xattn_layer(h, g0, wq, kv, wo, g1, g_next, *, seq_len, pending=None, tm=256):
    M, D = h.shape
    xw = wq.shape[1]
    mem_len = kv.shape[1]
    per_seq = seq_len // tm
    vec = pl.BlockSpec((1, D), lambda i: (0, 0))
    row = pl.BlockSpec((tm, D), lambda i: (i, 0))
    pend_specs, pend_args = ([row, vec], [pending[0], pending[1].reshape(1, D)]) if pending is not None else ([], [])
    return pl.pallas_call(
        functools.partial(_xattn_kernel, pending=pending is not None),
        grid=(M // tm,),
        in_specs=[
            row,
            *pend_specs,
            vec,
            _once((D, xw), lambda i: (0, 0)),
            pl.BlockSpec((1, mem_len, xw), lambda i: (i // per_seq, 0, 0)),
            pl.BlockSpec((1, mem_len, xw), lambda i: (i // per_seq, 0, 1)),
            _once((xw, D), lambda i: (0, 0)),
            vec,
            vec,
        ],
        out_specs=[row, row],
        out_shape=[jax.ShapeDtypeStruct((M, D), F32), jax.ShapeDtypeStruct((M, D), BF16)],
        compiler_params=_params("parallel"),
        name="xattn_layer",
    )(h, *pend_args, g0.reshape(1, D), wq, kv, kv, wo, g1.reshape(1, D), g_next.reshape(1, D))


def _ffn_up_kernel(n_ref, halo_ref, wg_ref, wu_ref, cw_ref, cb_ref, o_ref, *, seq_len):
    tm = n_ref.shape[0]
    t0 = (pl.program_id(0) * tm) % seq_len
    a = n_ref[...]
    halo = jnp.where(t0 == 0, 0.0, halo_ref[...]).astype(BF16)
    wg = wg_ref[...].astype(BF16)
    gate = jnp.dot(a, wg, preferred_element_type=F32)
    up = jnp.dot(a, wu_ref[...].astype(BF16), preferred_element_type=F32)
    hg = jnp.dot(halo, wg, preferred_element_type=F32)
    row = lax.broadcasted_iota(jnp.int32, (tm, 1), 0)
    prev1 = hg[CONV_HALO - 1:CONV_HALO]
    prev2 = hg[CONV_HALO - 2:CONV_HALO - 1]
    g1 = jnp.where(row == 0, prev1, pltpu.roll(gate, 1, 0))
    g2 = jnp.where(row == 0, prev2, jnp.where(row == 1, prev1, pltpu.roll(gate, 2, 0)))
    cw = cw_ref[...]
    gc = cw[0:1] * g2 + cw[1:2] * g1 + cw[2:3] * gate + cb_ref[...]
    o_ref[...] = (gc * jax.nn.sigmoid(gc) * up).astype(o_ref.dtype)


def ffn_up(n, w_gu, layer, conv_w, conv_b, *, seq_len, tm=2048, tn=256):
    M, D = n.shape
    F = w_gu.shape[2] // 2
    assert F % tn == 0 and M % tm == 0 and seq_len % tm == 0
    nj = F // tn
    hb = tm // CONV_HALO
    return pl.pallas_call(
        functools.partial(_ffn_up_kernel, seq_len=seq_len),
        grid=(M // tm, nj),
        in_specs=[
            _once((tm, D), lambda i, j: (i, 0)),
            _once((CONV_HALO, D), lambda i, j: (jnp.maximum(i * hb - 1, 0), 0)),
            pl.BlockSpec((None, D, tn), lambda i, j: (layer, 0, j)),
            pl.BlockSpec((None, D, tn), lambda i, j: (layer, 0, j + nj)),
            pl.BlockSpec((conv_w.shape[0], tn), lambda i, j: (0, j)),
            pl.BlockSpec((1, tn), lambda i, j: (0, j)),
        ],
        out_specs=pl.BlockSpec((tm, tn), lambda i, j: (i, j)),
        out_shape=jax.ShapeDtypeStruct((M, F), BF16),
        compiler_params=_params("parallel", "arbitrary"),
        name="ffn_up",
    )(n, n, w_gu, w_gu, conv_w, conv_b.reshape(1, F))


def _gelu_tanh(x):
    return 0.5 * x * (1.0 + jnp.tanh(0.7978845608028654 * (x + 0.044715 * x * x * x)))


def _compress_kernel(r_ref, pos_ref, w1_ref, b1_ref, w2_ref, o_ref):
    r = r_ref[0, 0].astype(F32)
    nch, half = r.shape
    pos = pos_ref[0]
    xa = (r + pos[0:1]).astype(BF16)
    xb = (r + pos[1:2]).astype(BF16)
    h1 = jnp.dot(xa, w1_ref[0, :half, :], preferred_element_type=F32)
    h2 = jnp.dot(xb, w1_ref[0, half:, :], preferred_element_type=F32)
    hid = h1 + pltpu.roll(h2, nch - 1, 0) + b1_ref[0]
    out = jnp.dot(_gelu_tanh(hid).astype(BF16), w2_ref[0], preferred_element_type=F32)
    row = lax.broadcasted_iota(jnp.int32, (nch, 1), 0)
    o_ref[0, 0, 0] = jnp.where(row < nch - 1, out, 0.0).astype(o_ref.dtype)


def compress(kv_slabs, pos, w1, b1, w2, *, batch, seq_len):
    G = NSA_KV_GROUPS
    nch = seq_len // CMP_STRIDE
    half = CMP_STRIDE * HEAD_DIM
    r = kv_slabs[:2 * G].reshape(2 * G, batch, nch, half)
    hidden = w1.shape[-1]
    return pl.pallas_call(
        _compress_kernel,
        grid=(2, batch, G),
        in_specs=[
            pl.BlockSpec((1, 1, nch, half), lambda kv, b, g: (kv * G + g, b, 0, 0)),
            pl.BlockSpec((1, 2, half), lambda kv, b, g: (kv, 0, 0)),
            pl.BlockSpec((1, 2 * half, hidden), lambda kv, b, g: (kv, 0, 0)),
            pl.BlockSpec((1, 1, hidden), lambda kv, b, g: (kv, 0, 0)),
            pl.BlockSpec((1, hidden, HEAD_DIM), lambda kv, b, g: (kv, 0, 0)),
        ],
        out_specs=pl.BlockSpec((1, 1, 1, nch, HEAD_DIM), lambda kv, b, g: (kv, b, g, 0, 0)),
        out_shape=jax.ShapeDtypeStruct((2, batch, G, nch, HEAD_DIM), BF16),
        compiler_params=_params("parallel", "parallel", "parallel"),
        name="nsa_compress",
    )(r, pos.reshape(2, 2, half), w1, b1.reshape(2, 1, hidden), w2)


def _dot_row_halves(p_ref, width, rhs):
    half = p_ref.shape[0] // 2
    return jnp.concatenate([jnp.dot(p_ref[:half, :width], rhs, preferred_element_type=F32),
                            jnp.dot(p_ref[half:, :width], rhs, preferred_element_type=F32)], axis=0)


def _nsa_kernel(slope_ref, q_ref, gate_ref, kc_ref, vc_ref, ks_ref, vs_ref, kw_ref, vw_ref, o_ref,
                qa_scr, ka_scr, va_scr, p_scr, oc_scr, mx_scr, trk_scr, acc_scr, used_ref, *, kt):
    J, dh = NSA_HPG, HEAD_DIM
    qb = q_ref.shape[0]
    seq_len = ks_ref.shape[2]
    ncp = kc_ref.shape[3]
    ns = seq_len // SEL_BLOCK
    g = pl.program_id(1)
    qi = pl.program_id(2)
    start = qi * qb

    @pl.when(qi == 0)
    def _():
        chunk = min(512, seq_len)

        def body(c, carry):
            r = pl.multiple_of(c * chunk, chunk)
            ka_scr[pl.ds(r, chunk), :dh] = ks_ref[0, 0, pl.ds(r, chunk), :]
            blk = (r + lax.broadcasted_iota(jnp.int32, (chunk, ns), 0)) // SEL_BLOCK
            ka_scr[pl.ds(r, chunk), dh:] = (blk == lax.broadcasted_iota(jnp.int32, (chunk, ns), 1)).astype(BF16)
            va_scr[pl.ds(r, chunk), :dh] = vs_ref[0, 0, pl.ds(r, chunk), :]
            va_scr[pl.ds(r, chunk), dh:] = jnp.ones((chunk, dh), BF16)
            return carry

        lax.fori_loop(0, seq_len // chunk, body, 0)

    q = q_ref[...]
    for j in range(J):
        qa_scr[j * qb:(j + 1) * qb, :dh] = q[:, j * dh:(j + 1) * dh]
    qr = qa_scr[:, :dh]
    slopes = [slope_ref[g * J + j] for j in range(J)]
    t_col = start + lax.broadcasted_iota(jnp.int32, (qb, 1), 0)

    lanes = mx_scr.shape[1]
    kc = kc_ref[0, 0, 0]
    sc = lax.dot_general(qr, kc, _NT, preferred_element_type=F32)
    c_end = lax.broadcasted_iota(jnp.int32, (1, ncp), 1) * CMP_STRIDE + (CMP_LEN - 1)
    bias_c = jnp.where(t_col >= c_end, 0.0, NEG_BIG)
    rel_c = (c_end - start).astype(F32)
    for j in range(J):
        rows = slice(j * qb, (j + 1) * qb)
        sj = sc[rows] + bias_c + slopes[j] * rel_c
        p_scr[rows, :ncp] = jnp.exp(sj - jnp.max(sj, axis=-1, keepdims=True)).astype(BF16)
    c_row = lax.broadcasted_iota(jnp.int32, (ncp, ns), 0) * CMP_STRIDE
    n_col = lax.broadcasted_iota(jnp.int32, (ncp, ns), 1) * SEL_BLOCK
    overlap = ((c_row < n_col + SEL_BLOCK) & (c_row + CMP_LEN > n_col)
               & (c_row < (ncp - 1) * CMP_STRIDE)).astype(BF16)
    rhs_c = jnp.concatenate([vc_ref[0, 0, 0], jnp.ones((ncp, lanes), BF16), overlap], axis=1)
    oc = _dot_row_halves(p_scr, ncp, rhs_c)
    has_key = t_col >= CMP_LEN - 1
    score = jnp.zeros((qb, ns), F32)
    for j in range(J):
        rows = slice(j * qb, (j + 1) * qb)
        inv = jnp.where(has_key, 1.0 / jnp.maximum(oc[rows, dh:dh + lanes], 1e-30), 0.0)
        oc_scr[rows] = oc[rows, :dh] * inv
        score = score + oc[rows, dh + lanes:] * inv[:, :ns]

    wl = WINDOW + qb
    w0 = pl.multiple_of(jnp.maximum(start - WINDOW, 0), qb)
    kw = kw_ref[0, 0, pl.ds(w0, wl), :]
    vw = jnp.concatenate([vw_ref[0, 0, pl.ds(w0, wl), :], jnp.ones((wl, dh), BF16)], axis=1)
    sw = lax.dot_general(qr, kw, _NT, preferred_element_type=F32)
    pos_w = w0 + lax.broadcasted_iota(jnp.int32, (1, wl), 1)
    bias_w = jnp.where((pos_w <= t_col) & (pos_w > t_col - WINDOW), 0.0, NEG_BIG)
    rel_w = (pos_w - start).astype(F32)
    for j in range(J):
        rows = slice(j * qb, (j + 1) * qb)
        sj = sw[rows] + bias_w + slopes[j] * rel_w
        p_scr[rows, :wl] = jnp.exp(sj - jnp.max(sj, axis=-1, keepdims=True)).astype(BF16)
    ow = _dot_row_halves(p_scr, wl, vw)
    o_win = ow[:, :dh] * (1.0 / jnp.maximum(ow[:, dh:], 1e-30))

    score_t = score.T
    n_idx = lax.broadcasted_iota(jnp.int32, (ns, qb), 0)
    cur = (start + lax.broadcasted_iota(jnp.int32, (1, qb), 1)) // SEL_BLOCK
    forced = (n_idx == 0) | (n_idx == cur) | (n_idx == cur - 1)
    score_t = jnp.where(n_idx > cur, -1.0, jnp.where(forced, FORCED_SCORE, score_t))
    sel_t = jnp.zeros((ns, qb), F32)
    for _ in range(min(N_SELECT, ns)):
        best = jnp.max(score_t, axis=0, keepdims=True)
        first = jnp.min(jnp.where(score_t == best, n_idx, ns), axis=0, keepdims=True)
        hit = n_idx == first
        sel_t = jnp.where(hit, 1.0, sel_t)
        score_t = jnp.where(hit, -2.0, score_t)
    block_bias = jnp.where(sel_t.T > 0.5, 0.0, NEG_BIG).astype(BF16)
    for j in range(J):
        qa_scr[j * qb:(j + 1) * qb, dh:] = block_bias
    bpt = kt // SEL_BLOCK
    for i in range(ns // bpt):
        used_ref[i] = (jnp.max(sel_t[i * bpt:(i + 1) * bpt, :]) > 0.5).astype(jnp.int32)

    pos_d = (start // kt) * kt + lax.broadcasted_iota(jnp.int32, (1, kt), 1)
    causal_bias = jnp.where(pos_d <= t_col, 0.0, NEG_BIG)

    def scores(i, j, s, diagonal):
        pos = i * kt + lax.broadcasted_iota(jnp.int32, (1, kt), 1)
        sj = s[j * qb:(j + 1) * qb] + slopes[j] * (pos - start).astype(F32)
        if diagonal:
            sj = sj + causal_bias
        return sj

    def max_tile(i, diagonal):
        k0 = pl.multiple_of(i * kt, kt)
        s = lax.dot_general(qa_scr[...], ka_scr[pl.ds(k0, kt), :], _NT, preferred_element_type=F32)
        for j in range(J):
            rows = slice(j * qb, (j + 1) * qb)
            sj = scores(i, j, s, diagonal)
            mx = mx_scr[rows]
            for c in range(kt // lanes):
                mx = jnp.maximum(mx, sj[:, c * lanes:(c + 1) * lanes])
            mx_scr[rows] = mx

    def pv_tile(i, diagonal, track=False):
        k0 = pl.multiple_of(i * kt, kt)
        s = lax.dot_general(qa_scr[...], ka_scr[pl.ds(k0, kt), :], _NT, preferred_element_type=F32)
        for j in range(J):
            rows = slice(j * qb, (j + 1) * qb)
            sj = scores(i, j, s, diagonal)
            m = mx_scr[rows]
            excess = trk_scr[rows] if track else None
            for c in range(kt // lanes):
                d = sj[:, c * lanes:(c + 1) * lanes] - m
                p_scr[rows, c * lanes:(c + 1) * lanes] = jnp.exp(d).astype(BF16)
                if track:
                    excess = jnp.maximum(excess, d)
            if track:
                trk_scr[rows] = excess
        acc_scr[...] += _dot_row_halves(p_scr, kt, va_scr[pl.ds(k0, kt), :])

    n_full = start // kt

    def over_used_tiles(tile_fn):
        def full(i, carry):
            @pl.when(used_ref[i] > 0)
            def _():
                tile_fn(i)

            return carry

        lax.fori_loop(0, n_full, full, 0)

    def reduce_row_max():
        for j in range(J):
            rows = slice(j * qb, (j + 1) * qb)
            mx_scr[rows] = jnp.broadcast_to(jnp.max(mx_scr[rows], axis=-1, keepdims=True), (qb, lanes))

    mx_scr[...] = jnp.full(mx_scr.shape, NEG_BIG, F32)
    max_tile(n_full, True)
    reduce_row_max()
    trk_scr[...] = jnp.full(trk_scr.shape, NEG_BIG, F32)
    acc_scr[...] = jnp.zeros(acc_scr.shape, F32)
    over_used_tiles(lambda i: pv_tile(i, False, track=True))
    pv_tile(n_full, True)

    @pl.when(jnp.max(trk_scr[...]) > SEL_EXCESS_LIMIT)
    def _():
        over_used_tiles(lambda i: max_tile(i, False))
        reduce_row_max()
        acc_scr[...] = jnp.zeros(acc_scr.shape, F32)
        over_used_tiles(lambda i: pv_tile(i, False))
        pv_tile(n_full, True)

    gate = gate_ref[0]
    for j in range(J):
        rows = slice(j * qb, (j + 1) * qb)
        o_sel = acc_scr[rows, :dh] * (1.0 / jnp.maximum(acc_scr[rows, dh:], 1e-30))
        o = (gate[:, j:j + 1] * oc_scr[rows] + gate[:, J + j:J + j + 1] * o_sel
             + gate[:, 2 * J + j:2 * J + j + 1] * o_win[rows])
        o_ref[:, j * dh:(j + 1) * dh] = o.astype(o_ref.dtype)


def nsa_attention(q, gates, kv_cmp, kv_slabs, slopes, *, batch, seq_len, qb=256, kt=512):
    G, J, dh = NSA_KV_GROUPS, NSA_HPG, HEAD_DIM
    M = q.shape[0]
    nq = seq_len // qb
    ncp = kv_cmp.shape[3]
    ns = seq_len // SEL_BLOCK
    assert seq_len % kt == 0 and kt % qb == 0 and seq_len >= WINDOW + qb
    slabs = kv_slabs.reshape(kv_slabs.shape[0], batch, seq_len, dh)

    def slab(branch, is_v):
        base = (branch * 2 + is_v) * G
        return pl.BlockSpec((1, 1, seq_len, dh), lambda b, g, i, sl: (base + g, b, 0, 0))

    def cmp_spec(is_v):
        return pl.BlockSpec((1, 1, 1, ncp, dh), lambda b, g, i, sl: (is_v, b, g, 0, 0))

    grid_spec = pltpu.PrefetchScalarGridSpec(
        num_scalar_prefetch=1,
        grid=(batch, G, nq),
        in_specs=[
            pl.BlockSpec((qb, J * dh), lambda b, g, i, sl: (b * nq + i, g)),
            pl.BlockSpec((1, qb, NSA_BRANCHES * J), lambda b, g, i, sl: (g, b * nq + i, 0)),
            cmp_spec(0), cmp_spec(1),
            slab(1, 0), slab(1, 1), slab(2, 0), slab(2, 1),
        ],
        out_specs=pl.BlockSpec((qb, J * dh), lambda b, g, i, sl: (b * nq + i, g)),
        scratch_shapes=[
            pltpu.VMEM((J * qb, dh + ns), BF16),
            pltpu.VMEM((seq_len, dh + ns), BF16),
            pltpu.VMEM((seq_len, 2 * dh), BF16),
            pltpu.VMEM((J * qb, max(ncp, WINDOW + qb, kt)), BF16),
            pltpu.VMEM((J * qb, dh), F32),
            pltpu.VMEM((J * qb, LANES), F32),
            pltpu.VMEM((J * qb, LANES), F32),
            pltpu.VMEM((J * qb, 2 * dh), F32),
            pltpu.SMEM((seq_len // kt,), jnp.int32),
        ],
    )
    return pl.pallas_call(
        functools.partial(_nsa_kernel, kt=kt),
        grid_spec=grid_spec,
        out_shape=jax.ShapeDtypeStruct((M, G * J * dh), BF16),
        compiler_params=_params("parallel", "parallel", "arbitrary"),
        name="nsa_attention",
    )(slopes, q, gates, kv_cmp, kv_cmp, slabs, slabs, slabs, slabs)


def _ffn_layer(h, n, g1, w_gu, w_down, layer, conv_w, conv_b, *, seq_len):
    act = ffn_up(n, w_gu, layer, conv_w, conv_b, seq_len=seq_len)
    f = matmul(act, w_down, layer, tm=1024, tn=512)
    return residual_norm(h, f, g1)


def _xattn(h, mem_rows, mem_norm, g0, g1, g_next, wq, wkv, wo, layer, *, batch, seq_len, pending=None):
    mem_len = mem_rows.shape[0] // batch
    kv = norm_matmul(mem_rows, mem_norm, wkv, layer, tm=mem_rows.shape[0], tn=wkv.shape[2] // 2, out_dtype=BF16)
    kv = kv.reshape(batch, mem_len, wkv.shape[2])
    return xattn_layer(h, g0, wq[layer], kv, wo[layer], g1, g_next, seq_len=seq_len, pending=pending)


def _nsa_mixer(h, g0, w_in, w_out, layer, cmp_pos, cmp_w1, cmp_b1, cmp_w2, *, batch, seq_len):
    G, J, dh = NSA_KV_GROUPS, NSA_HPG, HEAD_DIM
    qw = G * J * dh
    kvw = NSA_BRANCHES * 2 * G * dh
    w_gate = w_in[layer, :, qw + kvw:].reshape(-1, NSA_BRANCHES, G, J).transpose(0, 2, 1, 3)
    w_gate = w_gate.reshape(-1, G * NSA_BRANCHES * J)
    q, kv_slabs, gates = nsa_in_proj(h, g0, w_in, layer, w_gate, q_width=qw, kv_width=kvw)
    kv_cmp = compress(kv_slabs, cmp_pos, cmp_w1.astype(BF16), cmp_b1, cmp_w2.astype(BF16),
                      batch=batch, seq_len=seq_len)
    n_heads = G * J
    slopes = 2.0 ** (-8.0 * jnp.arange(1, n_heads + 1, dtype=F32) / n_heads)
    o = nsa_attention(q, gates, kv_cmp, kv_slabs, slopes, batch=batch, seq_len=seq_len)
    return matmul(o, w_out, layer, tm=2048, tn=512)


def kernel(x, mem, ln_mix, ln_xa, ln_ffn, mem_norm, pool_w, pool_scale, nsa_w_in, nsa_w_out, nsa_cmp_pos, nsa_cmp_w1, nsa_cmp_b1, nsa_cmp_w2, xa_wq, xa_wkv, xa_wo, ffn_w_gu, ffn_conv_w, ffn_conv_b, ffn_w_down):
    B, S, D = x.shape
    depth = ln_mix.shape[0]
    h = x.reshape(B * S, D)
    mem_rows = mem.reshape(B * mem.shape[1], D)
    nsa_w_in, nsa_w_out = nsa_w_in.astype(BF16), nsa_w_out.astype(BF16)
    xa_wq, xa_wkv, xa_wo = xa_wq.astype(BF16), xa_wkv.astype(BF16), xa_wo.astype(BF16)
    ffn_w_down = ffn_w_down.astype(BF16)
    for i in range(depth):
        j = i // 2
        pending = None
        if i % 2 == 0:
            h = pool_layer(h, ln_mix[i, 0], pool_w[j].astype(BF16), pool_scale[j], ln_mix[i, 1], seq_len=S)
        else:
            a = _nsa_mixer(h, ln_mix[i, 0], nsa_w_in, nsa_w_out, j, nsa_cmp_pos[j],
                           nsa_cmp_w1[j], nsa_cmp_b1[j], nsa_cmp_w2[j], batch=B, seq_len=S)
            pending = (a, ln_mix[i, 1])
        h, n = _xattn(h, mem_rows, mem_norm, ln_xa[i, 0], ln_xa[i, 1], ln_ffn[i, 0], xa_wq, xa_wkv, xa_wo, i,
                      batch=B, seq_len=S, pending=pending)
        h = _ffn_layer(h, n, ln_ffn[i, 1], ffn_w_gu, ffn_w_down, i, ffn_conv_w[i], ffn_conv_b[i], seq_len=S)
    return h.reshape(B, S, D)
```

```python
import functools

import jax
import jax.numpy as jnp
from jax import lax
from jax.experimental import pallas as pl
from jax.experimental.pallas import tpu as pltpu

F32 = jnp.float32
BF16 = jnp.bfloat16

RMS_EPS = 1e-6
NEG_BIG = -1e30
HEAD_DIM = 128
POOL_WINDOWS = (2, 4, 8, 16)
POOL_HALO = 16
NSA_KV_GROUPS = 4
NSA_HPG = 8
NSA_BRANCHES = 3
CMP_LEN = 32
CMP_STRIDE = 16
SEL_BLOCK = 64
N_SELECT = 16
WINDOW = 512
FORCED_SCORE = 1e6
SEL_EXCESS_LIMIT = 30.0
XA_HEADS = 4
CONV_HALO = 16
VMEM_LIMIT = 56 * 1024 * 1024
NORM_CHUNK = 128
LANES = 128

_NT = (((1,), (1,)), ((), ()))


def _params(*sem):
    return pltpu.CompilerParams(dimension_semantics=sem, vmem_limit_bytes=VMEM_LIMIT)


def _rms(x, g):
    ms = jnp.mean(x * x, axis=-1, keepdims=True)
    return x * lax.rsqrt(ms + RMS_EPS) * g


def _norm_rows(h_ref, g_ref, a_scr):
    tm = h_ref.shape[0]
    chunk = min(NORM_CHUNK, tm)

    def body(c, carry):
        r = pl.multiple_of(c * chunk, chunk)
        a_scr[pl.ds(r, chunk), :] = _rms(h_ref[pl.ds(r, chunk), :], g_ref[...]).astype(BF16)
        return carry

    lax.fori_loop(0, tm // chunk, body, 0)


def _once(shape, index_map):
    return pl.BlockSpec(shape, index_map, pipeline_mode=pl.Buffered(1))


def _norm_mm_kernel(h_ref, g_ref, w_ref, o_ref, a_scr):
    @pl.when(pl.program_id(1) == 0)
    def _():
        _norm_rows(h_ref, g_ref, a_scr)

    o_ref[...] = jnp.dot(a_scr[...], w_ref[...], preferred_element_type=F32).astype(o_ref.dtype)


def norm_matmul(h, g, w, layer, *, tm, tn, out_dtype):
    M, D = h.shape
    N = w.shape[2]
    assert M % tm == 0 and N % tn == 0
    return pl.pallas_call(
        _norm_mm_kernel,
        grid=(M // tm, N // tn),
        in_specs=[
            _once((tm, D), lambda i, j: (i, 0)),
            pl.BlockSpec((1, D), lambda i, j: (0, 0)),
            pl.BlockSpec((None, D, tn), lambda i, j: (layer, 0, j)),
        ],
        out_specs=pl.BlockSpec((tm, tn), lambda i, j: (i, j)),
        out_shape=jax.ShapeDtypeStruct((M, N), out_dtype),
        scratch_shapes=[pltpu.VMEM((tm, D), BF16)],
        compiler_params=_params("parallel", "arbitrary"),
        name="norm_matmul",
    )(h, g.reshape(1, D), w)


def _nsa_proj_kernel(n_ref, w_ref, wg_ref, q_ref, kv_ref, gate_ref, *, nq, nkv, q_scale):
    j = pl.program_id(1)

    @pl.when(j < nq)
    def _():
        acc = jnp.dot(n_ref[...], w_ref[...], preferred_element_type=F32)
        q_ref[...] = (acc * q_scale).astype(q_ref.dtype)

    @pl.when((j >= nq) & (j < nq + nkv))
    def _():
        acc = jnp.dot(n_ref[...], w_ref[...], preferred_element_type=F32)
        for s in range(kv_ref.shape[0]):
            kv_ref[s] = acc[:, s * HEAD_DIM:(s + 1) * HEAD_DIM].astype(kv_ref.dtype)

    @pl.when(j == nq + nkv)
    def _():
        sig = jax.nn.sigmoid(jnp.dot(n_ref[...], wg_ref[...], preferred_element_type=F32))
        per_group = gate_ref.shape[2]
        for grp in range(gate_ref.shape[0]):
            gate_ref[grp] = sig[:, grp * per_group:(grp + 1) * per_group]


def nsa_in_proj(n, w_in, layer, w_gate, *, q_width, kv_width, tm=2048, tn=512):
    M, D = n.shape
    G = NSA_KV_GROUPS
    n_gate = w_gate.shape[1]
    assert M % tm == 0 and q_width % tn == 0 and kv_width % tn == 0
    nq, nkv = q_width // tn, kv_width // tn
    spt = tn // HEAD_DIM
    return pl.pallas_call(
        functools.partial(_nsa_proj_kernel, nq=nq, nkv=nkv, q_scale=HEAD_DIM ** -0.5),
        grid=(M // tm, nq + nkv + 1),
        in_specs=[
            _once((tm, D), lambda i, j: (i, 0)),
            pl.BlockSpec((None, D, tn), lambda i, j: (layer, 0, jnp.minimum(j, nq + nkv - 1))),
            pl.BlockSpec((D, n_gate), lambda i, j: (0, 0)),
        ],
        out_specs=[
            pl.BlockSpec((tm, tn), lambda i, j: (i, jnp.minimum(j, nq - 1))),
            pl.BlockSpec((spt, tm, HEAD_DIM), lambda i, j: (jnp.clip(j - nq, 0, nkv - 1), i, 0)),
            pl.BlockSpec((G, tm, n_gate // G), lambda i, j: (0, i, 0)),
        ],
        out_shape=[
            jax.ShapeDtypeStruct((M, q_width), BF16),
            jax.ShapeDtypeStruct((kv_width // HEAD_DIM, M, HEAD_DIM), BF16),
            jax.ShapeDtypeStruct((G, M, n_gate // G), F32),
        ],
        compiler_params=_params("parallel", "arbitrary"),
        name="nsa_in_proj",
    )(n, w_in, w_gate)


def _mm_kernel(a_ref, w_ref, o_ref):
    o_ref[...] = jnp.dot(a_ref[...], w_ref[...], preferred_element_type=F32).astype(o_ref.dtype)


def matmul(a, w, layer, *, tm, tn, out_dtype=F32):
    M, K = a.shape
    N = w.shape[2]
    assert M % tm == 0 and N % tn == 0
    return pl.pallas_call(
        _mm_kernel,
        grid=(M // tm, N // tn),
        in_specs=[
            _once((tm, K), lambda i, j: (i, 0)),
            pl.BlockSpec((None, K, tn), lambda i, j: (layer, 0, j)),
        ],
        out_specs=pl.BlockSpec((tm, tn), lambda i, j: (i, j)),
        out_shape=jax.ShapeDtypeStruct((M, N), out_dtype),
        compiler_params=_params("parallel", "arbitrary"),
        name="matmul",
    )(a, w)


def _resid_kernel(h_ref, f_ref, g_ref, *refs):
    h_new = h_ref[...] + _rms(f_ref[...], g_ref[...])
    if len(refs) == 1:
        (o_ref,) = refs
    else:
        gn_ref, o_ref, n_ref = refs
        n_ref[...] = _rms(h_new, gn_ref[...]).astype(n_ref.dtype)
    o_ref[...] = h_new


def residual_norm(h, f, g, g_next=None, *, tm=256):
    M, D = h.shape
    row = pl.BlockSpec((tm, D), lambda i: (i, 0))
    vec = pl.BlockSpec((1, D), lambda i: (0, 0))
    with_next = g_next is not None
    return pl.pallas_call(
        _resid_kernel,
        grid=(M // tm,),
        in_specs=[row, row, vec] + ([vec] if with_next else []),
        out_specs=[row, row] if with_next else row,
        out_shape=([jax.ShapeDtypeStruct((M, D), F32), jax.ShapeDtypeStruct((M, D), BF16)] if with_next
                   else jax.ShapeDtypeStruct((M, D), F32)),
        compiler_params=_params("parallel"),
        name="residual_norm",
    )(h, f, g.reshape(1, D), *([g_next.reshape(1, D)] if with_next else []))


def _pool_kernel(x_ref, halo_ref, g0_ref, w_ref, sc_ref, g1_ref, o_ref, *, seq_len):
    tm, D = x_ref.shape
    gc = D // len(POOL_WINDOWS)
    t0 = (pl.program_id(0) * tm) % seq_len
    x = x_ref[...]
    a = _rms(x, g0_ref[...])
    ha = _rms(halo_ref[...], g0_ref[...])
    ha = jnp.where(t0 == 0, 0.0, ha)
    full = jnp.concatenate([ha, a], axis=0)
    t1 = (t0 + 1 + lax.broadcasted_iota(jnp.int32, (tm, 1), 0)).astype(F32)
    ys = []
    for gi, win in enumerate(POOL_WINDOWS):
        s = full[:, gi * gc:(gi + 1) * gc]
        k = 1
        while k < win:
            s = s + pltpu.roll(s, k, 0)
            k *= 2
        inv_cnt = 1.0 / jnp.minimum(t1, float(win))
        pooled = s[POOL_HALO:] * inv_cnt
        d = (pooled - a[:, gi * gc:(gi + 1) * gc]).astype(BF16)
        ys.append(jnp.dot(d, w_ref[gi], preferred_element_type=F32))
    y = jnp.concatenate(ys, axis=-1) * sc_ref[...]
    o_ref[...] = x + _rms(y, g1_ref[...])


def pool_layer(x, g0, w, scale, g1, *, seq_len, tm=256):
    M, D = x.shape
    G, gc, _ = w.shape
    vec = pl.BlockSpec((1, D), lambda i: (0, 0))
    hb = tm // POOL_HALO
    return pl.pallas_call(
        functools.partial(_pool_kernel, seq_len=seq_len),
        grid=(M // tm,),
        in_specs=[
            pl.BlockSpec((tm, D), lambda i: (i, 0)),
            pl.BlockSpec((POOL_HALO, D), lambda i: (jnp.maximum(i * hb - 1, 0), 0)),
            vec,
            _once((G, gc, gc), lambda i: (0, 0, 0)),
            vec,
            vec,
        ],
        out_specs=pl.BlockSpec((tm, D), lambda i: (i, 0)),
        out_shape=jax.ShapeDtypeStruct((M, D), F32),
        compiler_params=_params("parallel"),
        name="pool_layer",
    )(x, x, g0.reshape(1, D), w, scale.reshape(1, D), g1.reshape(1, D))


def _xattn_kernel(h_ref, *refs, pending):
    h = h_ref[...]
    if pending:
        a_ref, ga_ref, *refs = refs
        h = h + _rms(a_ref[...], ga_ref[...])
    g0_ref, wq_ref, k_ref, v_ref, wo_ref, g1_ref, gn_ref, o_ref, n_ref = refs
    n = _rms(h, g0_ref[...]).astype(BF16)
    q = jnp.dot(n, wq_ref[...], preferred_element_type=F32).astype(BF16)
    k = k_ref[0]
    v = v_ref[0]
    scale = HEAD_DIM ** -0.5
    outs = []
    for hh in range(XA_HEADS):
        sl = slice(hh * HEAD_DIM, (hh + 1) * HEAD_DIM)
        s = lax.dot_general(q[:, sl], k[:, sl], _NT, preferred_element_type=F32) * scale
        m = jnp.max(s, axis=-1, keepdims=True)
        e = jnp.exp(s - m)
        p = e * (1.0 / jnp.sum(e, axis=-1, keepdims=True))
        outs.append(jnp.dot(p.astype(BF16), v[:, sl], preferred_element_type=F32))
    o = jnp.concatenate(outs, axis=-1).astype(BF16)
    c = jnp.dot(o, wo_ref[...], preferred_element_type=F32)
    h_new = h + _rms(c, g1_ref[...])
    o_ref[...] = h_new
    n_ref[...] = _rms(h_new, gn_ref[...]).astype(n_ref.dtype)


def xattn_layer(h, g0, wq, kv, wo, g1, g_next, *, seq_len, pending=None, tm=256):
    M, D = h.shape
    xw = wq.shape[1]
    mem_len = kv.shape[1]
    per_seq = seq_len // tm
    vec = pl.BlockSpec((1, D), lambda i: (0, 0))
    row = pl.BlockSpec((tm, D), lambda i: (i, 0))
    pend_specs, pend_args = ([row, vec], [pending[0], pending[1].reshape(1, D)]) if pending is not None else ([], [])
    return pl.pallas_call(
        functools.partial(_xattn_kernel, pending=pending is not None),
        grid=(M // tm,),
        in_specs=[
            row,
            *pend_specs,
            vec,
            _once((D, xw), lambda i: (0, 0)),
            pl.BlockSpec((1, mem_len, xw), lambda i: (i // per_seq, 0, 0)),
            pl.BlockSpec((1, mem_len, xw), lambda i: (i // per_seq, 0, 1)),
            _once((xw, D), lambda i: (0, 0)),
            vec,
            vec,
        ],
        out_specs=[row, row],
        out_shape=[jax.ShapeDtypeStruct((M, D), F32), jax.ShapeDtypeStruct((M, D), BF16)],
        compiler_params=_params("parallel"),
        name="xattn_layer",
    )(h, *pend_args, g0.reshape(1, D), wq, kv, kv, wo, g1.reshape(1, D), g_next.reshape(1, D))


def _ffn_up_kernel(n_ref, halo_ref, wg_ref, wu_ref, cw_ref, cb_ref, o_ref, *, seq_len):
    tm = n_ref.shape[0]
    t0 = (pl.program_id(0) * tm) % seq_len
    a = n_ref[...]
    halo = jnp.where(t0 == 0, 0.0, halo_ref[...]).astype(BF16)
    wg = wg_ref[...].astype(BF16)
    gate = jnp.dot(a, wg, preferred_element_type=F32)
    up = jnp.dot(a, wu_ref[...].astype(BF16), preferred_element_type=F32)
    hg = jnp.dot(halo, wg, preferred_element_type=F32)
    row = lax.broadcasted_iota(jnp.int32, (tm, 1), 0)
    prev1 = hg[CONV_HALO - 1:CONV_HALO]
    prev2 = hg[CONV_HALO - 2:CONV_HALO - 1]
    g1 = jnp.where(row == 0, prev1, pltpu.roll(gate, 1, 0))
    g2 = jnp.where(row == 0, prev2, jnp.where(row == 1, prev1, pltpu.roll(gate, 2, 0)))
    cw = cw_ref[...]
    gc = cw[0:1] * g2 + cw[1:2] * g1 + cw[2:3] * gate + cb_ref[...]
    o_ref[...] = (gc * jax.nn.sigmoid(gc) * up).astype(o_ref.dtype)


def ffn_up(n, w_gu, layer, conv_w, conv_b, *, seq_len, tm=2048, tn=256):
    M, D = n.shape
    F = w_gu.shape[2] // 2
    assert F % tn == 0 and M % tm == 0 and seq_len % tm == 0
    nj = F // tn
    hb = tm // CONV_HALO
    return pl.pallas_call(
        functools.partial(_ffn_up_kernel, seq_len=seq_len),
        grid=(M // tm, nj),
        in_specs=[
            _once((tm, D), lambda i, j: (i, 0)),
            _once((CONV_HALO, D), lambda i, j: (jnp.maximum(i * hb - 1, 0), 0)),
            pl.BlockSpec((None, D, tn), lambda i, j: (layer, 0, j)),
            pl.BlockSpec((None, D, tn), lambda i, j: (layer, 0, j + nj)),
            pl.BlockSpec((conv_w.shape[0], tn), lambda i, j: (0, j)),
            pl.BlockSpec((1, tn), lambda i, j: (0, j)),
        ],
        out_specs=pl.BlockSpec((tm, tn), lambda i, j: (i, j)),
        out_shape=jax.ShapeDtypeStruct((M, F), BF16),
        compiler_params=_params("parallel", "arbitrary"),
        name="ffn_up",
    )(n, n, w_gu, w_gu, conv_w, conv_b.reshape(1, F))


def _gelu_tanh(x):
    return 0.5 * x * (1.0 + jnp.tanh(0.7978845608028654 * (x + 0.044715 * x * x * x)))


def _compress_kernel(r_ref, pos_ref, w1_ref, b1_ref, w2_ref, o_ref):
    r = r_ref[0, 0].astype(F32)
    nch, half = r.shape
    pos = pos_ref[0]
    xa = (r + pos[0:1]).astype(BF16)
    xb = (r + pos[1:2]).astype(BF16)
    h1 = jnp.dot(xa, w1_ref[0, :half, :], preferred_element_type=F32)
    h2 = jnp.dot(xb, w1_ref[0, half:, :], preferred_element_type=F32)
    hid = h1 + pltpu.roll(h2, nch - 1, 0) + b1_ref[0]
    out = jnp.dot(_gelu_tanh(hid).astype(BF16), w2_ref[0], preferred_element_type=F32)
    row = lax.broadcasted_iota(jnp.int32, (nch, 1), 0)
    o_ref[0, 0, 0] = jnp.where(row < nch - 1, out, 0.0).astype(o_ref.dtype)


def compress(kv_slabs, pos, w1, b1, w2, *, batch, seq_len):
    G = NSA_KV_GROUPS
    nch = seq_len // CMP_STRIDE
    half = CMP_STRIDE * HEAD_DIM
    r = kv_slabs[:2 * G].reshape(2 * G, batch, nch, half)
    hidden = w1.shape[-1]
    return pl.pallas_call(
        _compress_kernel,
        grid=(2, batch, G),
        in_specs=[
            pl.BlockSpec((1, 1, nch, half), lambda kv, b, g: (kv * G + g, b, 0, 0)),
            pl.BlockSpec((1, 2, half), lambda kv, b, g: (kv, 0, 0)),
            pl.BlockSpec((1, 2 * half, hidden), lambda kv, b, g: (kv, 0, 0)),
            pl.BlockSpec((1, 1, hidden), lambda kv, b, g: (kv, 0, 0)),
            pl.BlockSpec((1, hidden, HEAD_DIM), lambda kv, b, g: (kv, 0, 0)),
        ],
        out_specs=pl.BlockSpec((1, 1, 1, nch, HEAD_DIM), lambda kv, b, g: (kv, b, g, 0, 0)),
        out_shape=jax.ShapeDtypeStruct((2, batch, G, nch, HEAD_DIM), BF16),
        compiler_params=_params("parallel", "parallel", "parallel"),
        name="nsa_compress",
    )(r, pos.reshape(2, 2, half), w1, b1.reshape(2, 1, hidden), w2)


def _dot_row_halves(p_ref, width, rhs):
    half = p_ref.shape[0] // 2
    return jnp.concatenate([jnp.dot(p_ref[:half, :width], rhs, preferred_element_type=F32),
                            jnp.dot(p_ref[half:, :width], rhs, preferred_element_type=F32)], axis=0)


def _nsa_kernel(slope_ref, q_ref, gate_ref, kc_ref, vc_ref, ks_ref, vs_ref, kw_ref, vw_ref, o_ref,
                qa_scr, ka_scr, va_scr, p_scr, oc_scr, mx_scr, trk_scr, acc_scr, used_ref, *, kt):
    J, dh = NSA_HPG, HEAD_DIM
    qb = q_ref.shape[0]
    seq_len = ks_ref.shape[2]
    ncp = kc_ref.shape[3]
    ns = seq_len // SEL_BLOCK
    g = pl.program_id(1)
    qi = pl.program_id(2)
    start = qi * qb

    @pl.when(qi == 0)
    def _():
        chunk = min(512, seq_len)

        def body(c, carry):
            r = pl.multiple_of(c * chunk, chunk)
            ka_scr[pl.ds(r, chunk), :dh] = ks_ref[0, 0, pl.ds(r, chunk), :]
            blk = (r + lax.broadcasted_iota(jnp.int32, (chunk, ns), 0)) // SEL_BLOCK
            ka_scr[pl.ds(r, chunk), dh:] = (blk == lax.broadcasted_iota(jnp.int32, (chunk, ns), 1)).astype(BF16)
            va_scr[pl.ds(r, chunk), :dh] = vs_ref[0, 0, pl.ds(r, chunk), :]
            va_scr[pl.ds(r, chunk), dh:] = jnp.ones((chunk, dh), BF16)
            return carry

        lax.fori_loop(0, seq_len // chunk, body, 0)

    q = q_ref[...]
    for j in range(J):
        qa_scr[j * qb:(j + 1) * qb, :dh] = q[:, j * dh:(j + 1) * dh]
    qr = qa_scr[:, :dh]
    slopes = [slope_ref[g * J + j] for j in range(J)]
    t_col = start + lax.broadcasted_iota(jnp.int32, (qb, 1), 0)

    lanes = mx_scr.shape[1]
    kc = kc_ref[0, 0, 0]
    sc = lax.dot_general(qr, kc, _NT, preferred_element_type=F32)
    c_end = lax.broadcasted_iota(jnp.int32, (1, ncp), 1) * CMP_STRIDE + (CMP_LEN - 1)
    bias_c = jnp.where(t_col >= c_end, 0.0, NEG_BIG)
    rel_c = (c_end - start).astype(F32)
    for j in range(J):
        rows = slice(j * qb, (j + 1) * qb)
        sj = sc[rows] + bias_c + slopes[j] * rel_c
        p_scr[rows, :ncp] = jnp.exp(sj - jnp.max(sj, axis=-1, keepdims=True)).astype(BF16)
    c_row = lax.broadcasted_iota(jnp.int32, (ncp, ns), 0) * CMP_STRIDE
    n_col = lax.broadcasted_iota(jnp.int32, (ncp, ns), 1) * SEL_BLOCK
    overlap = ((c_row < n_col + SEL_BLOCK) & (c_row + CMP_LEN > n_col)
               & (c_row < (ncp - 1) * CMP_STRIDE)).astype(BF16)
    rhs_c = jnp.concatenate([vc_ref[0, 0, 0], jnp.ones((ncp, lanes), BF16), overlap], axis=1)
    oc = _dot_row_halves(p_scr, ncp, rhs_c)
    has_key = t_col >= CMP_LEN - 1
    score = jnp.zeros((qb, ns), F32)
    for j in range(J):
        rows = slice(j * qb, (j + 1) * qb)
        inv = jnp.where(has_key, 1.0 / jnp.maximum(oc[rows, dh:dh + lanes], 1e-30), 0.0)
        oc_scr[rows] = oc[rows, :dh] * inv
        score = score + oc[rows, dh + lanes:] * inv[:, :ns]

    wl = WINDOW + qb
    w0 = pl.multiple_of(jnp.maximum(start - WINDOW, 0), qb)
    kw = kw_ref[0, 0, pl.ds(w0, wl), :]
    vw = jnp.concatenate([vw_ref[0, 0, pl.ds(w0, wl), :], jnp.ones((wl, dh), BF16)], axis=1)
    sw = lax.dot_general(qr, kw, _NT, preferred_element_type=F32)
    pos_w = w0 + lax.broadcasted_iota(jnp.int32, (1, wl), 1)
    bias_w = jnp.where((pos_w <= t_col) & (pos_w > t_col - WINDOW), 0.0, NEG_BIG)
    rel_w = (pos_w - start).astype(F32)
    for j in range(J):
        rows = slice(j * qb, (j + 1) * qb)
        sj = sw[rows] + bias_w + slopes[j] * rel_w
        p_scr[rows, :wl] = jnp.exp(sj - jnp.max(sj, axis=-1, keepdims=True)).astype(BF16)
    ow = _dot_row_halves(p_scr, wl, vw)
    o_win = ow[:, :dh] * (1.0 / jnp.maximum(ow[:, dh:], 1e-30))

    score_t = score.T
    n_idx = lax.broadcasted_iota(jnp.int32, (ns, qb), 0)
    cur = (start + lax.broadcasted_iota(jnp.int32, (1, qb), 1)) // SEL_BLOCK
    forced = (n_idx == 0) | (n_idx == cur) | (n_idx == cur - 1)
    score_t = jnp.where(n_idx > cur, -1.0, jnp.where(forced, FORCED_SCORE, score_t))
    sel_t = jnp.zeros((ns, qb), F32)
    for _ in range(min(N_SELECT, ns)):
        best = jnp.max(score_t, axis=0, keepdims=True)
        first = jnp.min(jnp.where(score_t == best, n_idx, ns), axis=0, keepdims=True)
        hit = n_idx == first
        sel_t = jnp.where(hit, 1.0, sel_t)
        score_t = jnp.where(hit, -2.0, score_t)
    block_bias = jnp.where(sel_t.T > 0.5, 0.0, NEG_BIG).astype(BF16)
    for j in range(J):
        qa_scr[j * qb:(j + 1) * qb, dh:] = block_bias
    bpt = kt // SEL_BLOCK
    for i in range(ns // bpt):
        used_ref[i] = (jnp.max(sel_t[i * bpt:(i + 1) * bpt, :]) > 0.5).astype(jnp.int32)

    pos_d = (start // kt) * kt + lax.broadcasted_iota(jnp.int32, (1, kt), 1)
    causal_bias = jnp.where(pos_d <= t_col, 0.0, NEG_BIG)

    def scores(i, j, s, diagonal):
        pos = i * kt + lax.broadcasted_iota(jnp.int32, (1, kt), 1)
        sj = s[j * qb:(j + 1) * qb] + slopes[j] * (pos - start).astype(F32)
        if diagonal:
            sj = sj + causal_bias
        return sj

    def max_tile(i, diagonal):
        k0 = pl.multiple_of(i * kt, kt)
        s = lax.dot_general(qa_scr[...], ka_scr[pl.ds(k0, kt), :], _NT, preferred_element_type=F32)
        for j in range(J):
            rows = slice(j * qb, (j + 1) * qb)
            sj = scores(i, j, s, diagonal)
            mx = mx_scr[rows]
            for c in range(kt // lanes):
                mx = jnp.maximum(mx, sj[:, c * lanes:(c + 1) * lanes])
            mx_scr[rows] = mx

    def pv_tile(i, diagonal, track=False):
        k0 = pl.multiple_of(i * kt, kt)
        s = lax.dot_general(qa_scr[...], ka_scr[pl.ds(k0, kt), :], _NT, preferred_element_type=F32)
        for j in range(J):
            rows = slice(j * qb, (j + 1) * qb)
            sj = scores(i, j, s, diagonal)
            m = mx_scr[rows]
            excess = trk_scr[rows] if track else None
            for c in range(kt // lanes):
                d = sj[:, c * lanes:(c + 1) * lanes] - m
                p_scr[rows, c * lanes:(c + 1) * lanes] = jnp.exp(d).astype(BF16)
                if track:
                    excess = jnp.maximum(excess, d)
            if track:
                trk_scr[rows] = excess
        acc_scr[...] += _dot_row_halves(p_scr, kt, va_scr[pl.ds(k0, kt), :])

    n_full = start // kt

    def over_used_tiles(tile_fn):
        def full(i, carry):
            @pl.when(used_ref[i] > 0)
            def _():
                tile_fn(i)

            return carry

        lax.fori_loop(0, n_full, full, 0)

    def reduce_row_max():
        for j in range(J):
            rows = slice(j * qb, (j + 1) * qb)
            mx_scr[rows] = jnp.broadcast_to(jnp.max(mx_scr[rows], axis=-1, keepdims=True), (qb, lanes))

    mx_scr[...] = jnp.full(mx_scr.shape, NEG_BIG, F32)
    max_tile(n_full, True)
    reduce_row_max()
    trk_scr[...] = jnp.full(trk_scr.shape, NEG_BIG, F32)
    acc_scr[...] = jnp.zeros(acc_scr.shape, F32)
    over_used_tiles(lambda i: pv_tile(i, False, track=True))
    pv_tile(n_full, True)

    @pl.when(jnp.max(trk_scr[...]) > SEL_EXCESS_LIMIT)
    def _():
        over_used_tiles(lambda i: max_tile(i, False))
        reduce_row_max()
        acc_scr[...] = jnp.zeros(acc_scr.shape, F32)
        over_used_tiles(lambda i: pv_tile(i, False))
        pv_tile(n_full, True)

    gate = gate_ref[0]
    for j in range(J):
        rows = slice(j * qb, (j + 1) * qb)
        o_sel = acc_scr[rows, :dh] * (1.0 / jnp.maximum(acc_scr[rows, dh:], 1e-30))
        o = (gate[:, j:j + 1] * oc_scr[rows] + gate[:, J + j:J + j + 1] * o_sel
             + gate[:, 2 * J + j:2 * J + j + 1] * o_win[rows])
        o_ref[:, j * dh:(j + 1) * dh] = o.astype(o_ref.dtype)


def nsa_attention(q, gates, kv_cmp, kv_slabs, slopes, *, batch, seq_len, qb=256, kt=512):
    G, J, dh = NSA_KV_GROUPS, NSA_HPG, HEAD_DIM
    M = q.shape[0]
    nq = seq_len // qb
    ncp = kv_cmp.shape[3]
    ns = seq_len // SEL_BLOCK
    assert seq_len % kt == 0 and kt % qb == 0 and seq_len >= WINDOW + qb
    slabs = kv_slabs.reshape(kv_slabs.shape[0], batch, seq_len, dh)

    def slab(branch, is_v):
        base = (branch * 2 + is_v) * G
        return pl.BlockSpec((1, 1, seq_len, dh), lambda b, g, i, sl: (base + g, b, 0, 0))

    def cmp_spec(is_v):
        return pl.BlockSpec((1, 1, 1, ncp, dh), lambda b, g, i, sl: (is_v, b, g, 0, 0))

    grid_spec = pltpu.PrefetchScalarGridSpec(
        num_scalar_prefetch=1,
        grid=(batch, G, nq),
        in_specs=[
            pl.BlockSpec((qb, J * dh), lambda b, g, i, sl: (b * nq + i, g)),
            pl.BlockSpec((1, qb, NSA_BRANCHES * J), lambda b, g, i, sl: (g, b * nq + i, 0)),
            cmp_spec(0), cmp_spec(1),
            slab(1, 0), slab(1, 1), slab(2, 0), slab(2, 1),
        ],
        out_specs=pl.BlockSpec((qb, J * dh), lambda b, g, i, sl: (b * nq + i, g)),
        scratch_shapes=[
            pltpu.VMEM((J * qb, dh + ns), BF16),
            pltpu.VMEM((seq_len, dh + ns), BF16),
            pltpu.VMEM((seq_len, 2 * dh), BF16),
            pltpu.VMEM((J * qb, max(ncp, WINDOW + qb, kt)), BF16),
            pltpu.VMEM((J * qb, dh), F32),
            pltpu.VMEM((J * qb, LANES), F32),
            pltpu.VMEM((J * qb, LANES), F32),
            pltpu.VMEM((J * qb, 2 * dh), F32),
            pltpu.SMEM((seq_len // kt,), jnp.int32),
        ],
    )
    return pl.pallas_call(
        functools.partial(_nsa_kernel, kt=kt),
        grid_spec=grid_spec,
        out_shape=jax.ShapeDtypeStruct((M, G * J * dh), BF16),
        compiler_params=_params("parallel", "parallel", "arbitrary"),
        name="nsa_attention",
    )(slopes, q, gates, kv_cmp, kv_cmp, slabs, slabs, slabs, slabs)


def _ffn_layer(h, n, g1, w_gu, w_down, layer, conv_w, conv_b, *, seq_len, g_next=None):
    act = ffn_up(n, w_gu, layer, conv_w, conv_b, seq_len=seq_len)
    f = matmul(act, w_down, layer, tm=1024, tn=512)
    return residual_norm(h, f, g1, g_next)


def _xattn(h, mem_rows, mem_norm, g0, g1, g_next, wq, wkv, wo, layer, *, batch, seq_len, pending=None):
    mem_len = mem_rows.shape[0] // batch
    kv = norm_matmul(mem_rows, mem_norm, wkv, layer, tm=mem_rows.shape[0], tn=wkv.shape[2] // 2, out_dtype=BF16)
    kv = kv.reshape(batch, mem_len, wkv.shape[2])
    return xattn_layer(h, g0, wq[layer], kv, wo[layer], g1, g_next, seq_len=seq_len, pending=pending)


def _nsa_mixer(n, w_in, w_out, layer, cmp_pos, cmp_w1, cmp_b1, cmp_w2, *, batch, seq_len):
    G, J, dh = NSA_KV_GROUPS, NSA_HPG, HEAD_DIM
    qw = G * J * dh
    kvw = NSA_BRANCHES * 2 * G * dh
    w_gate = w_in[layer, :, qw + kvw:].reshape(-1, NSA_BRANCHES, G, J).transpose(0, 2, 1, 3)
    w_gate = w_gate.reshape(-1, G * NSA_BRANCHES * J)
    q, kv_slabs, gates = nsa_in_proj(n, w_in, layer, w_gate, q_width=qw, kv_width=kvw)
    kv_cmp = compress(kv_slabs, cmp_pos, cmp_w1.astype(BF16), cmp_b1, cmp_w2.astype(BF16),
                      batch=batch, seq_len=seq_len)
    n_heads = G * J
    slopes = 2.0 ** (-8.0 * jnp.arange(1, n_heads + 1, dtype=F32) / n_heads)
    o = nsa_attention(q, gates, kv_cmp, kv_slabs, slopes, batch=batch, seq_len=seq_len)
    return matmul(o, w_out, layer, tm=2048, tn=512)


def kernel(x, mem, ln_mix, ln_xa, ln_ffn, mem_norm, pool_w, pool_scale, nsa_w_in, nsa_w_out, nsa_cmp_pos, nsa_cmp_w1, nsa_cmp_b1, nsa_cmp_w2, xa_wq, xa_wkv, xa_wo, ffn_w_gu, ffn_conv_w, ffn_conv_b, ffn_w_down):
    B, S, D = x.shape
    depth = ln_mix.shape[0]
    h = x.reshape(B * S, D)
    mem_rows = mem.reshape(B * mem.shape[1], D)
    nsa_w_in, nsa_w_out = nsa_w_in.astype(BF16), nsa_w_out.astype(BF16)
    xa_wq, xa_wkv, xa_wo = xa_wq.astype(BF16), xa_wkv.astype(BF16), xa_wo.astype(BF16)
    ffn_w_down = ffn_w_down.astype(BF16)
    n_mix = None
    for i in range(depth):
        j = i // 2
        pending = None
        if i % 2 == 0:
            h = pool_layer(h, ln_mix[i, 0], pool_w[j].astype(BF16), pool_scale[j], ln_mix[i, 1], seq_len=S)
        else:
            a = _nsa_mixer(n_mix, nsa_w_in, nsa_w_out, j, nsa_cmp_pos[j],
                           nsa_cmp_w1[j], nsa_cmp_b1[j], nsa_cmp_w2[j], batch=B, seq_len=S)
            pending = (a, ln_mix[i, 1])
        h, n = _xattn(h, mem_rows, mem_norm, ln_xa[i, 0], ln_xa[i, 1], ln_ffn[i, 0], xa_wq, xa_wkv, xa_wo, i,
                      batch=B, seq_len=S, pending=pending)
        nsa_next = i + 1 < depth and (i + 1) % 2 == 1
        out = _ffn_layer(h, n, ln_ffn[i, 1], ffn_w_gu, ffn_w_down, i, ffn_conv_w[i], ffn_conv_b[i], seq_len=S,
                         g_next=ln_mix[i + 1, 0] if nsa_next else None)
        h, n_mix = out if nsa_next else (out, None)
    return h.reshape(B, S, D)
```

```python
import functools

import jax
import jax.numpy as jnp
from jax import lax
from jax.experimental import pallas as pl
from jax.experimental.pallas import tpu as pltpu

F32 = jnp.float32
BF16 = jnp.bfloat16

RMS_EPS = 1e-6
NEG_BIG = -1e30
HEAD_DIM = 128
POOL_WINDOWS = (2, 4, 8, 16)
POOL_HALO = 16
NSA_KV_GROUPS = 4
NSA_HPG = 8
NSA_BRANCHES = 3
CMP_LEN = 32
CMP_STRIDE = 16
SEL_BLOCK = 64
N_SELECT = 16
WINDOW = 512
FORCED_SCORE = 1e6
SEL_EXCESS_LIMIT = 30.0
XA_HEADS = 4
CONV_HALO = 16
VMEM_LIMIT = 56 * 1024 * 1024
NORM_CHUNK = 128
LANES = 128

_NT = (((1,), (1,)), ((), ()))


def _params(*sem):
    return pltpu.CompilerParams(dimension_semantics=sem, vmem_limit_bytes=VMEM_LIMIT)


def _rms(x, g):
    ms = jnp.mean(x * x, axis=-1, keepdims=True)
    return x * lax.rsqrt(ms + RMS_EPS) * g


def _norm_rows(h_ref, g_ref, a_scr):
    tm = h_ref.shape[0]
    chunk = min(NORM_CHUNK, tm)

    def body(c, carry):
        r = pl.multiple_of(c * chunk, chunk)
        a_scr[pl.ds(r, chunk), :] = _rms(h_ref[pl.ds(r, chunk), :], g_ref[...]).astype(BF16)
        return carry

    lax.fori_loop(0, tm // chunk, body, 0)


def _once(shape, index_map):
    return pl.BlockSpec(shape, index_map, pipeline_mode=pl.Buffered(1))


def _norm_mm_kernel(h_ref, g_ref, w_ref, o_ref, a_scr):
    @pl.when(pl.program_id(1) == 0)
    def _():
        _norm_rows(h_ref, g_ref, a_scr)

    o_ref[...] = jnp.dot(a_scr[...], w_ref[...], preferred_element_type=F32).astype(o_ref.dtype)


def norm_matmul(h, g, w, layer, *, tm, tn, out_dtype):
    M, D = h.shape
    N = w.shape[2]
    assert M % tm == 0 and N % tn == 0
    return pl.pallas_call(
        _norm_mm_kernel,
        grid=(M // tm, N // tn),
        in_specs=[
            _once((tm, D), lambda i, j: (i, 0)),
            pl.BlockSpec((1, D), lambda i, j: (0, 0)),
            pl.BlockSpec((None, D, tn), lambda i, j: (layer, 0, j)),
        ],
        out_specs=pl.BlockSpec((tm, tn), lambda i, j: (i, j)),
        out_shape=jax.ShapeDtypeStruct((M, N), out_dtype),
        scratch_shapes=[pltpu.VMEM((tm, D), BF16)],
        compiler_params=_params("parallel", "arbitrary"),
        name="norm_matmul",
    )(h, g.reshape(1, D), w)


def _nsa_proj_kernel(n_ref, w_ref, wg_ref, q_ref, kv_ref, gate_ref, *, nq, nkv, q_scale):
    j = pl.program_id(1)

    @pl.when(j < nq)
    def _():
        acc = jnp.dot(n_ref[...], w_ref[...], preferred_element_type=F32)
        q_ref[...] = (acc * q_scale).astype(q_ref.dtype)

    @pl.when((j >= nq) & (j < nq + nkv))
    def _():
        acc = jnp.dot(n_ref[...], w_ref[...], preferred_element_type=F32)
        for s in range(kv_ref.shape[0]):
            kv_ref[s] = acc[:, s * HEAD_DIM:(s + 1) * HEAD_DIM].astype(kv_ref.dtype)

    @pl.when(j == nq + nkv)
    def _():
        sig = jax.nn.sigmoid(jnp.dot(n_ref[...], wg_ref[...], preferred_element_type=F32))
        per_group = gate_ref.shape[2]
        for grp in range(gate_ref.shape[0]):
            gate_ref[grp] = sig[:, grp * per_group:(grp + 1) * per_group]


def nsa_in_proj(n, w_in, layer, w_gate, *, q_width, kv_width, tm=2048, tn=512):
    M, D = n.shape
    G = NSA_KV_GROUPS
    n_gate = w_gate.shape[1]
    assert M % tm == 0 and q_width % tn == 0 and kv_width % tn == 0
    nq, nkv = q_width // tn, kv_width // tn
    spt = tn // HEAD_DIM
    return pl.pallas_call(
        functools.partial(_nsa_proj_kernel, nq=nq, nkv=nkv, q_scale=HEAD_DIM ** -0.5),
        grid=(M // tm, nq + nkv + 1),
        in_specs=[
            _once((tm, D), lambda i, j: (i, 0)),
            pl.BlockSpec((None, D, tn), lambda i, j: (layer, 0, jnp.minimum(j, nq + nkv - 1))),
            pl.BlockSpec((D, n_gate), lambda i, j: (0, 0)),
        ],
        out_specs=[
            pl.BlockSpec((tm, tn), lambda i, j: (i, jnp.minimum(j, nq - 1))),
            pl.BlockSpec((spt, tm, HEAD_DIM), lambda i, j: (jnp.clip(j - nq, 0, nkv - 1), i, 0)),
            pl.BlockSpec((G, tm, n_gate // G), lambda i, j: (0, i, 0)),
        ],
        out_shape=[
            jax.ShapeDtypeStruct((M, q_width), BF16),
            jax.ShapeDtypeStruct((kv_width // HEAD_DIM, M, HEAD_DIM), BF16),
            jax.ShapeDtypeStruct((G, M, n_gate // G), F32),
        ],
        compiler_params=_params("parallel", "arbitrary"),
        name="nsa_in_proj",
    )(n, w_in, w_gate)


def _mm_kernel(a_ref, w_ref, o_ref):
    o_ref[...] = jnp.dot(a_ref[...], w_ref[...], preferred_element_type=F32).astype(o_ref.dtype)


def matmul(a, w, layer, *, tm, tn, out_dtype=F32):
    M, K = a.shape
    N = w.shape[2]
    assert M % tm == 0 and N % tn == 0
    return pl.pallas_call(
        _mm_kernel,
        grid=(M // tm, N // tn),
        in_specs=[
            _once((tm, K), lambda i, j: (i, 0)),
            pl.BlockSpec((None, K, tn), lambda i, j: (layer, 0, j)),
        ],
        out_specs=pl.BlockSpec((tm, tn), lambda i, j: (i, j)),
        out_shape=jax.ShapeDtypeStruct((M, N), out_dtype),
        compiler_params=_params("parallel", "arbitrary"),
        name="matmul",
    )(a, w)


def _resid_kernel(h_ref, f_ref, g_ref, *refs):
    h_new = h_ref[...] + _rms(f_ref[...], g_ref[...])
    if len(refs) == 1:
        (o_ref,) = refs
    else:
        gn_ref, o_ref, n_ref = refs
        n_ref[...] = _rms(h_new, gn_ref[...]).astype(n_ref.dtype)
    o_ref[...] = h_new


def residual_norm(h, f, g, g_next=None, *, tm=256):
    M, D = h.shape
    row = pl.BlockSpec((tm, D), lambda i: (i, 0))
    vec = pl.BlockSpec((1, D), lambda i: (0, 0))
    with_next = g_next is not None
    return pl.pallas_call(
        _resid_kernel,
        grid=(M // tm,),
        in_specs=[row, row, vec] + ([vec] if with_next else []),
        out_specs=[row, row] if with_next else row,
        out_shape=([jax.ShapeDtypeStruct((M, D), F32), jax.ShapeDtypeStruct((M, D), BF16)] if with_next
                   else jax.ShapeDtypeStruct((M, D), F32)),
        compiler_params=_params("parallel"),
        name="residual_norm",
    )(h, f, g.reshape(1, D), *([g_next.reshape(1, D)] if with_next else []))


def _pool_kernel(x_ref, halo_ref, g0_ref, w_ref, sc_ref, g1_ref, o_ref, *, seq_len):
    tm, D = x_ref.shape
    gc = D // len(POOL_WINDOWS)
    t0 = (pl.program_id(0) * tm) % seq_len
    x = x_ref[...]
    a = _rms(x, g0_ref[...])
    ha = _rms(halo_ref[...], g0_ref[...])
    ha = jnp.where(t0 == 0, 0.0, ha)
    full = jnp.concatenate([ha, a], axis=0)
    t1 = (t0 + 1 + lax.broadcasted_iota(jnp.int32, (tm, 1), 0)).astype(F32)
    ys = []
    for gi, win in enumerate(POOL_WINDOWS):
        s = full[:, gi * gc:(gi + 1) * gc]
        k = 1
        while k < win:
            s = s + pltpu.roll(s, k, 0)
            k *= 2
        inv_cnt = 1.0 / jnp.minimum(t1, float(win))
        pooled = s[POOL_HALO:] * inv_cnt
        d = (pooled - a[:, gi * gc:(gi + 1) * gc]).astype(BF16)
        ys.append(jnp.dot(d, w_ref[gi], preferred_element_type=F32))
    y = jnp.concatenate(ys, axis=-1) * sc_ref[...]
    o_ref[...] = x + _rms(y, g1_ref[...])


def pool_layer(x, g0, w, scale, g1, *, seq_len, tm=256):
    M, D = x.shape
    G, gc, _ = w.shape
    vec = pl.BlockSpec((1, D), lambda i: (0, 0))
    hb = tm // POOL_HALO
    return pl.pallas_call(
        functools.partial(_pool_kernel, seq_len=seq_len),
        grid=(M // tm,),
        in_specs=[
            pl.BlockSpec((tm, D), lambda i: (i, 0)),
            pl.BlockSpec((POOL_HALO, D), lambda i: (jnp.maximum(i * hb - 1, 0), 0)),
            vec,
            _once((G, gc, gc), lambda i: (0, 0, 0)),
            vec,
            vec,
        ],
        out_specs=pl.BlockSpec((tm, D), lambda i: (i, 0)),
        out_shape=jax.ShapeDtypeStruct((M, D), F32),
        compiler_params=_params("parallel"),
        name="pool_layer",
    )(x, x, g0.reshape(1, D), w, scale.reshape(1, D), g1.reshape(1, D))


def _xattn_kernel(h_ref, *refs, pending):
    h = h_ref[...]
    if pending:
        a_ref, ga_ref, *refs = refs
        h = h + _rms(a_ref[...], ga_ref[...])
    g0_ref, wq_ref, k_ref, v_ref, wo_ref, g1_ref, gn_ref, o_ref, n_ref = refs
    n = _rms(h, g0_ref[...]).astype(BF16)
    q = jnp.dot(n, wq_ref[...], preferred_element_type=F32).astype(BF16)
    k = k_ref[0]
    v = v_ref[0]
    scale = HEAD_DIM ** -0.5
    outs = []
    for hh in range(XA_HEADS):
        sl = slice(hh * HEAD_DIM, (hh + 1) * HEAD_DIM)
        s = lax.dot_general(q[:, sl], k[:, sl], _NT, preferred_element_type=F32) * scale
        m = jnp.max(s, axis=-1, keepdims=True)
        e = jnp.exp(s - m)
        p = e * (1.0 / jnp.sum(e, axis=-1, keepdims=True))
        outs.append(jnp.dot(p.astype(BF16), v[:, sl], preferred_element_type=F32))
    o = jnp.concatenate(outs, axis=-1).astype(BF16)
    c = jnp.dot(o, wo_ref[...], preferred_element_type=F32)
    h_new = h + _rms(c, g1_ref[...])
    o_ref[...] = h_new
    n_ref[...] = _rms(h_new, gn_ref[...]).astype(n_ref.dtype)


def xattn_layer(h, g0, wq, kv, wo, g1, g_next, *, seq_len, pending=None, tm=256):
    M, D = h.shape
    xw = wq.shape[1]
    mem_len = kv.shape[1]
    per_seq = seq_len // tm
    vec = pl.BlockSpec((1, D), lambda i: (0, 0))
    row = pl.BlockSpec((tm, D), lambda i: (i, 0))
    pend_specs, pend_args = ([row, vec], [pending[0], pending[1].reshape(1, D)]) if pending is not None else ([], [])
    return pl.pallas_call(
        functools.partial(_xattn_kernel, pending=pending is not None),
        grid=(M // tm,),
        in_specs=[
            row,
            *pend_specs,
            vec,
            _once((D, xw), lambda i: (0, 0)),
            pl.BlockSpec((1, mem_len, xw), lambda i: (i // per_seq, 0, 0)),
            pl.BlockSpec((1, mem_len, xw), lambda i: (i // per_seq, 0, 1)),
            _once((xw, D), lambda i: (0, 0)),
            vec,
            vec,
        ],
        out_specs=[row, row],
        out_shape=[jax.ShapeDtypeStruct((M, D), F32), jax.ShapeDtypeStruct((M, D), BF16)],
        compiler_params=_params("parallel"),
        name="xattn_layer",
    )(h, *pend_args, g0.reshape(1, D), wq, kv, kv, wo, g1.reshape(1, D), g_next.reshape(1, D))


def _ffn_up_kernel(n_ref, halo_ref, wg_ref, wu_ref, cw_ref, cb_ref, o_ref, *, seq_len):
    tm = n_ref.shape[0]
    t0 = (pl.program_id(0) * tm) % seq_len
    a = n_ref[...]
    halo = jnp.where(t0 == 0, 0.0, halo_ref[...]).astype(BF16)
    wg = wg_ref[...].astype(BF16)
    gate = jnp.dot(a, wg, preferred_element_type=F32)
    up = jnp.dot(a, wu_ref[...].astype(BF16), preferred_element_type=F32)
    hg = jnp.dot(halo, wg, preferred_element_type=F32)
    row = lax.broadcasted_iota(jnp.int32, (tm, 1), 0)
    prev1 = hg[CONV_HALO - 1:CONV_HALO]
    prev2 = hg[CONV_HALO - 2:CONV_HALO - 1]
    g1 = jnp.where(row == 0, prev1, pltpu.roll(gate, 1, 0))
    g2 = jnp.where(row == 0, prev2, jnp.where(row == 1, prev1, pltpu.roll(gate, 2, 0)))
    cw = cw_ref[...]
    gc = cw[0:1] * g2 + cw[1:2] * g1 + cw[2:3] * gate + cb_ref[...]
    o_ref[...] = (gc * jax.nn.sigmoid(gc) * up).astype(o_ref.dtype)


def ffn_up(n, w_gu, layer, conv_w, conv_b, *, seq_len, tm=2048, tn=256):
    M, D = n.shape
    F = w_gu.shape[2] // 2
    assert F % tn == 0 and M % tm == 0 and seq_len % tm == 0
    nj = F // tn
    hb = tm // CONV_HALO
    return pl.pallas_call(
        functools.partial(_ffn_up_kernel, seq_len=seq_len),
        grid=(M // tm, nj),
        in_specs=[
            _once((tm, D), lambda i, j: (i, 0)),
            _once((CONV_HALO, D), lambda i, j: (jnp.maximum(i * hb - 1, 0), 0)),
            pl.BlockSpec((None, D, tn), lambda i, j: (layer, 0, j)),
            pl.BlockSpec((None, D, tn), lambda i, j: (layer, 0, j + nj)),
            pl.BlockSpec((conv_w.shape[0], tn), lambda i, j: (0, j)),
            pl.BlockSpec((1, tn), lambda i, j: (0, j)),
        ],
        out_specs=pl.BlockSpec((tm, tn), lambda i, j: (i, j)),
        out_shape=jax.ShapeDtypeStruct((M, F), BF16),
        compiler_params=_params("parallel", "arbitrary"),
        name="ffn_up",
    )(n, n, w_gu, w_gu, conv_w, conv_b.reshape(1, F))


def _gelu_tanh(x):
    return 0.5 * x * (1.0 + jnp.tanh(0.7978845608028654 * (x + 0.044715 * x * x * x)))


def _compress_kernel(r_ref, pos_ref, w1_ref, b1_ref, w2_ref, o_ref):
    r = r_ref[0, 0].astype(F32)
    nch, half = r.shape
    pos = pos_ref[0]
    xa = (r + pos[0:1]).astype(BF16)
    xb = (r + pos[1:2]).astype(BF16)
    h1 = jnp.dot(xa, w1_ref[0, :half, :], preferred_element_type=F32)
    h2 = jnp.dot(xb, w1_ref[0, half:, :], preferred_element_type=F32)
    hid = h1 + pltpu.roll(h2, nch - 1, 0) + b1_ref[0]
    out = jnp.dot(_gelu_tanh(hid).astype(BF16), w2_ref[0], preferred_element_type=F32)
    row = lax.broadcasted_iota(jnp.int32, (nch, 1), 0)
    o_ref[0, 0, 0] = jnp.where(row < nch - 1, out, 0.0).astype(o_ref.dtype)


def compress(kv_slabs, pos, w1, b1, w2, *, batch, seq_len):
    G = NSA_KV_GROUPS
    nch = seq_len // CMP_STRIDE
    half = CMP_STRIDE * HEAD_DIM
    r = kv_slabs[:2 * G].reshape(2 * G, batch, nch, half)
    hidden = w1.shape[-1]
    return pl.pallas_call(
        _compress_kernel,
        grid=(2, batch, G),
        in_specs=[
            pl.BlockSpec((1, 1, nch, half), lambda kv, b, g: (kv * G + g, b, 0, 0)),
            pl.BlockSpec((1, 2, half), lambda kv, b, g: (kv, 0, 0)),
            pl.BlockSpec((1, 2 * half, hidden), lambda kv, b, g: (kv, 0, 0)),
            pl.BlockSpec((1, 1, hidden), lambda kv, b, g: (kv, 0, 0)),
            pl.BlockSpec((1, hidden, HEAD_DIM), lambda kv, b, g: (kv, 0, 0)),
        ],
        out_specs=pl.BlockSpec((1, 1, 1, nch, HEAD_DIM), lambda kv, b, g: (kv, b, g, 0, 0)),
        out_shape=jax.ShapeDtypeStruct((2, batch, G, nch, HEAD_DIM), BF16),
        compiler_params=_params("parallel", "parallel", "parallel"),
        name="nsa_compress",
    )(r, pos.reshape(2, 2, half), w1, b1.reshape(2, 1, hidden), w2)


def _dot_row_halves(p_ref, width, rhs):
    half = p_ref.shape[0] // 2
    return jnp.concatenate([jnp.dot(p_ref[:half, :width], rhs, preferred_element_type=F32),
                            jnp.dot(p_ref[half:, :width], rhs, preferred_element_type=F32)], axis=0)


def _nsa_kernel(slope_ref, q_ref, gate_ref, kc_ref, vc_ref, ks_ref, vs_ref, kw_ref, vw_ref, o_ref,
                qa_scr, ka_scr, va_scr, p_scr, oc_scr, score_scr, mx_scr, trk_scr, acc_scr, used_ref, *, kt):
    J, dh = NSA_HPG, HEAD_DIM
    qb = q_ref.shape[0]
    seq_len = ks_ref.shape[2]
    ncp = kc_ref.shape[3]
    ns = seq_len // SEL_BLOCK
    g = pl.program_id(1)
    qi = pl.program_id(2)
    start = qi * qb

    @pl.when(qi == 0)
    def _():
        chunk = min(512, seq_len)

        def body(c, carry):
            r = pl.multiple_of(c * chunk, chunk)
            ka_scr[pl.ds(r, chunk), :dh] = ks_ref[0, 0, pl.ds(r, chunk), :]
            blk = (r + lax.broadcasted_iota(jnp.int32, (chunk, ns), 0)) // SEL_BLOCK
            ka_scr[pl.ds(r, chunk), dh:] = (blk == lax.broadcasted_iota(jnp.int32, (chunk, ns), 1)).astype(BF16)
            va_scr[pl.ds(r, chunk), :dh] = vs_ref[0, 0, pl.ds(r, chunk), :]
            va_scr[pl.ds(r, chunk), dh:] = jnp.ones((chunk, dh), BF16)
            return carry

        lax.fori_loop(0, seq_len // chunk, body, 0)

    q = q_ref[...]
    for j in range(J):
        qa_scr[j * qb:(j + 1) * qb, :dh] = q[:, j * dh:(j + 1) * dh]
    qr = qa_scr[:, :dh]
    slopes = [slope_ref[g * J + j] for j in range(J)]
    t_col = start + lax.broadcasted_iota(jnp.int32, (qb, 1), 0)

    lanes = mx_scr.shape[1]
    has_key = t_col >= CMP_LEN - 1

    def cmp_branch(ncl):
        kc = kc_ref[0, 0, 0, :ncl, :]
        sc = lax.dot_general(qr, kc, _NT, preferred_element_type=F32)
        c_end = lax.broadcasted_iota(jnp.int32, (1, ncl), 1) * CMP_STRIDE + (CMP_LEN - 1)
        bias_c = jnp.where(t_col >= c_end, 0.0, NEG_BIG)
        rel_c = (c_end - start).astype(F32)
        for j in range(J):
            rows = slice(j * qb, (j + 1) * qb)
            sj = sc[rows] + bias_c + slopes[j] * rel_c
            p_scr[rows, :ncl] = jnp.exp(sj - jnp.max(sj, axis=-1, keepdims=True)).astype(BF16)
        c_row = lax.broadcasted_iota(jnp.int32, (ncl, ns), 0) * CMP_STRIDE
        n_col = lax.broadcasted_iota(jnp.int32, (ncl, ns), 1) * SEL_BLOCK
        overlap = ((c_row < n_col + SEL_BLOCK) & (c_row + CMP_LEN > n_col)
                   & (c_row < (ncp - 1) * CMP_STRIDE)).astype(BF16)
        rhs_c = jnp.concatenate([vc_ref[0, 0, 0, :ncl, :], jnp.ones((ncl, lanes), BF16), overlap], axis=1)
        oc = _dot_row_halves(p_scr, ncl, rhs_c)
        score = jnp.zeros((qb, ns), F32)
        for j in range(J):
            rows = slice(j * qb, (j + 1) * qb)
            inv = jnp.where(has_key, 1.0 / jnp.maximum(oc[rows, dh:dh + lanes], 1e-30), 0.0)
            oc_scr[rows] = oc[rows, :dh] * inv
            score = score + oc[rows, dh + lanes:] * inv[:, :ns]
        score_scr[...] = score

    chunk_tokens = CMP_STRIDE * lanes
    n_chunks = ncp // lanes
    live = jnp.minimum((start + qb - CMP_LEN) // chunk_tokens + 1, n_chunks)
    for k in range(1, n_chunks + 1):
        @pl.when(live == k)
        def _(k=k):
            cmp_branch(k * lanes)
    score = score_scr[...]

    wl = WINDOW + qb
    w0 = pl.multiple_of(jnp.maximum(start - WINDOW, 0), qb)
    kw = kw_ref[0, 0, pl.ds(w0, wl), :]
    vw = jnp.concatenate([vw_ref[0, 0, pl.ds(w0, wl), :], jnp.ones((wl, dh), BF16)], axis=1)
    sw = lax.dot_general(qr, kw, _NT, preferred_element_type=F32)
    pos_w = w0 + lax.broadcasted_iota(jnp.int32, (1, wl), 1)
    bias_w = jnp.where((pos_w <= t_col) & (pos_w > t_col - WINDOW), 0.0, NEG_BIG)
    rel_w = (pos_w - start).astype(F32)
    for j in range(J):
        rows = slice(j * qb, (j + 1) * qb)
        sj = sw[rows] + bias_w + slopes[j] * rel_w
        p_scr[rows, :wl] = jnp.exp(sj - jnp.max(sj, axis=-1, keepdims=True)).astype(BF16)
    ow = _dot_row_halves(p_scr, wl, vw)
    o_win = ow[:, :dh] * (1.0 / jnp.maximum(ow[:, dh:], 1e-30))

    score_t = score.T
    n_idx = lax.broadcasted_iota(jnp.int32, (ns, qb), 0)
    cur = (start + lax.broadcasted_iota(jnp.int32, (1, qb), 1)) // SEL_BLOCK
    forced = (n_idx == 0) | (n_idx == cur) | (n_idx == cur - 1)
    score_t = jnp.where(n_idx > cur, -1.0, jnp.where(forced, FORCED_SCORE, score_t))
    sel_t = jnp.zeros((ns, qb), F32)
    for _ in range(min(N_SELECT, ns)):
        best = jnp.max(score_t, axis=0, keepdims=True)
        first = jnp.min(jnp.where(score_t == best, n_idx, ns), axis=0, keepdims=True)
        hit = n_idx == first
        sel_t = jnp.where(hit, 1.0, sel_t)
        score_t = jnp.where(hit, -2.0, score_t)
    block_bias = jnp.where(sel_t.T > 0.5, 0.0, NEG_BIG).astype(BF16)
    for j in range(J):
        qa_scr[j * qb:(j + 1) * qb, dh:] = block_bias
    bpt = kt // SEL_BLOCK
    for i in range(ns // bpt):
        used_ref[i] = (jnp.max(sel_t[i * bpt:(i + 1) * bpt, :]) > 0.5).astype(jnp.int32)

    pos_d = (start // kt) * kt + lax.broadcasted_iota(jnp.int32, (1, kt), 1)
    causal_bias = jnp.where(pos_d <= t_col, 0.0, NEG_BIG)

    def scores(i, j, s, diagonal):
        pos = i * kt + lax.broadcasted_iota(jnp.int32, (1, kt), 1)
        sj = s[j * qb:(j + 1) * qb] + slopes[j] * (pos - start).astype(F32)
        if diagonal:
            sj = sj + causal_bias
        return sj

    def max_tile(i, diagonal):
        k0 = pl.multiple_of(i * kt, kt)
        s = lax.dot_general(qa_scr[...], ka_scr[pl.ds(k0, kt), :], _NT, preferred_element_type=F32)
        for j in range(J):
            rows = slice(j * qb, (j + 1) * qb)
            sj = scores(i, j, s, diagonal)
            mx = mx_scr[rows]
            for c in range(kt // lanes):
                mx = jnp.maximum(mx, sj[:, c * lanes:(c + 1) * lanes])
            mx_scr[rows] = mx

    def pv_tile(i, diagonal, track=False):
        k0 = pl.multiple_of(i * kt, kt)
        s = lax.dot_general(qa_scr[...], ka_scr[pl.ds(k0, kt), :], _NT, preferred_element_type=F32)
        for j in range(J):
            rows = slice(j * qb, (j + 1) * qb)
            sj = scores(i, j, s, diagonal)
            m = mx_scr[rows]
            excess = trk_scr[rows] if track else None
            for c in range(kt // lanes):
                d = sj[:, c * lanes:(c + 1) * lanes] - m
                p_scr[rows, c * lanes:(c + 1) * lanes] = jnp.exp(d).astype(BF16)
                if track:
                    excess = jnp.maximum(excess, d)
            if track:
                trk_scr[rows] = excess
        acc_scr[...] += _dot_row_halves(p_scr, kt, va_scr[pl.ds(k0, kt), :])

    n_full = start // kt

    def over_used_tiles(tile_fn):
        def full(i, carry):
            @pl.when(used_ref[i] > 0)
            def _():
                tile_fn(i)

            return carry

        lax.fori_loop(0, n_full, full, 0)

    def reduce_row_max():
        for j in range(J):
            rows = slice(j * qb, (j + 1) * qb)
            mx_scr[rows] = jnp.broadcast_to(jnp.max(mx_scr[rows], axis=-1, keepdims=True), (qb, lanes))

    mx_scr[...] = jnp.full(mx_scr.shape, NEG_BIG, F32)
    max_tile(n_full, True)
    reduce_row_max()
    trk_scr[...] = jnp.full(trk_scr.shape, NEG_BIG, F32)
    acc_scr[...] = jnp.zeros(acc_scr.shape, F32)
    over_used_tiles(lambda i: pv_tile(i, False, track=True))
    pv_tile(n_full, True)

    @pl.when(jnp.max(trk_scr[...]) > SEL_EXCESS_LIMIT)
    def _():
        over_used_tiles(lambda i: max_tile(i, False))
        reduce_row_max()
        acc_scr[...] = jnp.zeros(acc_scr.shape, F32)
        over_used_tiles(lambda i: pv_tile(i, False))
        pv_tile(n_full, True)

    gate = gate_ref[0]
    for j in range(J):
        rows = slice(j * qb, (j + 1) * qb)
        o_sel = acc_scr[rows, :dh] * (1.0 / jnp.maximum(acc_scr[rows, dh:], 1e-30))
        o = (gate[:, j:j + 1] * oc_scr[rows] + gate[:, J + j:J + j + 1] * o_sel
             + gate[:, 2 * J + j:2 * J + j + 1] * o_win[rows])
        o_ref[:, j * dh:(j + 1) * dh] = o.astype(o_ref.dtype)


def nsa_attention(q, gates, kv_cmp, kv_slabs, slopes, *, batch, seq_len, qb=256, kt=512):
    G, J, dh = NSA_KV_GROUPS, NSA_HPG, HEAD_DIM
    M = q.shape[0]
    nq = seq_len // qb
    ncp = kv_cmp.shape[3]
    ns = seq_len // SEL_BLOCK
    assert seq_len % kt == 0 and kt % qb == 0 and seq_len >= WINDOW + qb
    slabs = kv_slabs.reshape(kv_slabs.shape[0], batch, seq_len, dh)

    def slab(branch, is_v):
        base = (branch * 2 + is_v) * G
        return pl.BlockSpec((1, 1, seq_len, dh), lambda b, g, i, sl: (base + g, b, 0, 0))

    def cmp_spec(is_v):
        return pl.BlockSpec((1, 1, 1, ncp, dh), lambda b, g, i, sl: (is_v, b, g, 0, 0))

    grid_spec = pltpu.PrefetchScalarGridSpec(
        num_scalar_prefetch=1,
        grid=(batch, G, nq),
        in_specs=[
            pl.BlockSpec((qb, J * dh), lambda b, g, i, sl: (b * nq + i, g)),
            pl.BlockSpec((1, qb, NSA_BRANCHES * J), lambda b, g, i, sl: (g, b * nq + i, 0)),
            cmp_spec(0), cmp_spec(1),
            slab(1, 0), slab(1, 1), slab(2, 0), slab(2, 1),
        ],
        out_specs=pl.BlockSpec((qb, J * dh), lambda b, g, i, sl: (b * nq + i, g)),
        scratch_shapes=[
            pltpu.VMEM((J * qb, dh + ns), BF16),
            pltpu.VMEM((seq_len, dh + ns), BF16),
            pltpu.VMEM((seq_len, 2 * dh), BF16),
            pltpu.VMEM((J * qb, max(ncp, WINDOW + qb, kt)), BF16),
            pltpu.VMEM((J * qb, dh), F32),
            pltpu.VMEM((qb, ns), F32),
            pltpu.VMEM((J * qb, LANES), F32),
            pltpu.VMEM((J * qb, LANES), F32),
            pltpu.VMEM((J * qb, 2 * dh), F32),
            pltpu.SMEM((seq_len // kt,), jnp.int32),
        ],
    )
    return pl.pallas_call(
        functools.partial(_nsa_kernel, kt=kt),
        grid_spec=grid_spec,
        out_shape=jax.ShapeDtypeStruct((M, G * J * dh), BF16),
        compiler_params=_params("parallel", "parallel", "arbitrary"),
        name="nsa_attention",
    )(slopes, q, gates, kv_cmp, kv_cmp, slabs, slabs, slabs, slabs)


def _ffn_layer(h, n, g1, w_gu, w_down, layer, conv_w, conv_b, *, seq_len, g_next=None):
    act = ffn_up(n, w_gu, layer, conv_w, conv_b, seq_len=seq_len)
    f = matmul(act, w_down, layer, tm=1024, tn=512)
    return residual_norm(h, f, g1, g_next)


def _xattn(h, mem_rows, mem_norm, g0, g1, g_next, wq, wkv, wo, layer, *, batch, seq_len, pending=None):
    mem_len = mem_rows.shape[0] // batch
    kv = norm_matmul(mem_rows, mem_norm, wkv, layer, tm=mem_rows.shape[0], tn=wkv.shape[2] // 2, out_dtype=BF16)
    kv = kv.reshape(batch, mem_len, wkv.shape[2])
    return xattn_layer(h, g0, wq[layer], kv, wo[layer], g1, g_next, seq_len=seq_len, pending=pending)


def _nsa_mixer(n, w_in, w_out, layer, cmp_pos, cmp_w1, cmp_b1, cmp_w2, *, batch, seq_len):
    G, J, dh = NSA_KV_GROUPS, NSA_HPG, HEAD_DIM
    qw = G * J * dh
    kvw = NSA_BRANCHES * 2 * G * dh
    w_gate = w_in[layer, :, qw + kvw:].reshape(-1, NSA_BRANCHES, G, J).transpose(0, 2, 1, 3)
    w_gate = w_gate.reshape(-1, G * NSA_BRANCHES * J)
    q, kv_slabs, gates = nsa_in_proj(n, w_in, layer, w_gate, q_width=qw, kv_width=kvw)
    kv_cmp = compress(kv_slabs, cmp_pos, cmp_w1.astype(BF16), cmp_b1, cmp_w2.astype(BF16),
                      batch=batch, seq_len=seq_len)
    n_heads = G * J
    slopes = 2.0 ** (-8.0 * jnp.arange(1, n_heads + 1, dtype=F32) / n_heads)
    o = nsa_attention(q, gates, kv_cmp, kv_slabs, slopes, batch=batch, seq_len=seq_len)
    return matmul(o, w_out, layer, tm=2048, tn=512)


def kernel(x, mem, ln_mix, ln_xa, ln_ffn, mem_norm, pool_w, pool_scale, nsa_w_in, nsa_w_out, nsa_cmp_pos, nsa_cmp_w1, nsa_cmp_b1, nsa_cmp_w2, xa_wq, xa_wkv, xa_wo, ffn_w_gu, ffn_conv_w, ffn_conv_b, ffn_w_down):
    B, S, D = x.shape
    depth = ln_mix.shape[0]
    h = x.reshape(B * S, D)
    mem_rows = mem.reshape(B * mem.shape[1], D)
    nsa_w_in, nsa_w_out = nsa_w_in.astype(BF16), nsa_w_out.astype(BF16)
    xa_wq, xa_wkv, xa_wo = xa_wq.astype(BF16), xa_wkv.astype(BF16), xa_wo.astype(BF16)
    ffn_w_down = ffn_w_down.astype(BF16)
    n_mix = None
    for i in range(depth):
        j = i // 2
        pending = None
        if i % 2 == 0:
            h = pool_layer(h, ln_mix[i, 0], pool_w[j].astype(BF16), pool_scale[j], ln_mix[i, 1], seq_len=S)
        else:
            a = _nsa_mixer(n_mix, nsa_w_in, nsa_w_out, j, nsa_cmp_pos[j],
                           nsa_cmp_w1[j], nsa_cmp_b1[j], nsa_cmp_w2[j], batch=B, seq_len=S)
            pending = (a, ln_mix[i, 1])
        h, n = _xattn(h, mem_rows, mem_norm, ln_xa[i, 0], ln_xa[i, 1], ln_ffn[i, 0], xa_wq, xa_wkv, xa_wo, i,
                      batch=B, seq_len=S, pending=pending)
        nsa_next = i + 1 < depth and (i + 1) % 2 == 1
        out = _ffn_layer(h, n, ln_ffn[i, 1], ffn_w_gu, ffn_w_down, i, ffn_conv_w[i], ffn_conv_b[i], seq_len=S,
                         g_next=ln_mix[i + 1, 0] if nsa_next else None)
        h, n_mix = out if nsa_next else (out, None)
    return h.reshape(B, S, D)
```
